```python
import jax, jax.numpy as jnp
from jax import lax
import numpy as np

D_MODEL = 1024
BATCH = 8
SEQ = 4096
DEPTH = 2

N_BRANCH = 4
MIX_W = 256
SGU_GROUPS = 4
SGU_CHUNK = 128
N_Q_HEADS = 4
N_KV_HEADS = 2
HEAD_DIM = 64
WINDOW = 128
ROT_DIM = HEAD_DIM // 4
ROPE_THETA = 500000.0
SHORT_CONV = 3
CONFORMER_CONV = 31
N_GROUPS = 4
EXPERTS_PER_GROUP = 4
N_EXPERTS = N_GROUPS * EXPERTS_PER_GROUP
TOP_K = 2
D_EXPERT = 256
EPS = 1e-6

IN_WIDTHS = (MIX_W, MIX_W,
             N_Q_HEADS * HEAD_DIM, N_KV_HEADS * HEAD_DIM, N_KV_HEADS * HEAD_DIM,
             MIX_W, MIX_W, MIX_W,
             MIX_W, MIX_W)
D_IN = sum(IN_WIDTHS)

kernel_name = 'hybrid_gated_branches_hier_moe'


def rmsnorm(x, g):
    xf = x.astype(jnp.float32)
    y = xf * lax.rsqrt(jnp.mean(xf * xf, axis=-1, keepdims=True) + EPS)
    return (y * g.astype(jnp.float32)).astype(x.dtype)


def layernorm(x, g, b):
    xf = x.astype(jnp.float32)
    mu = jnp.mean(xf, axis=-1, keepdims=True)
    xc = xf - mu
    var = jnp.mean(xc * xc, axis=-1, keepdims=True)
    y = xc * lax.rsqrt(var + EPS) * g.astype(jnp.float32) + b.astype(jnp.float32)
    return y.astype(x.dtype)


def causal_dwconv(x, w):
    k, c = w.shape
    return lax.conv_general_dilated(x, w[:, None, :], window_strides=(1,), padding=[(k - 1, 0)],
                                    dimension_numbers=('NWC', 'WIO', 'NWC'), feature_group_count=c)


def split_cols(p):
    outs, start = [], 0
    for w in IN_WIDTHS:
        outs.append(p[..., start:start + w])
        start += w
    return outs


def rope_tables(positions):
    inv = ROPE_THETA ** (-jnp.arange(0, ROT_DIM, 2, dtype=jnp.float32) / ROT_DIM)
    ang = positions.astype(jnp.float32)[..., None] * inv
    return jnp.cos(ang)[:, :, None, :], jnp.sin(ang)[:, :, None, :]


def apply_partial_rope(x, cos, sin):
    half = ROT_DIM // 2
    x1 = x[..., :half].astype(jnp.float32)
    x2 = x[..., half:ROT_DIM].astype(jnp.float32)
    rot = jnp.concatenate([x1 * cos - x2 * sin, x2 * cos + x1 * sin], axis=-1).astype(x.dtype)
    return jnp.concatenate([rot, x[..., ROT_DIM:]], axis=-1)


def sgu_mixer(u, v, ln_g, ln_b, w_s, b_s):
    bsz, s, _ = u.shape
    nc = s // SGU_CHUNK
    u = jax.nn.gelu(u)
    v = layernorm(jax.nn.gelu(v), ln_g, ln_b)
    vc = v.reshape(bsz, nc, SGU_CHUNK, SGU_GROUPS, MIX_W // SGU_GROUPS)
    causal = jnp.tril(jnp.ones((SGU_CHUNK, SGU_CHUNK), dtype=bool))
    w = jnp.where(causal[None], w_s, 0)
    z = jnp.einsum('gts,bnsgc->bntgc', w, vc) + b_s.T[None, None, :, :, None]
    return u * z.reshape(bsz, s, MIX_W)


def swa_sink_attention(q, k, v, qn_g, kn_g, sinks, cos, sin):
    bsz, s, _ = q.shape
    nb = s // WINDOW
    grp = N_Q_HEADS // N_KV_HEADS
    q = apply_partial_rope(rmsnorm(q.reshape(bsz, s, N_Q_HEADS, HEAD_DIM), qn_g), cos, sin)
    k = apply_partial_rope(rmsnorm(k.reshape(bsz, s, N_KV_HEADS, HEAD_DIM), kn_g), cos, sin)
    v = v.reshape(bsz, s, N_KV_HEADS, HEAD_DIM)
    qb = q.reshape(bsz, nb, WINDOW, N_KV_HEADS, grp, HEAD_DIM)

    def band(t):
        tb = t.reshape(bsz, nb, WINDOW, N_KV_HEADS, HEAD_DIM)
        prev = jnp.pad(tb, ((0, 0), (1, 0), (0, 0), (0, 0), (0, 0)))[:, :-1]
        return jnp.concatenate([prev, tb], axis=2)

    kb, vb = band(k), band(v)
    sc = jnp.einsum('bnqhgd,bnkhd->bnhgqk', qb, kb, preferred_element_type=jnp.float32) * (HEAD_DIM ** -0.5)
    qi = jnp.arange(WINDOW)[:, None]
    ki = jnp.arange(2 * WINDOW)[None, :]
    diff = qi + WINDOW - ki
    in_win = (diff >= 0) & (diff < WINDOW)
    first = (jnp.arange(nb) == 0)[:, None, None]
    valid = in_win[None] & ~(first & (ki < WINDOW)[None])
    sc = jnp.where(valid[None, :, None, None], sc, -jnp.inf)
    sink = sinks.astype(jnp.float32).reshape(N_KV_HEADS, grp)[None, None, :, :, None, None]
    m = jnp.maximum(jnp.max(sc, axis=-1, keepdims=True), sink)
    p = jnp.exp(sc - m)
    p = p / (jnp.sum(p, axis=-1, keepdims=True) + jnp.exp(sink - m))
    o = jnp.einsum('bnhgqk,bnkhd->bnqhgd', p.astype(vb.dtype), vb)
    return o.reshape(bsz, s, N_Q_HEADS * HEAD_DIM)


def short_conv_mixer(b_gate, c_gate, xin, w3):
    return c_gate * causal_dwconv(b_gate * xin, w3)


def conformer_conv(a, gate, w31, b31, cn_g, cn_b):
    y = a * jax.nn.sigmoid(gate)
    y = causal_dwconv(y, w31) + b31
    return jax.nn.silu(layernorm(y, cn_g, cn_b))


def mixing_block(h, cos, sin, w_in, sgu_ln_g, sgu_ln_b, sgu_w, sgu_b, q_norm_g, k_norm_g, sinks,
                 conv3_w, conv31_w, conv31_b, cnorm_g, cnorm_b, w_branch, w_gate, b_gate, w_o):
    bsz, s, _ = h.shape
    a_u, a_v, b_q, b_k, b_v, c_b, c_c, c_x, d_a, d_g = split_cols(h @ w_in)
    y_a = sgu_mixer(a_u, a_v, sgu_ln_g, sgu_ln_b, sgu_w, sgu_b)
    y_b = swa_sink_attention(b_q, b_k, b_v, q_norm_g, k_norm_g, sinks, cos, sin)
    y_c = short_conv_mixer(c_b, c_c, c_x, conv3_w)
    y_d = conformer_conv(d_a, d_g, conv31_w, conv31_b, cnorm_g, cnorm_b)
    ys = jnp.stack([y_a, y_b, y_c, y_d], axis=2)
    gates = jax.nn.sigmoid(h @ w_gate + b_gate).reshape(bsz, s, N_BRANCH, D_MODEL)
    merged = jnp.einsum('bsnc,ncd,bsnd->bsd', ys, w_branch, gates)
    return merged @ w_o


def hier_moe(h, w_group, b_group, w_expert, b_expert, w_e_gate, w_e_up, w_e_down):
    bsz, s, d = h.shape
    t = h.reshape(-1, d)
    g_logits = (t @ w_group + b_group).astype(jnp.float32)
    g_idx = jnp.argmax(g_logits, axis=-1)
    g_w = jnp.take_along_axis(jax.nn.softmax(g_logits, axis=-1), g_idx[:, None], axis=-1)
    e_logits = (t @ w_expert + b_expert).astype(jnp.float32).reshape(-1, N_GROUPS, EXPERTS_PER_GROUP)
    e_in = jnp.take_along_axis(e_logits, g_idx[:, None, None], axis=1)[:, 0]
    top_v, top_i = lax.top_k(e_in, TOP_K)
    top_w = jax.nn.softmax(top_v, axis=-1) * g_w
    expert_id = g_idx[:, None] * EXPERTS_PER_GROUP + top_i
    combine = jnp.sum(jax.nn.one_hot(expert_id, N_EXPERTS, dtype=jnp.float32) * top_w[..., None], axis=1)
    hid = jax.nn.silu(jnp.einsum('td,edf->tef', t, w_e_gate)) * jnp.einsum('td,edf->tef', t, w_e_up)
    out = jnp.einsum('tef,efd,te->td', hid, w_e_down, combine.astype(t.dtype))
    return out.reshape(bsz, s, d)


def setup_inputs(seed: int = 0) -> dict:
    key = jax.random.key(seed)
    ks = jax.random.split(key, 32)
    L = DEPTH
    f32 = jnp.float32

    def nrm(k, shape, scale):
        return jax.random.normal(k, shape, dtype=f32) * scale

    def gain(k, shape):
        return 1.0 + nrm(k, shape, 0.02)

    return {
        'x': nrm(ks[0], (BATCH, SEQ, D_MODEL), 1.0),
        'positions': jnp.broadcast_to(jnp.arange(SEQ, dtype=jnp.int32)[None], (BATCH, SEQ)),
        'norm1_g': gain(ks[1], (L, D_MODEL)),
        'w_in': nrm(ks[2], (L, D_MODEL, D_IN), D_MODEL ** -0.5),
        'sgu_ln_g': gain(ks[3], (L, MIX_W)),
        'sgu_ln_b': nrm(ks[4], (L, MIX_W), 0.02),
        'sgu_w': nrm(ks[5], (L, SGU_GROUPS, SGU_CHUNK, SGU_CHUNK), SGU_CHUNK ** -0.5),
        'sgu_b': gain(ks[6], (L, SGU_GROUPS, SGU_CHUNK)),
        'q_norm_g': gain(ks[7], (L, HEAD_DIM)),
        'k_norm_g': gain(ks[8], (L, HEAD_DIM)),
        'sinks': nrm(ks[9], (L, N_Q_HEADS), 0.5),
        'conv3_w': nrm(ks[10], (L, SHORT_CONV, MIX_W), SHORT_CONV ** -0.5),
        'conv31_w': nrm(ks[11], (L, CONFORMER_CONV, MIX_W), CONFORMER_CONV ** -0.5),
        'conv31_b': nrm(ks[12], (L, MIX_W), 0.02),
        'cnorm_g': gain(ks[13], (L, MIX_W)),
        'cnorm_b': nrm(ks[14], (L, MIX_W), 0.02),
        'w_branch': nrm(ks[15], (L, N_BRANCH, MIX_W, D_MODEL), MIX_W ** -0.5),
        'w_gate': nrm(ks[16], (L, D_MODEL, N_BRANCH * D_MODEL), D_MODEL ** -0.5),
        'b_gate': nrm(ks[17], (L, N_BRANCH * D_MODEL), 0.02),
        'w_o': nrm(ks[18], (L, D_MODEL, D_MODEL), D_MODEL ** -0.5),
        'norm2_g': gain(ks[19], (L, D_MODEL)),
        'w_group': nrm(ks[20], (L, D_MODEL, N_GROUPS), D_MODEL ** -0.5),
        'b_group': nrm(ks[21], (L, N_GROUPS), 0.01),
        'w_expert': nrm(ks[22], (L, D_MODEL, N_EXPERTS), D_MODEL ** -0.5),
        'b_expert': nrm(ks[23], (L, N_EXPERTS), 0.01),
        'w_e_gate': nrm(ks[24], (L, N_EXPERTS, D_MODEL, D_EXPERT), D_MODEL ** -0.5),
        'w_e_up': nrm(ks[25], (L, N_EXPERTS, D_MODEL, D_EXPERT), D_MODEL ** -0.5),
        'w_e_down': nrm(ks[26], (L, N_EXPERTS, D_EXPERT, D_MODEL), D_EXPERT ** -0.5),
    }


def reference(x, positions, norm1_g, w_in, sgu_ln_g, sgu_ln_b, sgu_w, sgu_b, q_norm_g, k_norm_g, sinks,
              conv3_w, conv31_w, conv31_b, cnorm_g, cnorm_b, w_branch, w_gate, b_gate, w_o,
              norm2_g, w_group, b_group, w_expert, b_expert, w_e_gate, w_e_up, w_e_down):
    cos, sin = rope_tables(positions)
    for l in range(DEPTH):
        h = rmsnorm(x, norm1_g[l])
        x = x + mixing_block(h, cos, sin, w_in[l], sgu_ln_g[l], sgu_ln_b[l], sgu_w[l], sgu_b[l],
                             q_norm_g[l], k_norm_g[l], sinks[l], conv3_w[l], conv31_w[l], conv31_b[l],
                             cnorm_g[l], cnorm_b[l], w_branch[l], w_gate[l], b_gate[l], w_o[l])
        h = rmsnorm(x, norm2_g[l])
        x = x + hier_moe(h, w_group[l], b_group[l], w_expert[l], b_expert[l],
                         w_e_gate[l], w_e_up[l], w_e_down[l])
    return x
```

```python
import functools

import jax
import jax.numpy as jnp
from jax import lax
from jax.experimental import pallas as pl
from jax.experimental.pallas import tpu as pltpu

D_MODEL = 1024
N_BRANCH = 4
MIX_W = 256
SGU_GROUPS = 4
SGU_CHUNK = 128
N_Q_HEADS = 4
N_KV_HEADS = 2
HEAD_DIM = 64
WINDOW = 128
ROT_DIM = HEAD_DIM // 4
ROPE_THETA = 500000.0
SHORT_CONV = 3
CONFORMER_CONV = 31
N_GROUPS = 4
EXPERTS_PER_GROUP = 4
N_EXPERTS = N_GROUPS * EXPERTS_PER_GROUP
D_EXPERT = 256
EPS = 1e-6
D_IN = 2304

LANES = 128
SUBLANES = 8
TM = 512
TM_MOE = 512
C3_HALO = SUBLANES
C31_HALO = 32
CONV_ROWS = 64
VMEM_LIMIT = 56 * 1024 * 1024
NEG = -1e30

F32 = jnp.float32
BF16 = jnp.bfloat16


def _dot(a, b):
    return jnp.dot(a, b, preferred_element_type=F32)


def _gelu_tanh(x):
    c = 0.7978845608028654
    return 0.5 * x * (1.0 + jnp.tanh(c * (x + 0.044715 * (x * x * x))))


def _sigmoid(x):
    return 0.5 * jnp.tanh(0.5 * x) + 0.5


def _layernorm(x, g, b):
    mu = jnp.mean(x, axis=-1, keepdims=True)
    xc = x - mu
    var = jnp.mean(xc * xc, axis=-1, keepdims=True)
    return xc * lax.rsqrt(var + EPS) * g + b


def _head_meansq(t, width):
    r = lax.broadcasted_iota(jnp.int32, (width, width), 0) // HEAD_DIM
    c = lax.broadcasted_iota(jnp.int32, (width, width), 1) // HEAD_DIM
    bd = jnp.where(r == c, 1.0 / HEAD_DIM, 0.0).astype(BF16)
    t2 = t * t
    hi = t2.astype(BF16)
    lo = (t2 - hi.astype(F32)).astype(BF16)
    return _dot(hi, bd) + _dot(lo, bd)


def _rope(t, c, s1, s2):
    w = t.shape[-1]
    half = ROT_DIM // 2
    return t * c + pltpu.roll(t, w - half, axis=1) * s1 + pltpu.roll(t, half, axis=1) * s2


def _mixing_kernel(x_ref, rc_ref, rs1_ref, rs2_ref, n1g_ref, win_ref,
                   sgu_g_ref, sgu_b_ref, sgu_w_ref, sgu_bias_ref,
                   qg_ref, kg_ref, sinks_ref,
                   c3w_ref, c31w_ref, c31b_ref, cng_ref, cnb_ref,
                   wbr_ref, wgate_ref, bgate_ref, wo_ref,
                   out_ref,
                   kprev_ref, vprev_ref, c3buf_ref, c31buf_ref, ys_ref):
    j = pl.program_id(1)

    @pl.when(j == 0)
    def _():
        kprev_ref[...] = jnp.zeros_like(kprev_ref)
        vprev_ref[...] = jnp.zeros_like(vprev_ref)
        c3buf_ref[0:C3_HALO, :] = jnp.zeros((C3_HALO, MIX_W), F32)
        c31buf_ref[0:C31_HALO, :] = jnp.zeros((C31_HALO, MIX_W), F32)

    x = x_ref[...]
    ms = jnp.mean(x * x, axis=-1, keepdims=True)
    hb = (x * lax.rsqrt(ms + EPS) * n1g_ref[...]).astype(BF16)

    uv = _dot(hb, win_ref[:, 0:512])
    u = _gelu_tanh(uv[:, 0:MIX_W])
    v = _layernorm(_gelu_tanh(uv[:, MIX_W:2 * MIX_W]), sgu_g_ref[...], sgu_b_ref[...])
    tt = lax.broadcasted_iota(jnp.int32, (SGU_CHUNK, SGU_GROUPS * SGU_CHUNK), 0)
    ss = lax.broadcasted_iota(jnp.int32, (SGU_CHUNK, SGU_GROUPS * SGU_CHUNK), 1) % SGU_CHUNK
    wcat = jnp.where(ss <= tt, sgu_w_ref[...], 0.0).astype(BF16)
    lane_grp = lax.broadcasted_iota(jnp.int32, (SGU_CHUNK, MIX_W), 1) // (MIX_W // SGU_GROUPS)
    for c in range(TM // SGU_CHUNK):
        rows = slice(c * SGU_CHUNK, (c + 1) * SGU_CHUNK)
        vc = v[rows]
        vstack = jnp.concatenate(
            [jnp.where(lane_grp == g, vc, 0.0) for g in range(SGU_GROUPS)], axis=0).astype(BF16)
        z = _dot(wcat, vstack) + sgu_bias_ref[...]
        ys_ref[0, rows, :] = (u[rows] * z).astype(BF16)

    qkv = _dot(hb, win_ref[:, 512:1024])
    rc, rs1, rs2 = rc_ref[...], rs1_ref[...], rs2_ref[...]
    q = qkv[:, 0:256]
    q = q * lax.rsqrt(_head_meansq(q, 256) + EPS) * qg_ref[...]
    q = _rope(q, jnp.concatenate([rc, rc], axis=1), jnp.concatenate([rs1, rs1], axis=1),
              jnp.concatenate([rs2, rs2], axis=1)) * (HEAD_DIM ** -0.5)
    k = qkv[:, 256:384]
    k = k * lax.rsqrt(_head_meansq(k, 128) + EPS) * kg_ref[...]
    k = _rope(k, rc, rs1, rs2)
    vv = qkv[:, 384:512]
    kfull = jnp.concatenate([kprev_ref[...], k], axis=0)
    vfull = jnp.concatenate([vprev_ref[...], vv], axis=0)
    kprev_ref[...] = k[TM - WINDOW:TM]
    vprev_ref[...] = vv[TM - WINDOW:TM]
    low_full = lax.broadcasted_iota(jnp.int32, (WINDOW + TM, LANES), 1) < HEAD_DIM
    krot = pltpu.roll(kfull, HEAD_DIM, axis=1)
    vrot = pltpu.roll(vfull, HEAD_DIM, axis=1)
    kdup = [jnp.where(low_full, kfull, krot).astype(BF16), jnp.where(low_full, krot, kfull).astype(BF16)]
    vdup = [jnp.where(low_full, vfull, vrot).astype(BF16), jnp.where(low_full, vrot, vfull).astype(BF16)]
    low = lax.broadcasted_iota(jnp.int32, (WINDOW, LANES), 1) < HEAD_DIM
    row2 = lax.broadcasted_iota(jnp.int32, (2 * WINDOW, 2 * WINDOW), 0)
    qi = row2 % WINDOW
    ki = lax.broadcasted_iota(jnp.int32, (2 * WINDOW, 2 * WINDOW), 1)
    in_prev = (ki < WINDOW) & (ki > qi)
    in_cur = (ki >= WINDOW) & (ki - WINDOW <= qi)
    is_g0 = lax.broadcasted_iota(jnp.int32, (2 * WINDOW, 1), 0) < WINDOW
    for n in range(TM // WINDOW):
        if n == 0:
            valid = (in_prev & (j > 0)) | in_cur
        else:
            valid = in_prev | in_cur
        cols = []
        for h in range(N_KV_HEADS):
            qcol = q[n * WINDOW:(n + 1) * WINDOW, h * LANES:(h + 1) * LANES]
            qs = jnp.concatenate([jnp.where(low, qcol, 0.0), jnp.where(low, 0.0, qcol)], axis=0).astype(BF16)
            kk = kdup[h][n * WINDOW:(n + 2) * WINDOW]
            sc = lax.dot_general(qs, kk, (((1,), (1,)), ((), ())), preferred_element_type=F32)
            sc = jnp.where(valid, sc, NEG)
            sink = jnp.where(is_g0, sinks_ref[2 * h], sinks_ref[2 * h + 1])
            m = jnp.maximum(jnp.max(sc, axis=-1, keepdims=True), sink)
            p = jnp.exp(sc - m)
            denom = jnp.sum(p, axis=-1, keepdims=True) + jnp.exp(sink - m)
            o = _dot(p.astype(BF16), vdup[h][n * WINDOW:(n + 2) * WINDOW]) / denom
            cols.append(jnp.where(low, o[0:WINDOW], o[WINDOW:2 * WINDOW]))
        ys_ref[1, n * WINDOW:(n + 1) * WINDOW, :] = jnp.concatenate(cols, axis=1).astype(BF16)

    cc = _dot(hb, win_ref[:, 1024:1792])
    c3buf_ref[C3_HALO:C3_HALO + TM, :] = cc[:, 0:256] * cc[:, 512:768]
    conv = c3w_ref[0:1, :] * c3buf_ref[C3_HALO - 2:C3_HALO - 2 + TM, :]
    conv = conv + c3w_ref[1:2, :] * c3buf_ref[C3_HALO - 1:C3_HALO - 1 + TM, :]
    conv = conv + c3w_ref[2:3, :] * c3buf_ref[C3_HALO:C3_HALO + TM, :]
    ys_ref[2] = (cc[:, 256:512] * conv).astype(BF16)
    c3buf_ref[0:C3_HALO, :] = c3buf_ref[TM:TM + C3_HALO, :]

    dd = _dot(hb, win_ref[:, 1792:2304])
    c31buf_ref[C31_HALO:C31_HALO + TM, :] = dd[:, 0:256] * _sigmoid(dd[:, 256:512])
    base = C31_HALO - (CONFORMER_CONV - 1)
    for r in range(TM // CONV_ROWS):
        acc = jnp.broadcast_to(c31b_ref[...], (CONV_ROWS, MIX_W))
        for tap in range(CONFORMER_CONV):
            start = base + tap + r * CONV_ROWS
            acc = acc + c31w_ref[tap:tap + 1, :] * c31buf_ref[start:start + CONV_ROWS, :]
        yn = _layernorm(acc, cng_ref[...], cnb_ref[...])
        ys_ref[3, r * CONV_ROWS:(r + 1) * CONV_ROWS, :] = (yn * _sigmoid(yn)).astype(BF16)
    c31buf_ref[0:C31_HALO, :] = c31buf_ref[TM:TM + C31_HALO, :]

    merged = jnp.zeros((TM, D_MODEL), F32)
    for b in range(N_BRANCH):
        yb = _dot(ys_ref[b], wbr_ref[b])
        gate = _dot(hb, wgate_ref[:, b * D_MODEL:(b + 1) * D_MODEL]) + bgate_ref[:, b * D_MODEL:(b + 1) * D_MODEL]
        merged = merged + _sigmoid(gate) * yb
    out_ref[...] = x + _dot(merged.astype(BF16), wo_ref[...])


def _const_spec(shape):
    zeros = (0,) * len(shape)
    return pl.BlockSpec(shape, lambda b, j: zeros, pipeline_mode=pl.Buffered(1))


def _mixing_call(x, rc, rs1, rs2, n1g, w_in, sgu_g, sgu_b, sgu_w, sgu_bias, qg, kg, sinks,
                 c3w, c31w, c31b, cng, cnb, wbr, wgate, bgate, wo):
    bsz, seq, _ = x.shape
    tok = lambda width: pl.BlockSpec((None, TM, width), lambda b, j: (b, j, 0))
    in_specs = [
        tok(D_MODEL), tok(LANES), tok(LANES), tok(LANES),
        _const_spec((1, D_MODEL)), _const_spec((D_MODEL, D_IN)),
        _const_spec((1, MIX_W)), _const_spec((1, MIX_W)),
        _const_spec((SGU_CHUNK, SGU_GROUPS * SGU_CHUNK)), _const_spec((SGU_CHUNK, MIX_W)),
        _const_spec((1, N_Q_HEADS * HEAD_DIM)), _const_spec((1, N_KV_HEADS * HEAD_DIM)),
        pl.BlockSpec(memory_space=pltpu.SMEM),
        _const_spec((SUBLANES, MIX_W)), _const_spec((C31_HALO, MIX_W)), _const_spec((1, MIX_W)),
        _const_spec((1, MIX_W)), _const_spec((1, MIX_W)),
        _const_spec((N_BRANCH, MIX_W, D_MODEL)), _const_spec((D_MODEL, N_BRANCH * D_MODEL)),
        _const_spec((1, N_BRANCH * D_MODEL)), _const_spec((D_MODEL, D_MODEL)),
    ]
    return pl.pallas_call(
        _mixing_kernel,
        grid=(bsz, seq // TM),
        in_specs=in_specs,
        out_specs=tok(D_MODEL),
        out_shape=jax.ShapeDtypeStruct(x.shape, F32),
        scratch_shapes=[
            pltpu.VMEM((WINDOW, LANES), F32), pltpu.VMEM((WINDOW, LANES), F32),
            pltpu.VMEM((C3_HALO + TM, MIX_W), F32), pltpu.VMEM((C31_HALO + TM, MIX_W), F32),
            pltpu.VMEM((N_BRANCH, TM, MIX_W), BF16),
        ],
        compiler_params=pltpu.CompilerParams(
            dimension_semantics=("arbitrary", "arbitrary"), vmem_limit_bytes=VMEM_LIMIT),
        name="mixing_block",
    )(x, rc, rs1, rs2, n1g, w_in, sgu_g, sgu_b, sgu_w, sgu_bias, qg, kg, sinks,
      c3w, c31w, c31b, cng, cnb, wbr, wgate, bgate, wo)


def _route(logits_t):
    g = [logits_t[i:i + 1, :] for i in range(N_GROUPS)]
    gmax, gidx = g[0], jnp.zeros_like(g[0], dtype=jnp.int32)
    for i in range(1, N_GROUPS):
        better = g[i] > gmax
        gmax = jnp.where(better, g[i], gmax)
        gidx = jnp.where(better, i, gidx)
    gsum = sum(jnp.exp(gi - gmax) for gi in g)
    g_w = 1.0 / gsum
    e = []
    for i in range(EXPERTS_PER_GROUP):
        ei = logits_t[8 + i:9 + i, :]
        for grp in range(1, N_GROUPS):
            row = 8 + grp * EXPERTS_PER_GROUP + i
            ei = jnp.where(gidx == grp, logits_t[row:row + 1, :], ei)
        e.append(ei)
    v1, i1 = e[0], jnp.zeros_like(gidx)
    for i in range(1, EXPERTS_PER_GROUP):
        better = e[i] > v1
        v1 = jnp.where(better, e[i], v1)
        i1 = jnp.where(better, i, i1)
    v2, i2 = jnp.full_like(v1, -jnp.inf), jnp.zeros_like(gidx)
    for i in range(EXPERTS_PER_GROUP):
        better = (e[i] > v2) & (i1 != i)
        v2 = jnp.where(better, e[i], v2)
        i2 = jnp.where(better, i, i2)
    t = jnp.exp(v2 - v1)
    w1 = g_w / (1.0 + t)
    w2 = g_w * t / (1.0 + t)
    return gidx * EXPERTS_PER_GROUP + i1, gidx * EXPERTS_PER_GROUP + i2, w1, w2


def _moe_kernel(x_ref, n2g_ref, wr_ref, br_ref, wg_ref, wu_ref, wd_ref, out_ref, h_ref, comb_ref, acc_ref):
    e = pl.program_id(1)

    @pl.when(e == 0)
    def _():
        x = x_ref[...]
        ms = jnp.mean(x * x, axis=-1, keepdims=True)
        hb = (x * lax.rsqrt(ms + EPS) * n2g_ref[...]).astype(BF16)
        h_ref[...] = hb
        acc_ref[...] = x
        logits = _dot(hb, wr_ref[...]) + br_ref[...]
        lt = logits.T
        e1, e2, w1, w2 = _route(lt)
        rows = lax.broadcasted_iota(jnp.int32, (LANES, TM_MOE), 0)
        comb_t = jnp.where(rows == e1, w1, 0.0) + jnp.where(rows == e2, w2, 0.0)
        comb_ref[...] = comb_t.T

    hb = h_ref[...]
    hid = _dot(hb, wg_ref[...])
    hid = hid * _sigmoid(hid) * _dot(hb, wu_ref[...])
    lane = lax.broadcasted_iota(jnp.int32, (TM_MOE, LANES), 1)
    col = jnp.sum(jnp.where(lane == e, comb_ref[...], 0.0), axis=-1, keepdims=True)
    acc_ref[...] += _dot((hid * col).astype(BF16), wd_ref[...])

    @pl.when(e == N_EXPERTS - 1)
    def _():
        out_ref[...] = acc_ref[...]


def _moe_call(x2d, n2g, wr, br, wg, wu, wd):
    ntok = x2d.shape[0]
    return pl.pallas_call(
        _moe_kernel,
        grid=(ntok // TM_MOE, N_EXPERTS),
        in_specs=[
            pl.BlockSpec((TM_MOE, D_MODEL), lambda i, e: (i, 0)),
            pl.BlockSpec((1, D_MODEL), lambda i, e: (0, 0)),
            pl.BlockSpec((D_MODEL, LANES), lambda i, e: (0, 0)),
            pl.BlockSpec((1, LANES), lambda i, e: (0, 0)),
            pl.BlockSpec((None, D_MODEL, D_EXPERT), lambda i, e: (e, 0, 0)),
            pl.BlockSpec((None, D_MODEL, D_EXPERT), lambda i, e: (e, 0, 0)),
            pl.BlockSpec((None, D_EXPERT, D_MODEL), lambda i, e: (e, 0, 0)),
        ],
        out_specs=pl.BlockSpec((TM_MOE, D_MODEL), lambda i, e: (i, 0)),
        out_shape=jax.ShapeDtypeStruct(x2d.shape, F32),
        scratch_shapes=[
            pltpu.VMEM((TM_MOE, D_MODEL), BF16), pltpu.VMEM((TM_MOE, LANES), F32),
            pltpu.VMEM((TM_MOE, D_MODEL), F32),
        ],
        compiler_params=pltpu.CompilerParams(
            dimension_semantics=("arbitrary", "arbitrary"), vmem_limit_bytes=VMEM_LIMIT),
        name="hier_moe",
    )(x2d, n2g, wr, br, wg, wu, wd)


def _rope_tables(positions):
    inv = ROPE_THETA ** (-jnp.arange(0, ROT_DIM, 2, dtype=F32) / ROT_DIM)
    ang = positions.astype(F32)[..., None] * inv
    cos, sin = jnp.cos(ang), jnp.sin(ang)
    rest = HEAD_DIM - ROT_DIM
    half = ROT_DIM // 2
    one = jnp.ones(ang.shape[:-1] + (rest,), F32)
    zero = jnp.zeros(ang.shape[:-1] + (rest,), F32)
    zh = jnp.zeros_like(sin)
    c = jnp.concatenate([cos, cos, one], axis=-1)
    s1 = jnp.concatenate([-sin, zh, zero], axis=-1)
    s2 = jnp.concatenate([zh, sin, zero], axis=-1)
    two = lambda t: jnp.concatenate([t, t], axis=-1)
    return two(c), two(s1), two(s2)


def kernel(x, positions, norm1_g, w_in, sgu_ln_g, sgu_ln_b, sgu_w, sgu_b, q_norm_g, k_norm_g, sinks, conv3_w, conv31_w, conv31_b, cnorm_g, cnorm_b, w_branch, w_gate, b_gate, w_o, norm2_g, w_group, b_group, w_expert, b_expert, w_e_gate, w_e_up, w_e_down):
    bsz, seq, d = x.shape
    depth = norm1_g.shape[0]
    rc, rs1, rs2 = _rope_tables(positions)
    row = lambda t: t.reshape(1, -1)
    for l in range(depth):
        sgu_wcat = jnp.transpose(sgu_w[l], (1, 0, 2)).reshape(SGU_CHUNK, SGU_GROUPS * SGU_CHUNK)
        sgu_bias = jnp.repeat(sgu_b[l].T, MIX_W // SGU_GROUPS, axis=1)
        c3w = jnp.zeros((SUBLANES, MIX_W), F32).at[:SHORT_CONV].set(conv3_w[l])
        c31w = jnp.zeros((C31_HALO, MIX_W), F32).at[:CONFORMER_CONV].set(conv31_w[l])
        x = _mixing_call(
            x, rc, rs1, rs2, row(norm1_g[l]), w_in[l].astype(BF16),
            row(sgu_ln_g[l]), row(sgu_ln_b[l]), sgu_wcat, sgu_bias,
            row(jnp.tile(q_norm_g[l], N_Q_HEADS)), row(jnp.tile(k_norm_g[l], N_KV_HEADS)), sinks[l],
            c3w, c31w, row(conv31_b[l]), row(cnorm_g[l]), row(cnorm_b[l]),
            w_branch[l].astype(BF16), w_gate[l].astype(BF16), row(b_gate[l]), w_o[l].astype(BF16))
        wr = jnp.zeros((d, LANES), F32).at[:, 0:N_GROUPS].set(w_group[l]).at[:, 8:8 + N_EXPERTS].set(w_expert[l])
        br = jnp.zeros((1, LANES), F32).at[0, 0:N_GROUPS].set(b_group[l]).at[0, 8:8 + N_EXPERTS].set(b_expert[l])
        x = _moe_call(
            x.reshape(bsz * seq, d), row(norm2_g[l]), wr.astype(BF16), br,
            w_e_gate[l].astype(BF16), w_e_up[l].astype(BF16), w_e_down[l].astype(BF16)).reshape(bsz, seq, d)
    return x
```

```python
import functools

import jax
import jax.numpy as jnp
from jax import lax
from jax.experimental import pallas as pl
from jax.experimental.pallas import tpu as pltpu

D_MODEL = 1024
N_BRANCH = 4
MIX_W = 256
SGU_GROUPS = 4
SGU_CHUNK = 128
N_Q_HEADS = 4
N_KV_HEADS = 2
HEAD_DIM = 64
WINDOW = 128
ROT_DIM = HEAD_DIM // 4
ROPE_THETA = 500000.0
SHORT_CONV = 3
CONFORMER_CONV = 31
N_GROUPS = 4
EXPERTS_PER_GROUP = 4
N_EXPERTS = N_GROUPS * EXPERTS_PER_GROUP
D_EXPERT = 256
EPS = 1e-6
D_IN = 2304

BATCH = 8
SEQ = 4096
PAIRS_PER_GROUP = EXPERTS_PER_GROUP * (EXPERTS_PER_GROUP - 1) // 2
N_BUCKETS = N_GROUPS * PAIRS_PER_GROUP
NB_PAD = 32

LANES = 128
SUBLANES = 8
TM = 512
TMS = 256
NT_MAX = BATCH * SEQ // TMS + N_BUCKETS
TB_LEN = 256
ISSUE_UNROLL = 8
WAIT_ROWS = 128
C3_HALO = SUBLANES
C31_HALO = 32
CONV_ROWS = 64
VMEM_LIMIT = 56 * 1024 * 1024
NEG = -1e30

F32 = jnp.float32
BF16 = jnp.bfloat16


def _dot(a, b):
    return jnp.dot(a, b, preferred_element_type=F32)


def _gelu_tanh(x):
    c = 0.7978845608028654
    return 0.5 * x * (1.0 + jnp.tanh(c * (x + 0.044715 * (x * x * x))))


def _sigmoid(x):
    return 0.5 * jnp.tanh(0.5 * x) + 0.5


def _layernorm(x, g, b):
    mu = jnp.mean(x, axis=-1, keepdims=True)
    xc = x - mu
    var = jnp.mean(xc * xc, axis=-1, keepdims=True)
    return xc * lax.rsqrt(var + EPS) * g + b


def _head_meansq(t, width):
    r = lax.broadcasted_iota(jnp.int32, (width, width), 0) // HEAD_DIM
    c = lax.broadcasted_iota(jnp.int32, (width, width), 1) // HEAD_DIM
    bd = jnp.where(r == c, 1.0 / HEAD_DIM, 0.0).astype(BF16)
    t2 = t * t
    hi = t2.astype(BF16)
    lo = (t2 - hi.astype(F32)).astype(BF16)
    return _dot(hi, bd) + _dot(lo, bd)


def _rope(t, c, s1, s2):
    w = t.shape[-1]
    half = ROT_DIM // 2
    return t * c + pltpu.roll(t, w - half, axis=1) * s1 + pltpu.roll(t, half, axis=1) * s2


def _mixing_kernel(x_ref, cos_ref, sin_ref, n1g_ref, win_ref,
                   sgu_g_ref, sgu_b_ref, sgu_w_ref, sgu_bias_ref,
                   qg_ref, kg_ref, sinks_ref,
                   c3w_ref, c31w_ref, c31b_ref, cng_ref, cnb_ref,
                   wbr_ref, wgate_ref, bgate_ref, wo_ref,
                   n2g_ref, wr_ref, br_ref,
                   out_ref, bucket_ref,
                   kprev_ref, vprev_ref, c3buf_ref, c31buf_ref, ys_ref):
    j = pl.program_id(1)

    @pl.when(j == 0)
    def _():
        kprev_ref[...] = jnp.zeros_like(kprev_ref)
        vprev_ref[...] = jnp.zeros_like(vprev_ref)
        c3buf_ref[0:C3_HALO, :] = jnp.zeros((C3_HALO, MIX_W), F32)
        c31buf_ref[0:C31_HALO, :] = jnp.zeros((C31_HALO, MIX_W), F32)

    x = x_ref[...]
    ms = jnp.mean(x * x, axis=-1, keepdims=True)
    hb = (x * lax.rsqrt(ms + EPS) * n1g_ref[...]).astype(BF16)

    uv = _dot(hb, win_ref[:, 0:512])
    u = _gelu_tanh(uv[:, 0:MIX_W])
    v = _layernorm(_gelu_tanh(uv[:, MIX_W:2 * MIX_W]), sgu_g_ref[...], sgu_b_ref[...])
    tt = lax.broadcasted_iota(jnp.int32, (SGU_CHUNK, SGU_GROUPS * SGU_CHUNK), 0)
    ss = lax.broadcasted_iota(jnp.int32, (SGU_CHUNK, SGU_GROUPS * SGU_CHUNK), 1) % SGU_CHUNK
    wcat = jnp.where(ss <= tt, sgu_w_ref[...], 0.0).astype(BF16)
    lane_grp = lax.broadcasted_iota(jnp.int32, (SGU_CHUNK, MIX_W), 1) // (MIX_W // SGU_GROUPS)
    for c in range(TM // SGU_CHUNK):
        rows = slice(c * SGU_CHUNK, (c + 1) * SGU_CHUNK)
        vc = v[rows]
        vstack = jnp.concatenate(
            [jnp.where(lane_grp == g, vc, 0.0) for g in range(SGU_GROUPS)], axis=0).astype(BF16)
        z = _dot(wcat, vstack) + sgu_bias_ref[...]
        ys_ref[0, rows, :] = (u[rows] * z).astype(BF16)

    qkv = _dot(hb, win_ref[:, 512:1024])
    rc, sin_t = cos_ref[...], sin_ref[...]
    head_dim_idx = lax.broadcasted_iota(jnp.int32, (1, LANES), 1) % HEAD_DIM
    rs1 = jnp.where(head_dim_idx < ROT_DIM // 2, -sin_t, 0.0)
    rs2 = jnp.where((head_dim_idx >= ROT_DIM // 2) & (head_dim_idx < ROT_DIM), sin_t, 0.0)
    q = qkv[:, 0:256]
    q = q * lax.rsqrt(_head_meansq(q, 256) + EPS) * qg_ref[...]
    q = _rope(q, jnp.concatenate([rc, rc], axis=1), jnp.concatenate([rs1, rs1], axis=1),
              jnp.concatenate([rs2, rs2], axis=1)) * (HEAD_DIM ** -0.5)
    k = qkv[:, 256:384]
    k = k * lax.rsqrt(_head_meansq(k, 128) + EPS) * kg_ref[...]
    k = _rope(k, rc, rs1, rs2)
    vv = qkv[:, 384:512]
    kfull = jnp.concatenate([kprev_ref[...], k], axis=0)
    vfull = jnp.concatenate([vprev_ref[...], vv], axis=0)
    kprev_ref[...] = k[TM - WINDOW:TM]
    vprev_ref[...] = vv[TM - WINDOW:TM]
    low_full = lax.broadcasted_iota(jnp.int32, (WINDOW + TM, LANES), 1) < HEAD_DIM
    krot = pltpu.roll(kfull, HEAD_DIM, axis=1)
    vrot = pltpu.roll(vfull, HEAD_DIM, axis=1)
    kdup = [jnp.where(low_full, kfull, krot).astype(BF16), jnp.where(low_full, krot, kfull).astype(BF16)]
    vdup = [jnp.where(low_full, vfull, vrot).astype(BF16), jnp.where(low_full, vrot, vfull).astype(BF16)]
    low = lax.broadcasted_iota(jnp.int32, (WINDOW, LANES), 1) < HEAD_DIM
    row2 = lax.broadcasted_iota(jnp.int32, (2 * WINDOW, 2 * WINDOW), 0)
    qi = row2 % WINDOW
    ki = lax.broadcasted_iota(jnp.int32, (2 * WINDOW, 2 * WINDOW), 1)
    in_prev = (ki < WINDOW) & (ki > qi)
    in_cur = (ki >= WINDOW) & (ki - WINDOW <= qi)
    is_g0 = lax.broadcasted_iota(jnp.int32, (2 * WINDOW, 1), 0) < WINDOW
    for n in range(TM // WINDOW):
        if n == 0:
            valid = (in_prev & (j > 0)) | in_cur
        else:
            valid = in_prev | in_cur
        cols = []
        for h in range(N_KV_HEADS):
            qcol = q[n * WINDOW:(n + 1) * WINDOW, h * LANES:(h + 1) * LANES]
            qs = jnp.concatenate([jnp.where(low, qcol, 0.0), jnp.where(low, 0.0, qcol)], axis=0).astype(BF16)
            kk = kdup[h][n * WINDOW:(n + 2) * WINDOW]
            sc = lax.dot_general(qs, kk, (((1,), (1,)), ((), ())), preferred_element_type=F32)
            sc = jnp.where(valid, sc, NEG)
            sink = jnp.where(is_g0, sinks_ref[2 * h], sinks_ref[2 * h + 1])
            m = jnp.maximum(jnp.max(sc, axis=-1, keepdims=True), sink)
            p = jnp.exp(sc - m)
            denom = jnp.sum(p, axis=-1, keepdims=True) + jnp.exp(sink - m)
            o = _dot(p.astype(BF16), vdup[h][n * WINDOW:(n + 2) * WINDOW]) / denom
            cols.append(jnp.where(low, o[0:WINDOW], o[WINDOW:2 * WINDOW]))
        ys_ref[1, n * WINDOW:(n + 1) * WINDOW, :] = jnp.concatenate(cols, axis=1).astype(BF16)

    cc = _dot(hb, win_ref[:, 1024:1792])
    c3buf_ref[C3_HALO:C3_HALO + TM, :] = cc[:, 0:256] * cc[:, 512:768]
    conv = c3w_ref[0:1, :] * c3buf_ref[C3_HALO - 2:C3_HALO - 2 + TM, :]
    conv = conv + c3w_ref[1:2, :] * c3buf_ref[C3_HALO - 1:C3_HALO - 1 + TM, :]
    conv = conv + c3w_ref[2:3, :] * c3buf_ref[C3_HALO:C3_HALO + TM, :]
    ys_ref[2] = (cc[:, 256:512] * conv).astype(BF16)
    c3buf_ref[0:C3_HALO, :] = c3buf_ref[TM:TM + C3_HALO, :]

    dd = _dot(hb, win_ref[:, 1792:2304])
    c31buf_ref[C31_HALO:C31_HALO + TM, :] = dd[:, 0:256] * _sigmoid(dd[:, 256:512])
    base = C31_HALO - (CONFORMER_CONV - 1)
    for r in range(TM // CONV_ROWS):
        acc = jnp.broadcast_to(c31b_ref[...], (CONV_ROWS, MIX_W))
        for tap in range(CONFORMER_CONV):
            start = base + tap + r * CONV_ROWS
            acc = acc + c31w_ref[tap:tap + 1, :] * c31buf_ref[start:start + CONV_ROWS, :]
        yn = _layernorm(acc, cng_ref[...], cnb_ref[...])
        ys_ref[3, r * CONV_ROWS:(r + 1) * CONV_ROWS, :] = (yn * _sigmoid(yn)).astype(BF16)
    c31buf_ref[0:C31_HALO, :] = c31buf_ref[TM:TM + C31_HALO, :]

    merged = jnp.zeros((TM, D_MODEL), F32)
    for b in range(N_BRANCH):
        yb = _dot(ys_ref[b], wbr_ref[b])
        gate = _dot(hb, wgate_ref[:, b * D_MODEL:(b + 1) * D_MODEL]) + bgate_ref[:, b * D_MODEL:(b + 1) * D_MODEL]
        merged = merged + _sigmoid(gate) * yb
    xo = x + _dot(merged.astype(BF16), wo_ref[...])
    out_ref[...] = xo

    ms2 = jnp.mean(xo * xo, axis=-1, keepdims=True)
    h2 = (xo * lax.rsqrt(ms2 + EPS) * n2g_ref[...]).astype(BF16)
    logits = _dot(h2, wr_ref[...]) + br_ref[...]
    bucket_ref[...] = _route_bucket(logits.T)


def _const_spec(shape):
    zeros = (0,) * len(shape)
    return pl.BlockSpec(shape, lambda b, j: zeros, pipeline_mode=pl.Buffered(1))


def _mixing_call(x, cos_t, sin_t, n1g, w_in, sgu_g, sgu_b, sgu_w, sgu_bias, qg, kg, sinks,
                 c3w, c31w, c31b, cng, cnb, wbr, wgate, bgate, wo, n2g, wr, br):
    bsz, seq, _ = x.shape
    nj = seq // TM
    tok = lambda width: pl.BlockSpec((None, TM, width), lambda b, j: (b, j, 0))
    in_specs = [
        tok(D_MODEL), tok(LANES), tok(LANES),
        _const_spec((1, D_MODEL)), _const_spec((D_MODEL, D_IN)),
        _const_spec((1, MIX_W)), _const_spec((1, MIX_W)),
        _const_spec((SGU_CHUNK, SGU_GROUPS * SGU_CHUNK)), _const_spec((SGU_CHUNK, MIX_W)),
        _const_spec((1, N_Q_HEADS * HEAD_DIM)), _const_spec((1, N_KV_HEADS * HEAD_DIM)),
        pl.BlockSpec(memory_space=pltpu.SMEM),
        _const_spec((SUBLANES, MIX_W)), _const_spec((C31_HALO, MIX_W)), _const_spec((1, MIX_W)),
        _const_spec((1, MIX_W)), _const_spec((1, MIX_W)),
        _const_spec((N_BRANCH, MIX_W, D_MODEL)), _const_spec((D_MODEL, N_BRANCH * D_MODEL)),
        _const_spec((1, N_BRANCH * D_MODEL)), _const_spec((D_MODEL, D_MODEL)),
        _const_spec((1, D_MODEL)), _const_spec((D_MODEL, LANES)), _const_spec((1, LANES)),
    ]
    return pl.pallas_call(
        _mixing_kernel,
        grid=(bsz, nj),
        in_specs=in_specs,
        out_specs=[tok(D_MODEL), pl.BlockSpec((None, 1, TM), lambda b, j: (b * nj + j, 0, 0))],
        out_shape=[jax.ShapeDtypeStruct(x.shape, F32),
                   jax.ShapeDtypeStruct((bsz * nj, 1, TM), jnp.int32)],
        scratch_shapes=[
            pltpu.VMEM((WINDOW, LANES), F32), pltpu.VMEM((WINDOW, LANES), F32),
            pltpu.VMEM((C3_HALO + TM, MIX_W), F32), pltpu.VMEM((C31_HALO + TM, MIX_W), F32),
            pltpu.VMEM((N_BRANCH, TM, MIX_W), BF16),
        ],
        compiler_params=pltpu.CompilerParams(
            dimension_semantics=("arbitrary", "arbitrary"), vmem_limit_bytes=VMEM_LIMIT),
        name="mixing_block",
    )(x, cos_t, sin_t, n1g, w_in, sgu_g, sgu_b, sgu_w, sgu_bias, qg, kg, sinks,
      c3w, c31w, c31b, cng, cnb, wbr, wgate, bgate, wo, n2g, wr, br)


def _route_bucket(logits_t):
    g = [logits_t[i:i + 1, :] for i in range(N_GROUPS)]
    gmax, gidx = g[0], jnp.zeros(g[0].shape, jnp.int32)
    for i in range(1, N_GROUPS):
        better = g[i] > gmax
        gmax = jnp.where(better, g[i], gmax)
        gidx = jnp.where(better, i, gidx)
    e = []
    for i in range(EXPERTS_PER_GROUP):
        ei = logits_t[8 + i:9 + i, :]
        for grp in range(1, N_GROUPS):
            row = 8 + grp * EXPERTS_PER_GROUP + i
            ei = jnp.where(gidx == grp, logits_t[row:row + 1, :], ei)
        e.append(ei)
    v1, i1 = e[0], jnp.zeros_like(gidx)
    for i in range(1, EXPERTS_PER_GROUP):
        better = e[i] > v1
        v1 = jnp.where(better, e[i], v1)
        i1 = jnp.where(better, i, i1)
    v2, i2 = jnp.full_like(v1, -jnp.inf), jnp.zeros_like(gidx)
    for i in range(EXPERTS_PER_GROUP):
        better = (e[i] > v2) & (i1 != i)
        v2 = jnp.where(better, e[i], v2)
        i2 = jnp.where(better, i, i2)
    lo, hi = jnp.minimum(i1, i2), jnp.maximum(i1, i2)
    pair = jnp.where(lo == 0, hi - 1, jnp.where(lo == 1, hi + 1, 5))
    return gidx * PAIRS_PER_GROUP + pair


def _bucket_experts(b):
    g = b // PAIRS_PER_GROUP
    pair = b - g * PAIRS_PER_GROUP
    lo = jnp.where(pair < 3, 0, jnp.where(pair < 5, 1, 2))
    hi = jnp.where(pair < 3, pair + 1, jnp.where(pair < 5, pair - 1, 3))
    return g, g * EXPERTS_PER_GROUP + lo, g * EXPERTS_PER_GROUP + hi


def _sort_kernel(bucket_ref, pos_ref, tb_ref, meta_ref, carry_ref, base_ref):
    p = pl.program_id(0)
    j = pl.program_id(1)

    @pl.when((p == 0) & (j == 0))
    def _():
        carry_ref[...] = jnp.zeros_like(carry_ref)

    b = bucket_ref[...]
    rows = lax.broadcasted_iota(jnp.int32, (NB_PAD, TM), 0)
    onehot = jnp.where(rows == b, 1.0, 0.0)
    r_i = lax.broadcasted_iota(jnp.int32, (TM, TM), 0)
    c_i = lax.broadcasted_iota(jnp.int32, (TM, TM), 1)
    upper = jnp.where(r_i <= c_i, 1.0, 0.0).astype(BF16)
    prefix = _dot(onehot.astype(BF16), upper)
    total = prefix[:, TM - 1:TM]

    @pl.when(p == 0)
    def _():
        carry_ref[...] += total
        pos_ref[...] = jnp.zeros_like(pos_ref)

    @pl.when((p == 1) & (j == 0))
    def _():
        cnt = carry_ref[...]
        ntile = jnp.floor((cnt + (TMS - 1)) * (1.0 / TMS))
        rr = lax.broadcasted_iota(jnp.int32, (NB_PAD, NB_PAD), 0)
        cc = lax.broadcasted_iota(jnp.int32, (NB_PAD, NB_PAD), 1)
        strict_lower = jnp.where(cc < rr, 1.0, 0.0).astype(BF16)
        ntile_b = jnp.broadcast_to(ntile, (NB_PAD, LANES))
        tstart = _dot(strict_lower, ntile_b.astype(BF16))
        base_ref[...] = tstart[:, 0:1] * TMS
        carry_ref[...] = jnp.zeros_like(carry_ref)
        nused = jnp.sum(ntile_b, axis=0, keepdims=True)
        last = jnp.where(ntile_b > 0, tstart + ntile_b - 1.0, -1.0)
        col = lax.broadcasted_iota(jnp.int32, (NB_PAD, LANES), 1)
        meta_ref[...] = jnp.where(col == 0, last, jnp.broadcast_to(nused, (NB_PAD, LANES))).astype(jnp.int32)
        tile = lax.broadcasted_iota(jnp.int32, (NB_PAD, TB_LEN), 1).astype(F32)
        row_ok = lax.broadcasted_iota(jnp.int32, (NB_PAD, TB_LEN), 0) < N_BUCKETS
        started = jnp.where(row_ok & (tstart[:, 0:1] <= tile), 1.0, 0.0)
        tb_ref[...] = (jnp.sum(started, axis=0, keepdims=True) - 1.0).astype(jnp.int32)

    @pl.when(p == 1)
    def _():
        slot = base_ref[...] + carry_ref[...] + prefix - 1.0
        pos_ref[...] = jnp.sum(onehot * slot, axis=0, keepdims=True).astype(jnp.int32)
        carry_ref[...] += total


def _sort_call(bucket):
    return pl.pallas_call(
        _sort_kernel,
        grid=(2, bucket.shape[0]),
        in_specs=[pl.BlockSpec((None, 1, TM), lambda p, j: (j, 0, 0))],
        out_specs=[
            pl.BlockSpec((None, None, 1, TM), lambda p, j: (p, j, 0, 0)),
            pl.BlockSpec((1, TB_LEN), lambda p, j: (0, 0)),
            pl.BlockSpec((NB_PAD, LANES), lambda p, j: (0, 0)),
        ],
        out_shape=[
            jax.ShapeDtypeStruct((2,) + bucket.shape, jnp.int32),
            jax.ShapeDtypeStruct((1, TB_LEN), jnp.int32),
            jax.ShapeDtypeStruct((NB_PAD, LANES), jnp.int32),
        ],
        scratch_shapes=[pltpu.VMEM((NB_PAD, 1), F32), pltpu.VMEM((NB_PAD, 1), F32)],
        compiler_params=pltpu.CompilerParams(dimension_semantics=("arbitrary", "arbitrary")),
        name="bucket_sort",
    )(bucket)


def _row_copy(src_hbm, src_row, dst_hbm, dst_row, sem):
    return pltpu.make_async_copy(src_hbm.at[pl.ds(src_row, 1)], dst_hbm.at[pl.ds(dst_row, 1)], sem)


def _wait_rows(src_hbm, dst_hbm, sem):
    for _ in range(TM // WAIT_ROWS):
        pltpu.make_async_copy(src_hbm.at[pl.ds(0, WAIT_ROWS)], dst_hbm.at[pl.ds(0, WAIT_ROWS)], sem).wait()


def _dispatch_kernel(last_ref, nused_ref, pos_ref, x_hbm, zeros_hbm, xs_hbm, sem, zsem):
    c = pl.program_id(0)

    @pl.when(c == 0)
    def _():
        def zero_copy(tile):
            return pltpu.make_async_copy(zeros_hbm, xs_hbm.at[pl.ds(tile * TMS, TMS)], zsem)
        for k in range(N_BUCKETS):
            @pl.when(last_ref[k] >= 0)
            def _():
                zero_copy(last_ref[k]).start()

        def start_unused(tile, carry):
            zero_copy(tile).start()
            return carry
        lax.fori_loop(nused_ref[0], NT_MAX, start_unused, 0)
        for k in range(N_BUCKETS):
            @pl.when(last_ref[k] >= 0)
            def _():
                zero_copy(last_ref[k]).wait()

        def wait_unused(tile, carry):
            zero_copy(tile).wait()
            return carry
        lax.fori_loop(nused_ref[0], NT_MAX, wait_unused, 0)

    def issue(i, carry):
        _row_copy(x_hbm, c * TM + i, xs_hbm, pos_ref[0, i], sem).start()
        return carry
    lax.fori_loop(0, TM, issue, 0, unroll=ISSUE_UNROLL)
    _wait_rows(x_hbm, xs_hbm, sem)


def _dispatch_call(last_tile, nused, pos, x2d, zeros_tile):
    return pl.pallas_call(
        _dispatch_kernel,
        grid_spec=pltpu.PrefetchScalarGridSpec(
            num_scalar_prefetch=2,
            grid=(x2d.shape[0] // TM,),
            in_specs=[
                pl.BlockSpec((None, 1, TM), lambda c, last, nu: (c, 0, 0), memory_space=pltpu.SMEM),
                pl.BlockSpec(memory_space=pl.ANY),
                pl.BlockSpec(memory_space=pl.ANY),
            ],
            out_specs=pl.BlockSpec(memory_space=pl.ANY),
            scratch_shapes=[pltpu.SemaphoreType.DMA(()), pltpu.SemaphoreType.DMA(())],
        ),
        out_shape=jax.ShapeDtypeStruct((NT_MAX * TMS, D_MODEL), F32),
        compiler_params=pltpu.CompilerParams(dimension_semantics=("arbitrary",), has_side_effects=True),
        name="moe_dispatch",
    )(last_tile, nused, pos, x2d, zeros_tile)


def _combine_kernel(pos_ref, ys_hbm, out_hbm, sem):
    c = pl.program_id(0)

    def issue(i, carry):
        _row_copy(ys_hbm, pos_ref[0, i], out_hbm, c * TM + i, sem).start()
        return carry
    lax.fori_loop(0, TM, issue, 0, unroll=ISSUE_UNROLL)
    _wait_rows(ys_hbm, out_hbm, sem)


def _combine_call(pos, ys, ntok):
    return pl.pallas_call(
        _combine_kernel,
        grid=(ntok // TM,),
        in_specs=[
            pl.BlockSpec((None, 1, TM), lambda c: (c, 0, 0), memory_space=pltpu.SMEM),
            pl.BlockSpec(memory_space=pl.ANY),
        ],
        out_specs=pl.BlockSpec(memory_space=pl.ANY),
        out_shape=jax.ShapeDtypeStruct((ntok, D_MODEL), F32),
        scratch_shapes=[pltpu.SemaphoreType.DMA(())],
        compiler_params=pltpu.CompilerParams(dimension_semantics=("arbitrary",), has_side_effects=True),
        name="moe_combine",
    )(pos, ys)


def _moe_kernel(tb_ref, nused_ref, x_ref, n2g_ref, wr_ref, br_ref,
                wg_lo_ref, wu_lo_ref, wd_lo_ref, wg_hi_ref, wu_hi_ref, wd_hi_ref, out_ref):
    i = pl.program_id(0)

    @pl.when(i < nused_ref[0])
    def _():
        x = x_ref[...]
        ms = jnp.mean(x * x, axis=-1, keepdims=True)
        hb = (x * lax.rsqrt(ms + EPS) * n2g_ref[...]).astype(BF16)
        logits = _dot(hb, wr_ref[...]) + br_ref[...]
        g, e_lo, e_hi = _bucket_experts(tb_ref[i])
        lane = lax.broadcasted_iota(jnp.int32, (TMS, LANES), 1)
        pick = lambda l: jnp.sum(jnp.where(lane == l, logits, 0.0), axis=-1, keepdims=True)
        lg, l_lo, l_hi = pick(g), pick(8 + e_lo), pick(8 + e_hi)
        gsum = jnp.sum(jnp.where(lane < N_GROUPS, jnp.exp(logits - lg), 0.0), axis=-1, keepdims=True)
        g_w = 1.0 / gsum
        w_lo = g_w / (1.0 + jnp.exp(l_hi - l_lo))
        w_hi = g_w / (1.0 + jnp.exp(l_lo - l_hi))

        def expert(wg_ref, wu_ref, wd_ref, w):
            a = _dot(hb, wg_ref[...])
            hid = a * _sigmoid(a) * _dot(hb, wu_ref[...]) * w
            return _dot(hid.astype(BF16), wd_ref[...])
        out_ref[...] = (x + expert(wg_lo_ref, wu_lo_ref, wd_lo_ref, w_lo)
                        + expert(wg_hi_ref, wu_hi_ref, wd_hi_ref, w_hi))

    @pl.when(i >= nused_ref[0])
    def _():
        out_ref[...] = jnp.zeros_like(out_ref)


def _moe_call(tile_bucket, nused, xs, n2g, wr, br, wg, wu, wd):
    def used(i, nu):
        return jnp.minimum(i, nu[0] - 1)

    def w_spec(shape, which):
        def index(i, tb, nu):
            return (_bucket_experts(tb[used(i, nu)])[which], 0, 0)
        return pl.BlockSpec((None,) + shape, index)

    const = lambda shape: pl.BlockSpec(shape, lambda i, tb, nu: (0, 0))
    up, down = (D_MODEL, D_EXPERT), (D_EXPERT, D_MODEL)
    return pl.pallas_call(
        _moe_kernel,
        grid_spec=pltpu.PrefetchScalarGridSpec(
            num_scalar_prefetch=2,
            grid=(NT_MAX,),
            in_specs=[
                pl.BlockSpec((TMS, D_MODEL), lambda i, tb, nu: (used(i, nu), 0)),
                const((1, D_MODEL)), const((D_MODEL, LANES)), const((1, LANES)),
                w_spec(up, 1), w_spec(up, 1), w_spec(down, 1),
                w_spec(up, 2), w_spec(up, 2), w_spec(down, 2),
            ],
            out_specs=pl.BlockSpec((TMS, D_MODEL), lambda i, tb, nu: (i, 0)),
        ),
        out_shape=jax.ShapeDtypeStruct(xs.shape, F32),
        compiler_params=pltpu.CompilerParams(
            dimension_semantics=("arbitrary",), vmem_limit_bytes=VMEM_LIMIT),
        name="hier_moe",
    )(tile_bucket, nused, xs, n2g, wr, br, wg, wu, wd, wg, wu, wd)


def _rope_tables(positions):
    inv = ROPE_THETA ** (-jnp.arange(0, ROT_DIM, 2, dtype=F32) / ROT_DIM)
    inv_head = jnp.concatenate([inv, inv, jnp.zeros((HEAD_DIM - ROT_DIM,), F32)])
    inv_lane = jnp.concatenate([inv_head, inv_head])
    ang = positions.astype(F32)[..., None] * inv_lane
    return jnp.cos(ang), jnp.sin(ang)


def kernel(x, positions, norm1_g, w_in, sgu_ln_g, sgu_ln_b, sgu_w, sgu_b, q_norm_g, k_norm_g, sinks, conv3_w, conv31_w, conv31_b, cnorm_g, cnorm_b, w_branch, w_gate, b_gate, w_o, norm2_g, w_group, b_group, w_expert, b_expert, w_e_gate, w_e_up, w_e_down):
    bsz, seq, d = x.shape
    assert (bsz, seq, d) == (BATCH, SEQ, D_MODEL)
    ntok = bsz * seq
    depth = norm1_g.shape[0]
    cos_t, sin_t = _rope_tables(positions)
    row = lambda t: t.reshape(1, -1)
    zeros_tile = jnp.zeros((TMS, d), F32)
    for l in range(depth):
        sgu_wcat = jnp.transpose(sgu_w[l], (1, 0, 2)).reshape(SGU_CHUNK, SGU_GROUPS * SGU_CHUNK)
        sgu_bias = jnp.repeat(sgu_b[l].T, MIX_W // SGU_GROUPS, axis=1)
        c3w = jnp.zeros((SUBLANES, MIX_W), F32).at[:SHORT_CONV].set(conv3_w[l])
        c31w = jnp.zeros((C31_HALO, MIX_W), F32).at[:CONFORMER_CONV].set(conv31_w[l])
        wr = jnp.zeros((d, LANES), F32).at[:, 0:N_GROUPS].set(w_group[l]).at[:, 8:8 + N_EXPERTS].set(w_expert[l])
        wr = wr.astype(BF16)
        br = jnp.zeros((1, LANES), F32).at[0, 0:N_GROUPS].set(b_group[l]).at[0, 8:8 + N_EXPERTS].set(b_expert[l])
        x, bucket = _mixing_call(
            x, cos_t, sin_t, row(norm1_g[l]), w_in[l].astype(BF16),
            row(sgu_ln_g[l]), row(sgu_ln_b[l]), sgu_wcat, sgu_bias,
            row(jnp.tile(q_norm_g[l], N_Q_HEADS)), row(jnp.tile(k_norm_g[l], N_KV_HEADS)), sinks[l],
            c3w, c31w, row(conv31_b[l]), row(cnorm_g[l]), row(cnorm_b[l]),
            w_branch[l].astype(BF16), w_gate[l].astype(BF16), row(b_gate[l]), w_o[l].astype(BF16),
            row(norm2_g[l]), wr, br)
        pos2, tile_bucket, meta = _sort_call(bucket)
        pos = pos2[1]
        xs = _dispatch_call(meta[:, 0], meta[0, 1:2], pos, x.reshape(ntok, d), zeros_tile)
        ys = _moe_call(tile_bucket[0], meta[0, 1:2], xs, row(norm2_g[l]), wr, br,
                       w_e_gate[l].astype(BF16), w_e_up[l].astype(BF16), w_e_down[l].astype(BF16))
        x = _combine_call(pos, ys, ntok).reshape(bsz, seq, d)
    return x
```

```python
import functools

import jax
import jax.numpy as jnp
from jax import lax
from jax.experimental import pallas as pl
from jax.experimental.pallas import tpu as pltpu

D_MODEL = 1024
N_BRANCH = 4
MIX_W = 256
SGU_GROUPS = 4
SGU_CHUNK = 128
N_Q_HEADS = 4
N_KV_HEADS = 2
HEAD_DIM = 64
WINDOW = 128
ROT_DIM = HEAD_DIM // 4
ROPE_THETA = 500000.0
SHORT_CONV = 3
CONFORMER_CONV = 31
N_GROUPS = 4
EXPERTS_PER_GROUP = 4
N_EXPERTS = N_GROUPS * EXPERTS_PER_GROUP
D_EXPERT = 256
EPS = 1e-6
D_IN = 2304

BATCH = 8
SEQ = 4096
PAIRS_PER_GROUP = EXPERTS_PER_GROUP * (EXPERTS_PER_GROUP - 1) // 2
N_BUCKETS = N_GROUPS * PAIRS_PER_GROUP
NB_PAD = 32

LANES = 128
SUBLANES = 8
TM = 512
TMS = 256
NTOK = BATCH * SEQ
NT_MAX = NTOK // TMS + N_BUCKETS
JUNK_ROWS = 2 * TMS
TB_LEN = 256
ISSUE_UNROLL = 8
WAIT_ROWS = 128
C3_HALO = SUBLANES
C31_HALO = 32
CONV_ROWS = 64
VMEM_LIMIT = 56 * 1024 * 1024
NEG = -1e30

F32 = jnp.float32
BF16 = jnp.bfloat16


def _dot(a, b):
    return jnp.dot(a, b, preferred_element_type=F32)


def _gelu_tanh(x):
    c = 0.7978845608028654
    return 0.5 * x * (1.0 + jnp.tanh(c * (x + 0.044715 * (x * x * x))))


def _sigmoid(x):
    return 0.5 * jnp.tanh(0.5 * x) + 0.5


def _layernorm(x, g, b):
    mu = jnp.mean(x, axis=-1, keepdims=True)
    xc = x - mu
    var = jnp.mean(xc * xc, axis=-1, keepdims=True)
    return xc * lax.rsqrt(var + EPS) * g + b


def _head_meansq(t, width):
    r = lax.broadcasted_iota(jnp.int32, (width, width), 0) // HEAD_DIM
    c = lax.broadcasted_iota(jnp.int32, (width, width), 1) // HEAD_DIM
    bd = jnp.where(r == c, 1.0 / HEAD_DIM, 0.0).astype(BF16)
    t2 = t * t
    hi = t2.astype(BF16)
    lo = (t2 - hi.astype(F32)).astype(BF16)
    return _dot(hi, bd) + _dot(lo, bd)


def _rope(t, c, s1, s2):
    w = t.shape[-1]
    half = ROT_DIM // 2
    return t * c + pltpu.roll(t, w - half, axis=1) * s1 + pltpu.roll(t, half, axis=1) * s2


def _mixing_kernel(token_major_in, x_ref, cos_ref, sin_ref, n1g_ref, win_ref,
                   sgu_g_ref, sgu_b_ref, sgu_w_ref, sgu_bias_ref,
                   qg_ref, kg_ref, sinks_ref,
                   c3w_ref, c31w_ref, c31b_ref, cng_ref, cnb_ref,
                   wbr_ref, wgate_ref, bgate_ref, wo_ref,
                   n2g_ref, wr_ref, br_ref,
                   out_ref, bucket_ref,
                   kprev_ref, vprev_ref, c3buf_ref, c31buf_ref, ys_ref):
    j = pl.program_id(1)

    @pl.when(j == 0)
    def _():
        kprev_ref[...] = jnp.zeros_like(kprev_ref)
        vprev_ref[...] = jnp.zeros_like(vprev_ref)
        c3buf_ref[0:C3_HALO, :] = jnp.zeros((C3_HALO, MIX_W), F32)
        c31buf_ref[0:C31_HALO, :] = jnp.zeros((C31_HALO, MIX_W), F32)

    x = _load_token_major(x_ref, TM) if token_major_in else x_ref[...]
    ms = jnp.mean(x * x, axis=-1, keepdims=True)
    hb = (x * lax.rsqrt(ms + EPS) * n1g_ref[...]).astype(BF16)

    uv = _dot(hb, win_ref[:, 0:512])
    u = _gelu_tanh(uv[:, 0:MIX_W])
    v = _layernorm(_gelu_tanh(uv[:, MIX_W:2 * MIX_W]), sgu_g_ref[...], sgu_b_ref[...])
    tt = lax.broadcasted_iota(jnp.int32, (SGU_CHUNK, SGU_GROUPS * SGU_CHUNK), 0)
    ss = lax.broadcasted_iota(jnp.int32, (SGU_CHUNK, SGU_GROUPS * SGU_CHUNK), 1) % SGU_CHUNK
    wcat = jnp.where(ss <= tt, sgu_w_ref[...], 0.0).astype(BF16)
    lane_grp = lax.broadcasted_iota(jnp.int32, (SGU_CHUNK, MIX_W), 1) // (MIX_W // SGU_GROUPS)
    for c in range(TM // SGU_CHUNK):
        rows = slice(c * SGU_CHUNK, (c + 1) * SGU_CHUNK)
        vc = v[rows]
        vstack = jnp.concatenate(
            [jnp.where(lane_grp == g, vc, 0.0) for g in range(SGU_GROUPS)], axis=0).astype(BF16)
        z = _dot(wcat, vstack) + sgu_bias_ref[...]
        ys_ref[0, rows, :] = (u[rows] * z).astype(BF16)

    qkv = _dot(hb, win_ref[:, 512:1024])
    rc, sin_t = cos_ref[...], sin_ref[...]
    head_dim_idx = lax.broadcasted_iota(jnp.int32, (1, LANES), 1) % HEAD_DIM
    rs1 = jnp.where(head_dim_idx < ROT_DIM // 2, -sin_t, 0.0)
    rs2 = jnp.where((head_dim_idx >= ROT_DIM // 2) & (head_dim_idx < ROT_DIM), sin_t, 0.0)
    q = qkv[:, 0:256]
    q = q * lax.rsqrt(_head_meansq(q, 256) + EPS) * qg_ref[...]
    q = _rope(q, jnp.concatenate([rc, rc], axis=1), jnp.concatenate([rs1, rs1], axis=1),
              jnp.concatenate([rs2, rs2], axis=1)) * (HEAD_DIM ** -0.5)
    k = qkv[:, 256:384]
    k = k * lax.rsqrt(_head_meansq(k, 128) + EPS) * kg_ref[...]
    k = _rope(k, rc, rs1, rs2)
    vv = qkv[:, 384:512]
    kfull = jnp.concatenate([kprev_ref[...], k], axis=0)
    vfull = jnp.concatenate([vprev_ref[...], vv], axis=0)
    kprev_ref[...] = k[TM - WINDOW:TM]
    vprev_ref[...] = vv[TM - WINDOW:TM]
    low_full = lax.broadcasted_iota(jnp.int32, (WINDOW + TM, LANES), 1) < HEAD_DIM
    krot = pltpu.roll(kfull, HEAD_DIM, axis=1)
    vrot = pltpu.roll(vfull, HEAD_DIM, axis=1)
    kdup = [jnp.where(low_full, kfull, krot).astype(BF16), jnp.where(low_full, krot, kfull).astype(BF16)]
    vdup = [jnp.where(low_full, vfull, vrot).astype(BF16), jnp.where(low_full, vrot, vfull).astype(BF16)]
    low = lax.broadcasted_iota(jnp.int32, (WINDOW, LANES), 1) < HEAD_DIM
    row2 = lax.broadcasted_iota(jnp.int32, (2 * WINDOW, 2 * WINDOW), 0)
    qi = row2 % WINDOW
    ki = lax.broadcasted_iota(jnp.int32, (2 * WINDOW, 2 * WINDOW), 1)
    in_prev = (ki < WINDOW) & (ki > qi)
    in_cur = (ki >= WINDOW) & (ki - WINDOW <= qi)
    is_g0 = lax.broadcasted_iota(jnp.int32, (2 * WINDOW, 1), 0) < WINDOW
    for n in range(TM // WINDOW):
        if n == 0:
            valid = (in_prev & (j > 0)) | in_cur
        else:
            valid = in_prev | in_cur
        cols = []
        for h in range(N_KV_HEADS):
            qcol = q[n * WINDOW:(n + 1) * WINDOW, h * LANES:(h + 1) * LANES]
            qs = jnp.concatenate([jnp.where(low, qcol, 0.0), jnp.where(low, 0.0, qcol)], axis=0).astype(BF16)
            kk = kdup[h][n * WINDOW:(n + 2) * WINDOW]
            sc = lax.dot_general(qs, kk, (((1,), (1,)), ((), ())), preferred_element_type=F32)
            sc = jnp.where(valid, sc, NEG)
            sink = jnp.where(is_g0, sinks_ref[2 * h], sinks_ref[2 * h + 1])
            m = jnp.maximum(jnp.max(sc, axis=-1, keepdims=True), sink)
            p = jnp.exp(sc - m)
            denom = jnp.sum(p, axis=-1, keepdims=True) + jnp.exp(sink - m)
            o = _dot(p.astype(BF16), vdup[h][n * WINDOW:(n + 2) * WINDOW]) / denom
            cols.append(jnp.where(low, o[0:WINDOW], o[WINDOW:2 * WINDOW]))
        ys_ref[1, n * WINDOW:(n + 1) * WINDOW, :] = jnp.concatenate(cols, axis=1).astype(BF16)

    cc = _dot(hb, win_ref[:, 1024:1792])
    c3buf_ref[C3_HALO:C3_HALO + TM, :] = cc[:, 0:256] * cc[:, 512:768]
    conv = c3w_ref[0:1, :] * c3buf_ref[C3_HALO - 2:C3_HALO - 2 + TM, :]
    conv = conv + c3w_ref[1:2, :] * c3buf_ref[C3_HALO - 1:C3_HALO - 1 + TM, :]
    conv = conv + c3w_ref[2:3, :] * c3buf_ref[C3_HALO:C3_HALO + TM, :]
    ys_ref[2] = (cc[:, 256:512] * conv).astype(BF16)
    c3buf_ref[0:C3_HALO, :] = c3buf_ref[TM:TM + C3_HALO, :]

    dd = _dot(hb, win_ref[:, 1792:2304])
    c31buf_ref[C31_HALO:C31_HALO + TM, :] = dd[:, 0:256] * _sigmoid(dd[:, 256:512])
    base = C31_HALO - (CONFORMER_CONV - 1)
    for r in range(TM // CONV_ROWS):
        acc = jnp.broadcast_to(c31b_ref[...], (CONV_ROWS, MIX_W))
        for tap in range(CONFORMER_CONV):
            start = base + tap + r * CONV_ROWS
            acc = acc + c31w_ref[tap:tap + 1, :] * c31buf_ref[start:start + CONV_ROWS, :]
        yn = _layernorm(acc, cng_ref[...], cnb_ref[...])
        ys_ref[3, r * CONV_ROWS:(r + 1) * CONV_ROWS, :] = (yn * _sigmoid(yn)).astype(BF16)
    c31buf_ref[0:C31_HALO, :] = c31buf_ref[TM:TM + C31_HALO, :]

    merged = jnp.zeros((TM, D_MODEL), F32)
    for b in range(N_BRANCH):
        yb = _dot(ys_ref[b], wbr_ref[b])
        gate = _dot(hb, wgate_ref[:, b * D_MODEL:(b + 1) * D_MODEL]) + bgate_ref[:, b * D_MODEL:(b + 1) * D_MODEL]
        merged = merged + _sigmoid(gate) * yb
    xo = x + _dot(merged.astype(BF16), wo_ref[...])
    _store_token_major(out_ref, xo, TM)

    ms2 = jnp.mean(xo * xo, axis=-1, keepdims=True)
    h2 = (xo * lax.rsqrt(ms2 + EPS) * n2g_ref[...]).astype(BF16)
    logits = _dot(h2, wr_ref[...]) + br_ref[...]
    bucket_ref[...] = _route_bucket(logits.T)


def _const_spec(shape):
    zeros = (0,) * len(shape)
    return pl.BlockSpec(shape, lambda b, j: zeros, pipeline_mode=pl.Buffered(1))


def _mixing_call(x, cos_t, sin_t, n1g, w_in, sgu_g, sgu_b, sgu_w, sgu_bias, qg, kg, sinks,
                 c3w, c31w, c31b, cng, cnb, wbr, wgate, bgate, wo, n2g, wr, br):
    bsz, seq = BATCH, SEQ
    nj = seq // TM
    tok = lambda width: pl.BlockSpec((None, TM, width), lambda b, j: (b, j, 0))
    tok_major = pl.BlockSpec((TM * SUBLANES, LANES), lambda b, j: (b * nj + j, 0))
    token_major_in = x.ndim == 2
    in_specs = [
        tok_major if token_major_in else tok(D_MODEL), tok(LANES), tok(LANES),
        _const_spec((1, D_MODEL)), _const_spec((D_MODEL, D_IN)),
        _const_spec((1, MIX_W)), _const_spec((1, MIX_W)),
        _const_spec((SGU_CHUNK, SGU_GROUPS * SGU_CHUNK)), _const_spec((SGU_CHUNK, MIX_W)),
        _const_spec((1, N_Q_HEADS * HEAD_DIM)), _const_spec((1, N_KV_HEADS * HEAD_DIM)),
        pl.BlockSpec(memory_space=pltpu.SMEM),
        _const_spec((SUBLANES, MIX_W)), _const_spec((C31_HALO, MIX_W)), _const_spec((1, MIX_W)),
        _const_spec((1, MIX_W)), _const_spec((1, MIX_W)),
        _const_spec((N_BRANCH, MIX_W, D_MODEL)), _const_spec((D_MODEL, N_BRANCH * D_MODEL)),
        _const_spec((1, N_BRANCH * D_MODEL)), _const_spec((D_MODEL, D_MODEL)),
        _const_spec((1, D_MODEL)), _const_spec((D_MODEL, LANES)), _const_spec((1, LANES)),
    ]
    return pl.pallas_call(
        functools.partial(_mixing_kernel, token_major_in),
        grid=(bsz, nj),
        in_specs=in_specs,
        out_specs=[tok_major, pl.BlockSpec((None, 1, TM), lambda b, j: (b * nj + j, 0, 0))],
        out_shape=[jax.ShapeDtypeStruct((NTOK * SUBLANES, LANES), F32),
                   jax.ShapeDtypeStruct((bsz * nj, 1, TM), jnp.int32)],
        scratch_shapes=[
            pltpu.VMEM((WINDOW, LANES), F32), pltpu.VMEM((WINDOW, LANES), F32),
            pltpu.VMEM((C3_HALO + TM, MIX_W), F32), pltpu.VMEM((C31_HALO + TM, MIX_W), F32),
            pltpu.VMEM((N_BRANCH, TM, MIX_W), BF16),
        ],
        compiler_params=pltpu.CompilerParams(
            dimension_semantics=("arbitrary", "arbitrary"), vmem_limit_bytes=VMEM_LIMIT),
        name="mixing_block",
    )(x, cos_t, sin_t, n1g, w_in, sgu_g, sgu_b, sgu_w, sgu_bias, qg, kg, sinks,
      c3w, c31w, c31b, cng, cnb, wbr, wgate, bgate, wo, n2g, wr, br)


def _route_bucket(logits_t):
    g = [logits_t[i:i + 1, :] for i in range(N_GROUPS)]
    gmax, gidx = g[0], jnp.zeros(g[0].shape, jnp.int32)
    for i in range(1, N_GROUPS):
        better = g[i] > gmax
        gmax = jnp.where(better, g[i], gmax)
        gidx = jnp.where(better, i, gidx)
    e = []
    for i in range(EXPERTS_PER_GROUP):
        ei = logits_t[8 + i:9 + i, :]
        for grp in range(1, N_GROUPS):
            row = 8 + grp * EXPERTS_PER_GROUP + i
            ei = jnp.where(gidx == grp, logits_t[row:row + 1, :], ei)
        e.append(ei)
    v1, i1 = e[0], jnp.zeros_like(gidx)
    for i in range(1, EXPERTS_PER_GROUP):
        better = e[i] > v1
        v1 = jnp.where(better, e[i], v1)
        i1 = jnp.where(better, i, i1)
    v2, i2 = jnp.full_like(v1, -jnp.inf), jnp.zeros_like(gidx)
    for i in range(EXPERTS_PER_GROUP):
        better = (e[i] > v2) & (i1 != i)
        v2 = jnp.where(better, e[i], v2)
        i2 = jnp.where(better, i, i2)
    lo, hi = jnp.minimum(i1, i2), jnp.maximum(i1, i2)
    pair = jnp.where(lo == 0, hi - 1, jnp.where(lo == 1, hi + 1, 5))
    return gidx * PAIRS_PER_GROUP + pair


def _bucket_experts(b):
    g = b // PAIRS_PER_GROUP
    pair = b - g * PAIRS_PER_GROUP
    lo = jnp.where(pair < 3, 0, jnp.where(pair < 5, 1, 2))
    hi = jnp.where(pair < 3, pair + 1, jnp.where(pair < 5, pair - 1, 3))
    return g, g * EXPERTS_PER_GROUP + lo, g * EXPERTS_PER_GROUP + hi


def _sort_kernel(bucket_ref, pos_ref, tb_ref, meta_ref, carry_ref, base_ref):
    p = pl.program_id(0)
    j = pl.program_id(1)

    @pl.when((p == 0) & (j == 0))
    def _():
        carry_ref[...] = jnp.zeros_like(carry_ref)

    b = bucket_ref[...]
    rows = lax.broadcasted_iota(jnp.int32, (NB_PAD, TM), 0)
    onehot = jnp.where(rows == b, 1.0, 0.0)
    r_i = lax.broadcasted_iota(jnp.int32, (TM, TM), 0)
    c_i = lax.broadcasted_iota(jnp.int32, (TM, TM), 1)
    upper = jnp.where(r_i <= c_i, 1.0, 0.0).astype(BF16)
    prefix = _dot(onehot.astype(BF16), upper)
    total = prefix[:, TM - 1:TM]

    @pl.when(p == 0)
    def _():
        carry_ref[...] += total
        pos_ref[...] = jnp.zeros_like(pos_ref)

    @pl.when((p == 1) & (j == 0))
    def _():
        cnt = carry_ref[...]
        ntile = jnp.floor((cnt + (TMS - 1)) * (1.0 / TMS))
        rr = lax.broadcasted_iota(jnp.int32, (NB_PAD, NB_PAD), 0)
        cc = lax.broadcasted_iota(jnp.int32, (NB_PAD, NB_PAD), 1)
        strict_lower = jnp.where(cc < rr, 1.0, 0.0).astype(BF16)
        ntile_b = jnp.broadcast_to(ntile, (NB_PAD, LANES))
        tstart = _dot(strict_lower, ntile_b.astype(BF16))
        base_ref[...] = tstart[:, 0:1] * TMS
        carry_ref[...] = jnp.zeros_like(carry_ref)
        nused = jnp.broadcast_to(jnp.sum(ntile_b, axis=0, keepdims=True), (NB_PAD, LANES))
        col = lax.broadcasted_iota(jnp.int32, (NB_PAD, LANES), 1)
        meta = jnp.where(col == 0, tstart * TMS, jnp.where(col == 1, jnp.broadcast_to(cnt, (NB_PAD, LANES)), nused))
        meta_ref[...] = meta.astype(jnp.int32)
        tile = lax.broadcasted_iota(jnp.int32, (NB_PAD, TB_LEN), 1).astype(F32)
        row_ok = lax.broadcasted_iota(jnp.int32, (NB_PAD, TB_LEN), 0) < N_BUCKETS
        started = jnp.where(row_ok & (tstart[:, 0:1] <= tile), 1.0, 0.0)
        tb_ref[...] = (jnp.sum(started, axis=0, keepdims=True) - 1.0).astype(jnp.int32)

    @pl.when(p == 1)
    def _():
        slot = base_ref[...] + carry_ref[...] + prefix - 1.0
        pos_ref[...] = jnp.sum(onehot * slot, axis=0, keepdims=True).astype(jnp.int32)
        carry_ref[...] += total


def _sort_call(bucket):
    return pl.pallas_call(
        _sort_kernel,
        grid=(2, bucket.shape[0]),
        in_specs=[pl.BlockSpec((None, 1, TM), lambda p, j: (j, 0, 0))],
        out_specs=[
            pl.BlockSpec((None, None, 1, TM), lambda p, j: (p, j, 0, 0)),
            pl.BlockSpec((1, TB_LEN), lambda p, j: (0, 0)),
            pl.BlockSpec((NB_PAD, LANES), lambda p, j: (0, 0)),
        ],
        out_shape=[
            jax.ShapeDtypeStruct((2,) + bucket.shape, jnp.int32),
            jax.ShapeDtypeStruct((1, TB_LEN), jnp.int32),
            jax.ShapeDtypeStruct((NB_PAD, LANES), jnp.int32),
        ],
        scratch_shapes=[pltpu.VMEM((NB_PAD, 1), F32), pltpu.VMEM((NB_PAD, 1), F32)],
        compiler_params=pltpu.CompilerParams(dimension_semantics=("arbitrary", "arbitrary")),
        name="bucket_sort",
    )(bucket)


def _invert_kernel(base_ref, cnt_ref, nused_ref, pos_ref, inv_ref):
    c = pl.program_id(0)

    def fill(s, carry):
        inv_ref[s] = NTOK + (s & (JUNK_ROWS - 1))
        return carry

    @pl.when(c == 0)
    def _():
        for k in range(N_BUCKETS):
            n_slots = ((cnt_ref[k] + (TMS - 1)) // TMS) * TMS
            lax.fori_loop(base_ref[k] + cnt_ref[k], base_ref[k] + n_slots, fill, 0)
        lax.fori_loop(nused_ref[0] * TMS, NT_MAX * TMS, fill, 0)

    def place(i, carry):
        inv_ref[pos_ref[0, i]] = c * TM + i
        return carry
    lax.fori_loop(0, TM, place, 0, unroll=ISSUE_UNROLL)


def _invert_call(base, cnt, nused, pos):
    return pl.pallas_call(
        _invert_kernel,
        grid_spec=pltpu.PrefetchScalarGridSpec(
            num_scalar_prefetch=3,
            grid=(pos.shape[0],),
            in_specs=[pl.BlockSpec((None, 1, TM), lambda c, *_: (c, 0, 0), memory_space=pltpu.SMEM)],
            out_specs=pl.BlockSpec(memory_space=pltpu.SMEM),
        ),
        out_shape=jax.ShapeDtypeStruct((NT_MAX * TMS,), jnp.int32),
        compiler_params=pltpu.CompilerParams(dimension_semantics=("arbitrary",)),
        name="invert_permutation",
    )(base, cnt, nused, pos)


def _load_token_major(ref, rows):
    return jnp.concatenate([ref[pl.ds(c, rows, stride=SUBLANES), :] for c in range(SUBLANES)], axis=1)


def _store_token_major(ref, value, rows):
    for c in range(SUBLANES):
        ref[pl.ds(c, rows, stride=SUBLANES), :] = value[:, c * LANES:(c + 1) * LANES]


def _to_rows_kernel(x_ref, out_ref):
    out_ref[...] = _load_token_major(x_ref, TM)


def _to_rows_call(x_tm, ntok):
    return pl.pallas_call(
        _to_rows_kernel,
        grid=(ntok // TM,),
        in_specs=[pl.BlockSpec((TM * SUBLANES, LANES), lambda i: (i, 0))],
        out_specs=pl.BlockSpec((TM, D_MODEL), lambda i: (i, 0)),
        out_shape=jax.ShapeDtypeStruct((ntok, D_MODEL), F32),
        compiler_params=pltpu.CompilerParams(dimension_semantics=("arbitrary",)),
        name="token_major_to_rows",
    )(x_tm)


def _moe_kernel(tb_ref, nused_ref, inv_ref, inv_next_ref, x_hbm, n2g_ref, wr_ref, br_ref,
                wg_lo_ref, wu_lo_ref, wd_lo_ref, wg_hi_ref, wu_hi_ref, wd_hi_ref,
                out_hbm, xbuf, obuf, gsem, ssem):
    i = pl.program_id(0)
    nused = nused_ref[0]
    slot = lax.rem(i, 2)

    def gather_row(idx_ref, r, s):
        tok = jnp.minimum(idx_ref[0, r], NTOK - 1)
        return pltpu.make_async_copy(x_hbm.at[pl.ds(tok * SUBLANES, SUBLANES)],
                                     xbuf.at[s, pl.ds(r * SUBLANES, SUBLANES)], gsem.at[s])

    def scatter_row(r, s):
        tok = inv_ref[0, r]
        return pltpu.make_async_copy(obuf.at[s, pl.ds(r * SUBLANES, SUBLANES)],
                                     out_hbm.at[pl.ds(tok * SUBLANES, SUBLANES)], ssem.at[s])

    def start_gather(idx_ref, s):
        def body(r, carry):
            gather_row(idx_ref, r, s).start()
            return carry
        lax.fori_loop(0, TMS, body, 0, unroll=ISSUE_UNROLL)

    def wait_tile(sem, s):
        for _ in range(TMS // WAIT_ROWS):
            pltpu.make_async_copy(x_hbm.at[pl.ds(0, WAIT_ROWS * SUBLANES)],
                                  xbuf.at[s, pl.ds(0, WAIT_ROWS * SUBLANES)], sem.at[s]).wait()

    @pl.when(i == 0)
    def _():
        obuf[...] = jnp.zeros_like(obuf)
        for s in range(2):
            junk = pltpu.make_async_copy(
                obuf.at[s], out_hbm.at[pl.ds((NTOK + s * TMS) * SUBLANES, TMS * SUBLANES)], ssem.at[s])
            junk.start()
            junk.wait()
        start_gather(inv_ref, 0)

    @pl.when(i + 1 < nused)
    def _():
        start_gather(inv_next_ref, 1 - slot)

    @pl.when(i < nused)
    def _():
        wait_tile(gsem, slot)

        @pl.when(i >= 2)
        def _():
            wait_tile(ssem, slot)
        x = _load_token_major(xbuf.at[slot], TMS)
        ms = jnp.mean(x * x, axis=-1, keepdims=True)
        hb = (x * lax.rsqrt(ms + EPS) * n2g_ref[...]).astype(BF16)
        logits = _dot(hb, wr_ref[...]) + br_ref[...]
        g, e_lo, e_hi = _bucket_experts(tb_ref[i])
        lane = lax.broadcasted_iota(jnp.int32, (TMS, LANES), 1)
        pick = lambda l: jnp.sum(jnp.where(lane == l, logits, 0.0), axis=-1, keepdims=True)
        lg, l_lo, l_hi = pick(g), pick(8 + e_lo), pick(8 + e_hi)
        gsum = jnp.sum(jnp.where(lane < N_GROUPS, jnp.exp(logits - lg), 0.0), axis=-1, keepdims=True)
        g_w = 1.0 / gsum
        w_lo = g_w / (1.0 + jnp.exp(l_hi - l_lo))
        w_hi = g_w / (1.0 + jnp.exp(l_lo - l_hi))

        def expert(wg_ref, wu_ref, wd_ref, w):
            a = _dot(hb, wg_ref[...])
            hid = a * _sigmoid(a) * _dot(hb, wu_ref[...]) * w
            return _dot(hid.astype(BF16), wd_ref[...])
        y = (x + expert(wg_lo_ref, wu_lo_ref, wd_lo_ref, w_lo)
             + expert(wg_hi_ref, wu_hi_ref, wd_hi_ref, w_hi))
        _store_token_major(obuf.at[slot], y, TMS)

        def body(r, carry):
            scatter_row(r, slot).start()
            return carry
        lax.fori_loop(0, TMS, body, 0, unroll=ISSUE_UNROLL)

    @pl.when(i == nused - 1)
    def _():
        wait_tile(ssem, slot)

        @pl.when(i >= 1)
        def _():
            wait_tile(ssem, 1 - slot)


def _moe_call(tile_bucket, nused, inv, x_tm, n2g, wr, br, wg, wu, wd):
    def used(i, nu):
        return jnp.minimum(i, nu[0] - 1)

    def w_spec(shape, which):
        def index(i, tb, nu):
            return (_bucket_experts(tb[used(i, nu)])[which], 0, 0)
        return pl.BlockSpec((None,) + shape, index)

    const = lambda shape: pl.BlockSpec(shape, lambda i, tb, nu: (0, 0))
    idx_spec = lambda shift: pl.BlockSpec(
        (None, 1, TMS), lambda i, tb, nu: (jnp.minimum(i + shift, NT_MAX - 1), 0, 0), memory_space=pltpu.SMEM)
    up, down = (D_MODEL, D_EXPERT), (D_EXPERT, D_MODEL)
    inv3 = inv.reshape(NT_MAX, 1, TMS)
    return pl.pallas_call(
        _moe_kernel,
        grid_spec=pltpu.PrefetchScalarGridSpec(
            num_scalar_prefetch=2,
            grid=(NT_MAX,),
            in_specs=[
                idx_spec(0), idx_spec(1), pl.BlockSpec(memory_space=pl.ANY),
                const((1, D_MODEL)), const((D_MODEL, LANES)), const((1, LANES)),
                w_spec(up, 1), w_spec(up, 1), w_spec(down, 1),
                w_spec(up, 2), w_spec(up, 2), w_spec(down, 2),
            ],
            out_specs=pl.BlockSpec(memory_space=pl.ANY),
            scratch_shapes=[
                pltpu.VMEM((2, TMS * SUBLANES, LANES), F32), pltpu.VMEM((2, TMS * SUBLANES, LANES), F32),
                pltpu.SemaphoreType.DMA((2,)), pltpu.SemaphoreType.DMA((2,)),
            ],
        ),
        out_shape=jax.ShapeDtypeStruct(((NTOK + JUNK_ROWS) * SUBLANES, LANES), F32),
        compiler_params=pltpu.CompilerParams(
            dimension_semantics=("arbitrary",), vmem_limit_bytes=VMEM_LIMIT, has_side_effects=True),
        name="hier_moe",
    )(tile_bucket, nused, inv3, inv3, x_tm, n2g, wr, br, wg, wu, wd, wg, wu, wd)


def _rope_tables(positions):
    inv = ROPE_THETA ** (-jnp.arange(0, ROT_DIM, 2, dtype=F32) / ROT_DIM)
    inv_head = jnp.concatenate([inv, inv, jnp.zeros((HEAD_DIM - ROT_DIM,), F32)])
    inv_lane = jnp.concatenate([inv_head, inv_head])
    ang = positions.astype(F32)[..., None] * inv_lane
    return jnp.cos(ang), jnp.sin(ang)


def kernel(x, positions, norm1_g, w_in, sgu_ln_g, sgu_ln_b, sgu_w, sgu_b, q_norm_g, k_norm_g, sinks, conv3_w, conv31_w, conv31_b, cnorm_g, cnorm_b, w_branch, w_gate, b_gate, w_o, norm2_g, w_group, b_group, w_expert, b_expert, w_e_gate, w_e_up, w_e_down):
    bsz, seq, d = x.shape
    assert (bsz, seq, d) == (BATCH, SEQ, D_MODEL)
    ntok = bsz * seq
    depth = norm1_g.shape[0]
    cos_t, sin_t = _rope_tables(positions)
    row = lambda t: t.reshape(1, -1)
    for l in range(depth):
        sgu_wcat = jnp.transpose(sgu_w[l], (1, 0, 2)).reshape(SGU_CHUNK, SGU_GROUPS * SGU_CHUNK)
        sgu_bias = jnp.repeat(sgu_b[l].T, MIX_W // SGU_GROUPS, axis=1)
        c3w = jnp.zeros((SUBLANES, MIX_W), F32).at[:SHORT_CONV].set(conv3_w[l])
        c31w = jnp.zeros((C31_HALO, MIX_W), F32).at[:CONFORMER_CONV].set(conv31_w[l])
        wr = jnp.zeros((d, LANES), F32).at[:, 0:N_GROUPS].set(w_group[l]).at[:, 8:8 + N_EXPERTS].set(w_expert[l])
        wr = wr.astype(BF16)
        br = jnp.zeros((1, LANES), F32).at[0, 0:N_GROUPS].set(b_group[l]).at[0, 8:8 + N_EXPERTS].set(b_expert[l])
        x, bucket = _mixing_call(
            x, cos_t, sin_t, row(norm1_g[l]), w_in[l].astype(BF16),
            row(sgu_ln_g[l]), row(sgu_ln_b[l]), sgu_wcat, sgu_bias,
            row(jnp.tile(q_norm_g[l], N_Q_HEADS)), row(jnp.tile(k_norm_g[l], N_KV_HEADS)), sinks[l],
            c3w, c31w, row(conv31_b[l]), row(cnorm_g[l]), row(cnorm_b[l]),
            w_branch[l].astype(BF16), w_gate[l].astype(BF16), row(b_gate[l]), w_o[l].astype(BF16),
            row(norm2_g[l]), wr, br)
        pos2, tile_bucket, meta = _sort_call(bucket)
        nused = meta[0, 2:3]
        inv = _invert_call(meta[:, 0], meta[:, 1], nused, pos2[1])
        x = _moe_call(tile_bucket[0], nused, inv, x, row(norm2_g[l]), wr, br,
                      w_e_gate[l].astype(BF16), w_e_up[l].astype(BF16), w_e_down[l].astype(BF16))
    return _to_rows_call(x, ntok).reshape(bsz, seq, d)
```

```python
import functools

import jax
import jax.numpy as jnp
from jax import lax
from jax.experimental import pallas as pl
from jax.experimental.pallas import tpu as pltpu

D_MODEL = 1024
N_BRANCH = 4
MIX_W = 256
SGU_GROUPS = 4
SGU_CHUNK = 128
N_Q_HEADS = 4
N_KV_HEADS = 2
HEAD_DIM = 64
WINDOW = 128
ROT_DIM = HEAD_DIM // 4
ROPE_THETA = 500000.0
SHORT_CONV = 3
CONFORMER_CONV = 31
N_GROUPS = 4
EXPERTS_PER_GROUP = 4
N_EXPERTS = N_GROUPS * EXPERTS_PER_GROUP
D_EXPERT = 256
EPS = 1e-6
D_IN = 2304

BATCH = 8
SEQ = 4096
PAIRS_PER_GROUP = EXPERTS_PER_GROUP * (EXPERTS_PER_GROUP - 1) // 2
N_BUCKETS = N_GROUPS * PAIRS_PER_GROUP
NB_PAD = 32

LANES = 128
SUBLANES = 8
TM = 512
TMS = 256
NTOK = BATCH * SEQ
NT_MAX = NTOK // TMS + N_BUCKETS
JUNK_ROWS = 2 * TMS
TB_LEN = 256
ISSUE_UNROLL = 8
WAIT_ROWS = 128
C3_HALO = SUBLANES
C31_HALO = 32
CONV_ROWS = 64
VMEM_LIMIT = 56 * 1024 * 1024
NEG = -1e30

F32 = jnp.float32
BF16 = jnp.bfloat16


def _dot(a, b):
    return jnp.dot(a, b, preferred_element_type=F32)


def _gelu_tanh(x):
    c = 0.7978845608028654
    return 0.5 * x * (1.0 + jnp.tanh(c * (x + 0.044715 * (x * x * x))))


def _sigmoid(x):
    return 0.5 * jnp.tanh(0.5 * x) + 0.5


def _layernorm(x, g, b):
    mu = jnp.mean(x, axis=-1, keepdims=True)
    xc = x - mu
    var = jnp.mean(xc * xc, axis=-1, keepdims=True)
    return xc * lax.rsqrt(var + EPS) * g + b


def _head_meansq(t, width):
    r = lax.broadcasted_iota(jnp.int32, (width, width), 0) // HEAD_DIM
    c = lax.broadcasted_iota(jnp.int32, (width, width), 1) // HEAD_DIM
    bd = jnp.where(r == c, 1.0 / HEAD_DIM, 0.0).astype(BF16)
    t2 = t * t
    hi = t2.astype(BF16)
    lo = (t2 - hi.astype(F32)).astype(BF16)
    return _dot(hi, bd) + _dot(lo, bd)


def _rope(t, c, s1, s2):
    w = t.shape[-1]
    half = ROT_DIM // 2
    return t * c + pltpu.roll(t, w - half, axis=1) * s1 + pltpu.roll(t, half, axis=1) * s2


def _mixing_kernel(token_major_in, x_ref, cos_ref, sin_ref, n1g_ref, win_ref,
                   sgu_g_ref, sgu_b_ref, sgu_w_ref, sgu_bias_ref,
                   qg_ref, kg_ref, sinks_ref,
                   c3w_ref, c31w_ref, c31b_ref, cng_ref, cnb_ref,
                   wbr_ref, wgate_ref, bgate_ref, wo_ref,
                   n2g_ref, wr_ref, br_ref,
                   out_ref, bucket_ref,
                   kprev_ref, vprev_ref, c3buf_ref, c31buf_ref, ys_ref):
    j = pl.program_id(1)

    @pl.when(j == 0)
    def _():
        kprev_ref[...] = jnp.zeros_like(kprev_ref)
        vprev_ref[...] = jnp.zeros_like(vprev_ref)
        c3buf_ref[0:C3_HALO, :] = jnp.zeros((C3_HALO, MIX_W), F32)
        c31buf_ref[0:C31_HALO, :] = jnp.zeros((C31_HALO, MIX_W), F32)

    x = _load_token_major(x_ref, TM) if token_major_in else x_ref[...]
    ms = jnp.mean(x * x, axis=-1, keepdims=True)
    hb = (x * lax.rsqrt(ms + EPS) * n1g_ref[...]).astype(BF16)

    uv = _dot(hb, win_ref[:, 0:512])
    u = _gelu_tanh(uv[:, 0:MIX_W])
    v = _layernorm(_gelu_tanh(uv[:, MIX_W:2 * MIX_W]), sgu_g_ref[...], sgu_b_ref[...])
    tt = lax.broadcasted_iota(jnp.int32, (SGU_CHUNK, SGU_GROUPS * SGU_CHUNK), 0)
    ss = lax.broadcasted_iota(jnp.int32, (SGU_CHUNK, SGU_GROUPS * SGU_CHUNK), 1) % SGU_CHUNK
    wcat = jnp.where(ss <= tt, sgu_w_ref[...], 0.0).astype(BF16)
    lane_grp = lax.broadcasted_iota(jnp.int32, (SGU_CHUNK, MIX_W), 1) // (MIX_W // SGU_GROUPS)
    for c in range(TM // SGU_CHUNK):
        rows = slice(c * SGU_CHUNK, (c + 1) * SGU_CHUNK)
        vc = v[rows]
        vstack = jnp.concatenate(
            [jnp.where(lane_grp == g, vc, 0.0) for g in range(SGU_GROUPS)], axis=0).astype(BF16)
        z = _dot(wcat, vstack) + sgu_bias_ref[...]
        ys_ref[0, rows, :] = (u[rows] * z).astype(BF16)

    qkv = _dot(hb, win_ref[:, 512:1024])
    rc, sin_t = cos_ref[...], sin_ref[...]
    head_dim_idx = lax.broadcasted_iota(jnp.int32, (1, LANES), 1) % HEAD_DIM
    rs1 = jnp.where(head_dim_idx < ROT_DIM // 2, -sin_t, 0.0)
    rs2 = jnp.where((head_dim_idx >= ROT_DIM // 2) & (head_dim_idx < ROT_DIM), sin_t, 0.0)
    q = qkv[:, 0:256]
    q = q * lax.rsqrt(_head_meansq(q, 256) + EPS) * qg_ref[...]
    q = _rope(q, jnp.concatenate([rc, rc], axis=1), jnp.concatenate([rs1, rs1], axis=1),
              jnp.concatenate([rs2, rs2], axis=1)) * (HEAD_DIM ** -0.5)
    k = qkv[:, 256:384]
    k = k * lax.rsqrt(_head_meansq(k, 128) + EPS) * kg_ref[...]
    k = _rope(k, rc, rs1, rs2)
    vv = qkv[:, 384:512]
    kfull = jnp.concatenate([kprev_ref[...], k], axis=0)
    vfull = jnp.concatenate([vprev_ref[...], vv], axis=0)
    kprev_ref[...] = k[TM - WINDOW:TM]
    vprev_ref[...] = vv[TM - WINDOW:TM]
    low_full = lax.broadcasted_iota(jnp.int32, (WINDOW + TM, LANES), 1) < HEAD_DIM
    krot = pltpu.roll(kfull, HEAD_DIM, axis=1)
    vrot = pltpu.roll(vfull, HEAD_DIM, axis=1)
    kdup = [jnp.where(low_full, kfull, krot).astype(BF16), jnp.where(low_full, krot, kfull).astype(BF16)]
    vdup = [jnp.where(low_full, vfull, vrot).astype(BF16), jnp.where(low_full, vrot, vfull).astype(BF16)]
    low = lax.broadcasted_iota(jnp.int32, (WINDOW, LANES), 1) < HEAD_DIM
    row2 = lax.broadcasted_iota(jnp.int32, (2 * WINDOW, 2 * WINDOW), 0)
    qi = row2 % WINDOW
    ki = lax.broadcasted_iota(jnp.int32, (2 * WINDOW, 2 * WINDOW), 1)
    in_prev = (ki < WINDOW) & (ki > qi)
    in_cur = (ki >= WINDOW) & (ki - WINDOW <= qi)
    is_g0 = lax.broadcasted_iota(jnp.int32, (2 * WINDOW, 1), 0) < WINDOW
    for n in range(TM // WINDOW):
        if n == 0:
            valid = (in_prev & (j > 0)) | in_cur
        else:
            valid = in_prev | in_cur
        cols = []
        for h in range(N_KV_HEADS):
            qcol = q[n * WINDOW:(n + 1) * WINDOW, h * LANES:(h + 1) * LANES]
            qs = jnp.concatenate([jnp.where(low, qcol, 0.0), jnp.where(low, 0.0, qcol)], axis=0).astype(BF16)
            kk = kdup[h][n * WINDOW:(n + 2) * WINDOW]
            sc = lax.dot_general(qs, kk, (((1,), (1,)), ((), ())), preferred_element_type=F32)
            sc = jnp.where(valid, sc, NEG)
            sink = jnp.where(is_g0, sinks_ref[2 * h], sinks_ref[2 * h + 1])
            m = jnp.maximum(jnp.max(sc, axis=-1, keepdims=True), sink)
            p = jnp.exp(sc - m)
            denom = jnp.sum(p, axis=-1, keepdims=True) + jnp.exp(sink - m)
            o = _dot(p.astype(BF16), vdup[h][n * WINDOW:(n + 2) * WINDOW]) / denom
            cols.append(jnp.where(low, o[0:WINDOW], o[WINDOW:2 * WINDOW]))
        ys_ref[1, n * WINDOW:(n + 1) * WINDOW, :] = jnp.concatenate(cols, axis=1).astype(BF16)

    cc = _dot(hb, win_ref[:, 1024:1792])
    c3buf_ref[C3_HALO:C3_HALO + TM, :] = cc[:, 0:256] * cc[:, 512:768]
    conv = c3w_ref[0:1, :] * c3buf_ref[C3_HALO - 2:C3_HALO - 2 + TM, :]
    conv = conv + c3w_ref[1:2, :] * c3buf_ref[C3_HALO - 1:C3_HALO - 1 + TM, :]
    conv = conv + c3w_ref[2:3, :] * c3buf_ref[C3_HALO:C3_HALO + TM, :]
    ys_ref[2] = (cc[:, 256:512] * conv).astype(BF16)
    c3buf_ref[0:C3_HALO, :] = c3buf_ref[TM:TM + C3_HALO, :]

    dd = _dot(hb, win_ref[:, 1792:2304])
    c31buf_ref[C31_HALO:C31_HALO + TM, :] = dd[:, 0:256] * _sigmoid(dd[:, 256:512])
    base = C31_HALO - (CONFORMER_CONV - 1)
    for r in range(TM // CONV_ROWS):
        acc = jnp.broadcast_to(c31b_ref[...], (CONV_ROWS, MIX_W))
        for sub in range(SUBLANES):
            taps = [t for t in range(CONFORMER_CONV) if (base + t) % SUBLANES == sub]
            last = (base + taps[-1]) // SUBLANES * SUBLANES
            start = r * CONV_ROWS + sub
            window = c31buf_ref[start:start + last + CONV_ROWS, :]
            part = None
            for tap in taps:
                off = (base + tap) // SUBLANES * SUBLANES
                term = c31w_ref[tap:tap + 1, :] * window[off:off + CONV_ROWS]
                part = term if part is None else part + term
            acc = acc + part
        yn = _layernorm(acc, cng_ref[...], cnb_ref[...])
        ys_ref[3, r * CONV_ROWS:(r + 1) * CONV_ROWS, :] = (yn * _sigmoid(yn)).astype(BF16)
    c31buf_ref[0:C31_HALO, :] = c31buf_ref[TM:TM + C31_HALO, :]

    merged = jnp.zeros((TM, D_MODEL), F32)
    for b in range(N_BRANCH):
        yb = _dot(ys_ref[b], wbr_ref[b])
        half_gate = (_dot(hb, wgate_ref[:, b * D_MODEL:(b + 1) * D_MODEL])
                     + bgate_ref[:, b * D_MODEL:(b + 1) * D_MODEL])
        merged = merged + (jnp.tanh(half_gate) * yb + yb)
    xo = x + _dot((0.5 * merged).astype(BF16), wo_ref[...])
    _store_token_major(out_ref, xo, TM)

    ms2 = jnp.mean(xo * xo, axis=-1, keepdims=True)
    h2 = (xo * lax.rsqrt(ms2 + EPS) * n2g_ref[...]).astype(BF16)
    logits = _dot(h2, wr_ref[...]) + br_ref[...]
    bucket_ref[...] = _route_bucket(logits.T)


def _const_spec(shape):
    zeros = (0,) * len(shape)
    return pl.BlockSpec(shape, lambda b, j: zeros, pipeline_mode=pl.Buffered(1))


def _mixing_call(x, cos_t, sin_t, n1g, w_in, sgu_g, sgu_b, sgu_w, sgu_bias, qg, kg, sinks,
                 c3w, c31w, c31b, cng, cnb, wbr, wgate, bgate, wo, n2g, wr, br):
    bsz, seq = BATCH, SEQ
    nj = seq // TM
    tok = lambda width: pl.BlockSpec((None, TM, width), lambda b, j: (b, j, 0))
    tok_major = pl.BlockSpec((TM * SUBLANES, LANES), lambda b, j: (b * nj + j, 0))
    token_major_in = x.ndim == 2
    in_specs = [
        tok_major if token_major_in else tok(D_MODEL), tok(LANES), tok(LANES),
        _const_spec((1, D_MODEL)), _const_spec((D_MODEL, D_IN)),
        _const_spec((1, MIX_W)), _const_spec((1, MIX_W)),
        _const_spec((SGU_CHUNK, SGU_GROUPS * SGU_CHUNK)), _const_spec((SGU_CHUNK, MIX_W)),
        _const_spec((1, N_Q_HEADS * HEAD_DIM)), _const_spec((1, N_KV_HEADS * HEAD_DIM)),
        pl.BlockSpec(memory_space=pltpu.SMEM),
        _const_spec((SUBLANES, MIX_W)), _const_spec((C31_HALO, MIX_W)), _const_spec((1, MIX_W)),
        _const_spec((1, MIX_W)), _const_spec((1, MIX_W)),
        _const_spec((N_BRANCH, MIX_W, D_MODEL)), _const_spec((D_MODEL, N_BRANCH * D_MODEL)),
        _const_spec((1, N_BRANCH * D_MODEL)), _const_spec((D_MODEL, D_MODEL)),
        _const_spec((1, D_MODEL)), _const_spec((D_MODEL, LANES)), _const_spec((1, LANES)),
    ]
    return pl.pallas_call(
        functools.partial(_mixing_kernel, token_major_in),
        grid=(bsz, nj),
        in_specs=in_specs,
        out_specs=[tok_major, pl.BlockSpec((None, 1, TM), lambda b, j: (b * nj + j, 0, 0))],
        out_shape=[jax.ShapeDtypeStruct((NTOK * SUBLANES, LANES), F32),
                   jax.ShapeDtypeStruct((bsz * nj, 1, TM), jnp.int32)],
        scratch_shapes=[
            pltpu.VMEM((WINDOW, LANES), F32), pltpu.VMEM((WINDOW, LANES), F32),
            pltpu.VMEM((C3_HALO + TM, MIX_W), F32), pltpu.VMEM((C31_HALO + TM, MIX_W), F32),
            pltpu.VMEM((N_BRANCH, TM, MIX_W), BF16),
        ],
        compiler_params=pltpu.CompilerParams(
            dimension_semantics=("arbitrary", "arbitrary"), vmem_limit_bytes=VMEM_LIMIT),
        name="mixing_block",
    )(x, cos_t, sin_t, n1g, w_in, sgu_g, sgu_b, sgu_w, sgu_bias, qg, kg, sinks,
      c3w, c31w, c31b, cng, cnb, wbr, wgate, bgate, wo, n2g, wr, br)


def _route_bucket(logits_t):
    g = [logits_t[i:i + 1, :] for i in range(N_GROUPS)]
    gmax, gidx = g[0], jnp.zeros(g[0].shape, jnp.int32)
    for i in range(1, N_GROUPS):
        better = g[i] > gmax
        gmax = jnp.where(better, g[i], gmax)
        gidx = jnp.where(better, i, gidx)
    e = []
    for i in range(EXPERTS_PER_GROUP):
        ei = logits_t[8 + i:9 + i, :]
        for grp in range(1, N_GROUPS):
            row = 8 + grp * EXPERTS_PER_GROUP + i
            ei = jnp.where(gidx == grp, logits_t[row:row + 1, :], ei)
        e.append(ei)
    v1, i1 = e[0], jnp.zeros_like(gidx)
    for i in range(1, EXPERTS_PER_GROUP):
        better = e[i] > v1
        v1 = jnp.where(better, e[i], v1)
        i1 = jnp.where(better, i, i1)
    v2, i2 = jnp.full_like(v1, -jnp.inf), jnp.zeros_like(gidx)
    for i in range(EXPERTS_PER_GROUP):
        better = (e[i] > v2) & (i1 != i)
        v2 = jnp.where(better, e[i], v2)
        i2 = jnp.where(better, i, i2)
    lo, hi = jnp.minimum(i1, i2), jnp.maximum(i1, i2)
    pair = jnp.where(lo == 0, hi - 1, jnp.where(lo == 1, hi + 1, 5))
    return gidx * PAIRS_PER_GROUP + pair


def _bucket_experts(b):
    g = b // PAIRS_PER_GROUP
    pair = b - g * PAIRS_PER_GROUP
    lo = jnp.where(pair < 3, 0, jnp.where(pair < 5, 1, 2))
    hi = jnp.where(pair < 3, pair + 1, jnp.where(pair < 5, pair - 1, 3))
    return g, g * EXPERTS_PER_GROUP + lo, g * EXPERTS_PER_GROUP + hi


def _sort_kernel(bucket_ref, pos_ref, tb_ref, meta_ref, carry_ref, base_ref):
    p = pl.program_id(0)
    j = pl.program_id(1)

    @pl.when((p == 0) & (j == 0))
    def _():
        carry_ref[...] = jnp.zeros_like(carry_ref)

    b = bucket_ref[...]
    rows = lax.broadcasted_iota(jnp.int32, (NB_PAD, TM), 0)
    onehot = jnp.where(rows == b, 1.0, 0.0)
    r_i = lax.broadcasted_iota(jnp.int32, (TM, TM), 0)
    c_i = lax.broadcasted_iota(jnp.int32, (TM, TM), 1)
    upper = jnp.where(r_i <= c_i, 1.0, 0.0).astype(BF16)
    prefix = _dot(onehot.astype(BF16), upper)
    total = prefix[:, TM - 1:TM]

    @pl.when(p == 0)
    def _():
        carry_ref[...] += total
        pos_ref[...] = jnp.zeros_like(pos_ref)

    @pl.when((p == 1) & (j == 0))
    def _():
        cnt = carry_ref[...]
        ntile = jnp.floor((cnt + (TMS - 1)) * (1.0 / TMS))
        rr = lax.broadcasted_iota(jnp.int32, (NB_PAD, NB_PAD), 0)
        cc = lax.broadcasted_iota(jnp.int32, (NB_PAD, NB_PAD), 1)
        strict_lower = jnp.where(cc < rr, 1.0, 0.0).astype(BF16)
        ntile_b = jnp.broadcast_to(ntile, (NB_PAD, LANES))
        tstart = _dot(strict_lower, ntile_b.astype(BF16))
        base_ref[...] = tstart[:, 0:1] * TMS
        carry_ref[...] = jnp.zeros_like(carry_ref)
        nused = jnp.broadcast_to(jnp.sum(ntile_b, axis=0, keepdims=True), (NB_PAD, LANES))
        col = lax.broadcasted_iota(jnp.int32, (NB_PAD, LANES), 1)
        meta = jnp.where(col == 0, tstart * TMS, jnp.where(col == 1, jnp.broadcast_to(cnt, (NB_PAD, LANES)), nused))
        meta_ref[...] = meta.astype(jnp.int32)
        tile = lax.broadcasted_iota(jnp.int32, (NB_PAD, TB_LEN), 1).astype(F32)
        row_ok = lax.broadcasted_iota(jnp.int32, (NB_PAD, TB_LEN), 0) < N_BUCKETS
        started = jnp.where(row_ok & (tstart[:, 0:1] <= tile), 1.0, 0.0)
        tb_ref[...] = (jnp.sum(started, axis=0, keepdims=True) - 1.0).astype(jnp.int32)

    @pl.when(p == 1)
    def _():
        slot = base_ref[...] + carry_ref[...] + prefix - 1.0
        pos_ref[...] = jnp.sum(onehot * slot, axis=0, keepdims=True).astype(jnp.int32)
        carry_ref[...] += total


def _sort_call(bucket):
    return pl.pallas_call(
        _sort_kernel,
        grid=(2, bucket.shape[0]),
        in_specs=[pl.BlockSpec((None, 1, TM), lambda p, j: (j, 0, 0))],
        out_specs=[
            pl.BlockSpec((None, None, 1, TM), lambda p, j: (p, j, 0, 0)),
            pl.BlockSpec((1, TB_LEN), lambda p, j: (0, 0)),
            pl.BlockSpec((NB_PAD, LANES), lambda p, j: (0, 0)),
        ],
        out_shape=[
            jax.ShapeDtypeStruct((2,) + bucket.shape, jnp.int32),
            jax.ShapeDtypeStruct((1, TB_LEN), jnp.int32),
            jax.ShapeDtypeStruct((NB_PAD, LANES), jnp.int32),
        ],
        scratch_shapes=[pltpu.VMEM((NB_PAD, 1), F32), pltpu.VMEM((NB_PAD, 1), F32)],
        compiler_params=pltpu.CompilerParams(dimension_semantics=("arbitrary", "arbitrary")),
        name="bucket_sort",
    )(bucket)


def _invert_kernel(base_ref, cnt_ref, nused_ref, pos_ref, inv_ref):
    c = pl.program_id(0)

    def fill(s, carry):
        inv_ref[s] = NTOK + (s & (JUNK_ROWS - 1))
        return carry

    @pl.when(c == 0)
    def _():
        for k in range(N_BUCKETS):
            n_slots = ((cnt_ref[k] + (TMS - 1)) // TMS) * TMS
            lax.fori_loop(base_ref[k] + cnt_ref[k], base_ref[k] + n_slots, fill, 0)
        lax.fori_loop(nused_ref[0] * TMS, NT_MAX * TMS, fill, 0)

    def place(i, carry):
        inv_ref[pos_ref[0, i]] = c * TM + i
        return carry
    lax.fori_loop(0, TM, place, 0, unroll=ISSUE_UNROLL)


def _invert_call(base, cnt, nused, pos):
    return pl.pallas_call(
        _invert_kernel,
        grid_spec=pltpu.PrefetchScalarGridSpec(
            num_scalar_prefetch=3,
            grid=(pos.shape[0],),
            in_specs=[pl.BlockSpec((None, 1, TM), lambda c, *_: (c, 0, 0), memory_space=pltpu.SMEM)],
            out_specs=pl.BlockSpec(memory_space=pltpu.SMEM),
        ),
        out_shape=jax.ShapeDtypeStruct((NT_MAX * TMS,), jnp.int32),
        compiler_params=pltpu.CompilerParams(dimension_semantics=("arbitrary",)),
        name="invert_permutation",
    )(base, cnt, nused, pos)


def _load_token_major(ref, rows):
    return jnp.concatenate([ref[pl.ds(c, rows, stride=SUBLANES), :] for c in range(SUBLANES)], axis=1)


def _store_token_major(ref, value, rows):
    for c in range(SUBLANES):
        ref[pl.ds(c, rows, stride=SUBLANES), :] = value[:, c * LANES:(c + 1) * LANES]


def _to_rows_kernel(x_ref, out_ref):
    out_ref[...] = _load_token_major(x_ref, TM)


def _to_rows_call(x_tm, ntok):
    return pl.pallas_call(
        _to_rows_kernel,
        grid=(ntok // TM,),
        in_specs=[pl.BlockSpec((TM * SUBLANES, LANES), lambda i: (i, 0))],
        out_specs=pl.BlockSpec((TM, D_MODEL), lambda i: (i, 0)),
        out_shape=jax.ShapeDtypeStruct((ntok, D_MODEL), F32),
        compiler_params=pltpu.CompilerParams(dimension_semantics=("arbitrary",)),
        name="token_major_to_rows",
    )(x_tm)


def _moe_kernel(tb_ref, nused_ref, inv_ref, inv_next_ref, x_hbm, n2g_ref, wr_ref, br_ref,
                wg_lo_ref, wu_lo_ref, wd_lo_ref, wg_hi_ref, wu_hi_ref, wd_hi_ref,
                out_hbm, xbuf, obuf, gsem, ssem):
    i = pl.program_id(0)
    nused = nused_ref[0]
    slot = lax.rem(i, 2)

    def gather_row(idx_ref, r, s):
        tok = jnp.minimum(idx_ref[0, r], NTOK - 1)
        return pltpu.make_async_copy(x_hbm.at[pl.ds(tok * SUBLANES, SUBLANES)],
                                     xbuf.at[s, pl.ds(r * SUBLANES, SUBLANES)], gsem.at[s])

    def scatter_row(r, s):
        tok = inv_ref[0, r]
        return pltpu.make_async_copy(obuf.at[s, pl.ds(r * SUBLANES, SUBLANES)],
                                     out_hbm.at[pl.ds(tok * SUBLANES, SUBLANES)], ssem.at[s])

    def start_rows(row_copy):
        def body(g, carry):
            for u in range(ISSUE_UNROLL):
                row_copy(g * ISSUE_UNROLL + u).start(priority=u % 2)
            return carry
        lax.fori_loop(0, TMS // ISSUE_UNROLL, body, 0)

    def start_gather(idx_ref, s):
        start_rows(lambda r: gather_row(idx_ref, r, s))

    def wait_tile(sem, s):
        for _ in range(TMS // WAIT_ROWS):
            pltpu.make_async_copy(x_hbm.at[pl.ds(0, WAIT_ROWS * SUBLANES)],
                                  xbuf.at[s, pl.ds(0, WAIT_ROWS * SUBLANES)], sem.at[s]).wait()

    @pl.when(i == 0)
    def _():
        obuf[...] = jnp.zeros_like(obuf)
        for s in range(2):
            junk = pltpu.make_async_copy(
                obuf.at[s], out_hbm.at[pl.ds((NTOK + s * TMS) * SUBLANES, TMS * SUBLANES)], ssem.at[s])
            junk.start()
            junk.wait()
        start_gather(inv_ref, 0)

    @pl.when(i + 1 < nused)
    def _():
        start_gather(inv_next_ref, 1 - slot)

    @pl.when(i < nused)
    def _():
        wait_tile(gsem, slot)

        @pl.when(i >= 2)
        def _():
            wait_tile(ssem, slot)
        x = _load_token_major(xbuf.at[slot], TMS)
        ms = jnp.mean(x * x, axis=-1, keepdims=True)
        hb = (x * lax.rsqrt(ms + EPS) * n2g_ref[...]).astype(BF16)
        logits = _dot(hb, wr_ref[...]) + br_ref[...]
        g, e_lo, e_hi = _bucket_experts(tb_ref[i])
        lane = lax.broadcasted_iota(jnp.int32, (TMS, LANES), 1)
        pick = lambda l: jnp.sum(jnp.where(lane == l, logits, 0.0), axis=-1, keepdims=True)
        lg, l_lo, l_hi = pick(g), pick(8 + e_lo), pick(8 + e_hi)
        gsum = jnp.sum(jnp.where(lane < N_GROUPS, jnp.exp(logits - lg), 0.0), axis=-1, keepdims=True)
        g_w = 1.0 / gsum
        w_lo = g_w / (1.0 + jnp.exp(l_hi - l_lo))
        w_hi = g_w / (1.0 + jnp.exp(l_lo - l_hi))

        def expert(wg_ref, wu_ref, wd_ref, w):
            a = _dot(hb, wg_ref[...])
            hid = a * _sigmoid(a) * _dot(hb, wu_ref[...]) * w
            return _dot(hid.astype(BF16), wd_ref[...])
        y = (x + expert(wg_lo_ref, wu_lo_ref, wd_lo_ref, w_lo)
             + expert(wg_hi_ref, wu_hi_ref, wd_hi_ref, w_hi))
        _store_token_major(obuf.at[slot], y, TMS)

        start_rows(lambda r: scatter_row(r, slot))

    @pl.when(i == nused - 1)
    def _():
        wait_tile(ssem, slot)

        @pl.when(i >= 1)
        def _():
            wait_tile(ssem, 1 - slot)


def _moe_call(tile_bucket, nused, inv, x_tm, n2g, wr, br, wg, wu, wd):
    def used(i, nu):
        return jnp.minimum(i, nu[0] - 1)

    def w_spec(shape, which):
        def index(i, tb, nu):
            return (_bucket_experts(tb[used(i, nu)])[which], 0, 0)
        return pl.BlockSpec((None,) + shape, index)

    const = lambda shape: pl.BlockSpec(shape, lambda i, tb, nu: (0, 0))
    idx_spec = lambda shift: pl.BlockSpec(
        (None, 1, TMS), lambda i, tb, nu: (jnp.minimum(i + shift, NT_MAX - 1), 0, 0), memory_space=pltpu.SMEM)
    up, down = (D_MODEL, D_EXPERT), (D_EXPERT, D_MODEL)
    inv3 = inv.reshape(NT_MAX, 1, TMS)
    return pl.pallas_call(
        _moe_kernel,
        grid_spec=pltpu.PrefetchScalarGridSpec(
            num_scalar_prefetch=2,
            grid=(NT_MAX,),
            in_specs=[
                idx_spec(0), idx_spec(1), pl.BlockSpec(memory_space=pl.ANY),
                const((1, D_MODEL)), const((D_MODEL, LANES)), const((1, LANES)),
                w_spec(up, 1), w_spec(up, 1), w_spec(down, 1),
                w_spec(up, 2), w_spec(up, 2), w_spec(down, 2),
            ],
            out_specs=pl.BlockSpec(memory_space=pl.ANY),
            scratch_shapes=[
                pltpu.VMEM((2, TMS * SUBLANES, LANES), F32), pltpu.VMEM((2, TMS * SUBLANES, LANES), F32),
                pltpu.SemaphoreType.DMA((2,)), pltpu.SemaphoreType.DMA((2,)),
            ],
        ),
        out_shape=jax.ShapeDtypeStruct(((NTOK + JUNK_ROWS) * SUBLANES, LANES), F32),
        compiler_params=pltpu.CompilerParams(
            dimension_semantics=("arbitrary",), vmem_limit_bytes=VMEM_LIMIT, has_side_effects=True),
        name="hier_moe",
    )(tile_bucket, nused, inv3, inv3, x_tm, n2g, wr, br, wg, wu, wd, wg, wu, wd)


def _rope_tables(positions):
    inv = ROPE_THETA ** (-jnp.arange(0, ROT_DIM, 2, dtype=F32) / ROT_DIM)
    inv_head = jnp.concatenate([inv, inv, jnp.zeros((HEAD_DIM - ROT_DIM,), F32)])
    inv_lane = jnp.concatenate([inv_head, inv_head])
    ang = positions.astype(F32)[..., None] * inv_lane
    return jnp.cos(ang), jnp.sin(ang)


def kernel(x, positions, norm1_g, w_in, sgu_ln_g, sgu_ln_b, sgu_w, sgu_b, q_norm_g, k_norm_g, sinks, conv3_w, conv31_w, conv31_b, cnorm_g, cnorm_b, w_branch, w_gate, b_gate, w_o, norm2_g, w_group, b_group, w_expert, b_expert, w_e_gate, w_e_up, w_e_down):
    bsz, seq, d = x.shape
    assert (bsz, seq, d) == (BATCH, SEQ, D_MODEL)
    ntok = bsz * seq
    depth = norm1_g.shape[0]
    cos_t, sin_t = _rope_tables(positions)
    row = lambda t: t.reshape(1, -1)
    for l in range(depth):
        sgu_wcat = jnp.transpose(sgu_w[l], (1, 0, 2)).reshape(SGU_CHUNK, SGU_GROUPS * SGU_CHUNK)
        sgu_bias = jnp.repeat(sgu_b[l].T, MIX_W // SGU_GROUPS, axis=1)
        c3w = jnp.zeros((SUBLANES, MIX_W), F32).at[:SHORT_CONV].set(conv3_w[l])
        c31w = jnp.zeros((C31_HALO, MIX_W), F32).at[:CONFORMER_CONV].set(conv31_w[l])
        wr = jnp.zeros((d, LANES), F32).at[:, 0:N_GROUPS].set(w_group[l]).at[:, 8:8 + N_EXPERTS].set(w_expert[l])
        wr = wr.astype(BF16)
        br = jnp.zeros((1, LANES), F32).at[0, 0:N_GROUPS].set(b_group[l]).at[0, 8:8 + N_EXPERTS].set(b_expert[l])
        x, bucket = _mixing_call(
            x, cos_t, sin_t, row(norm1_g[l]), w_in[l].astype(BF16),
            row(sgu_ln_g[l]), row(sgu_ln_b[l]), sgu_wcat, sgu_bias,
            row(jnp.tile(q_norm_g[l], N_Q_HEADS)), row(jnp.tile(k_norm_g[l], N_KV_HEADS)), sinks[l],
            c3w, c31w, row(conv31_b[l]), row(cnorm_g[l]), row(cnorm_b[l]),
            w_branch[l].astype(BF16), (0.5 * w_gate[l]).astype(BF16), row(0.5 * b_gate[l]), w_o[l].astype(BF16),
            row(norm2_g[l]), wr, br)
        pos2, tile_bucket, meta = _sort_call(bucket)
        nused = meta[0, 2:3]
        inv = _invert_call(meta[:, 0], meta[:, 1], nused, pos2[1])
        x = _moe_call(tile_bucket[0], nused, inv, x, row(norm2_g[l]), wr, br,
                      w_e_gate[l].astype(BF16), w_e_up[l].astype(BF16), w_e_down[l].astype(BF16))
    return _to_rows_call(x, ntok).reshape(bsz, seq, d)
```

```python
import functools

import jax
import jax.numpy as jnp
from jax import lax
from jax.experimental import pallas as pl
from jax.experimental.pallas import tpu as pltpu

D_MODEL = 1024
N_BRANCH = 4
MIX_W = 256
SGU_GROUPS = 4
SGU_CHUNK = 128
N_Q_HEADS = 4
N_KV_HEADS = 2
HEAD_DIM = 64
WINDOW = 128
ROT_DIM = HEAD_DIM // 4
ROPE_THETA = 500000.0
SHORT_CONV = 3
CONFORMER_CONV = 31
N_GROUPS = 4
EXPERTS_PER_GROUP = 4
N_EXPERTS = N_GROUPS * EXPERTS_PER_GROUP
D_EXPERT = 256
EPS = 1e-6
D_IN = 2304

BATCH = 8
SEQ = 4096
PAIRS_PER_GROUP = EXPERTS_PER_GROUP * (EXPERTS_PER_GROUP - 1) // 2
N_BUCKETS = N_GROUPS * PAIRS_PER_GROUP
NB_PAD = 32

LANES = 128
SUBLANES = 8
TM = 512
TMS = 256
NTOK = BATCH * SEQ
N_TILES = NTOK // TM
TILES_PER_SEQ = SEQ // TM
NT_MAX = NTOK // TMS + N_BUCKETS
JUNK_ROWS = 2 * TMS
TB_LEN = 256
ISSUE_UNROLL = 8
WAIT_ROWS = 128
C3_HALO = SUBLANES
C31_HALO = 32
CONV_ROWS = 64
VMEM_LIMIT = 56 * 1024 * 1024
NEG = -1e30

F32 = jnp.float32
BF16 = jnp.bfloat16


def _dot(a, b):
    return jnp.dot(a, b, preferred_element_type=F32)


def _gelu_tanh(x):
    c = 0.7978845608028654
    return 0.5 * x * (1.0 + jnp.tanh(c * (x + 0.044715 * (x * x * x))))


def _sigmoid(x):
    return 0.5 * jnp.tanh(0.5 * x) + 0.5


def _layernorm(x, g, b):
    mu = jnp.mean(x, axis=-1, keepdims=True)
    xc = x - mu
    var = jnp.mean(xc * xc, axis=-1, keepdims=True)
    return xc * lax.rsqrt(var + EPS) * g + b


def _head_meansq(t, width):
    r = lax.broadcasted_iota(jnp.int32, (width, width), 0) // HEAD_DIM
    c = lax.broadcasted_iota(jnp.int32, (width, width), 1) // HEAD_DIM
    bd = jnp.where(r == c, 1.0 / HEAD_DIM, 0.0).astype(BF16)
    t2 = t * t
    hi = t2.astype(BF16)
    lo = (t2 - hi.astype(F32)).astype(BF16)
    return _dot(hi, bd) + _dot(lo, bd)


def _rope(t, c, s1, s2):
    w = t.shape[-1]
    half = ROT_DIM // 2
    return t * c + pltpu.roll(t, w - half, axis=1) * s1 + pltpu.roll(t, half, axis=1) * s2


def _mixing_step(token_major_in, j, x_ref, cos_ref, sin_ref, n1g_ref, win_ref,
                 sgu_g_ref, sgu_b_ref, sgu_w_ref, sgu_bias_ref,
                 qg_ref, kg_ref, sinks_ref,
                 c3w_ref, c31w_ref, c31b_ref, cng_ref, cnb_ref,
                 wbr_ref, wgate_ref, bgate_ref, wo_ref,
                 n2g_ref, wr_ref, br_ref,
                 out_ref, bucket_ref,
                 kprev_ref, vprev_ref, c3buf_ref, c31buf_ref,
                 ys_ref, hb_ref, xk_ref, ys_in, hb_in, xk_in):
    x = _load_token_major(x_ref, TM) if token_major_in else x_ref[...]
    xk_ref[...] = x
    ms = jnp.mean(x * x, axis=-1, keepdims=True)
    hb = (x * lax.rsqrt(ms + EPS) * n1g_ref[...]).astype(BF16)
    hb_ref[...] = hb

    uv = _dot(hb, win_ref[:, 0:512])
    u = _gelu_tanh(uv[:, 0:MIX_W])
    v = _layernorm(_gelu_tanh(uv[:, MIX_W:2 * MIX_W]), sgu_g_ref[...], sgu_b_ref[...])
    tt = lax.broadcasted_iota(jnp.int32, (SGU_CHUNK, SGU_GROUPS * SGU_CHUNK), 0)
    ss = lax.broadcasted_iota(jnp.int32, (SGU_CHUNK, SGU_GROUPS * SGU_CHUNK), 1) % SGU_CHUNK
    wcat = jnp.where(ss <= tt, sgu_w_ref[...], 0.0).astype(BF16)
    lane_grp = lax.broadcasted_iota(jnp.int32, (SGU_CHUNK, MIX_W), 1) // (MIX_W // SGU_GROUPS)
    for c in range(TM // SGU_CHUNK):
        rows = slice(c * SGU_CHUNK, (c + 1) * SGU_CHUNK)
        vc = v[rows]
        vstack = jnp.concatenate(
            [jnp.where(lane_grp == g, vc, 0.0) for g in range(SGU_GROUPS)], axis=0).astype(BF16)
        z = _dot(wcat, vstack) + sgu_bias_ref[...]
        ys_ref[0, rows, :] = (u[rows] * z).astype(BF16)

    qkv = _dot(hb, win_ref[:, 512:1024])
    rc, sin_t = cos_ref[...], sin_ref[...]
    head_dim_idx = lax.broadcasted_iota(jnp.int32, (1, LANES), 1) % HEAD_DIM
    rs1 = jnp.where(head_dim_idx < ROT_DIM // 2, -sin_t, 0.0)
    rs2 = jnp.where((head_dim_idx >= ROT_DIM // 2) & (head_dim_idx < ROT_DIM), sin_t, 0.0)
    q = qkv[:, 0:256]
    q = q * lax.rsqrt(_head_meansq(q, 256) + EPS) * qg_ref[...]
    q = _rope(q, jnp.concatenate([rc, rc], axis=1), jnp.concatenate([rs1, rs1], axis=1),
              jnp.concatenate([rs2, rs2], axis=1)) * (HEAD_DIM ** -0.5)
    k = qkv[:, 256:384]
    k = k * lax.rsqrt(_head_meansq(k, 128) + EPS) * kg_ref[...]
    k = _rope(k, rc, rs1, rs2)
    vv = qkv[:, 384:512]
    kfull = jnp.concatenate([kprev_ref[...], k], axis=0)
    vfull = jnp.concatenate([vprev_ref[...], vv], axis=0)
    kprev_ref[...] = k[TM - WINDOW:TM]
    vprev_ref[...] = vv[TM - WINDOW:TM]
    low_full = lax.broadcasted_iota(jnp.int32, (WINDOW + TM, LANES), 1) < HEAD_DIM
    krot = pltpu.roll(kfull, HEAD_DIM, axis=1)
    vrot = pltpu.roll(vfull, HEAD_DIM, axis=1)
    kdup = [jnp.where(low_full, kfull, krot).astype(BF16), jnp.where(low_full, krot, kfull).astype(BF16)]
    vdup = [jnp.where(low_full, vfull, vrot).astype(BF16), jnp.where(low_full, vrot, vfull).astype(BF16)]
    low = lax.broadcasted_iota(jnp.int32, (WINDOW, LANES), 1) < HEAD_DIM
    row2 = lax.broadcasted_iota(jnp.int32, (2 * WINDOW, 2 * WINDOW), 0)
    qi = row2 % WINDOW
    ki = lax.broadcasted_iota(jnp.int32, (2 * WINDOW, 2 * WINDOW), 1)
    in_prev = (ki < WINDOW) & (ki > qi)
    in_cur = (ki >= WINDOW) & (ki - WINDOW <= qi)
    is_g0 = lax.broadcasted_iota(jnp.int32, (2 * WINDOW, 1), 0) < WINDOW
    for n in range(TM // WINDOW):
        if n == 0:
            valid = (in_prev & (j > 0)) | in_cur
        else:
            valid = in_prev | in_cur
        cols = []
        for h in range(N_KV_HEADS):
            qcol = q[n * WINDOW:(n + 1) * WINDOW, h * LANES:(h + 1) * LANES]
            qs = jnp.concatenate([jnp.where(low, qcol, 0.0), jnp.where(low, 0.0, qcol)], axis=0).astype(BF16)
            kk = kdup[h][n * WINDOW:(n + 2) * WINDOW]
            sc = lax.dot_general(qs, kk, (((1,), (1,)), ((), ())), preferred_element_type=F32)
            sc = jnp.where(valid, sc, NEG)
            sink = jnp.where(is_g0, sinks_ref[2 * h], sinks_ref[2 * h + 1])
            m = jnp.maximum(jnp.max(sc, axis=-1, keepdims=True), sink)
            p = jnp.exp(sc - m)
            denom = jnp.sum(p, axis=-1, keepdims=True) + jnp.exp(sink - m)
            o = _dot(p.astype(BF16), vdup[h][n * WINDOW:(n + 2) * WINDOW]) / denom
            cols.append(jnp.where(low, o[0:WINDOW], o[WINDOW:2 * WINDOW]))
        ys_ref[1, n * WINDOW:(n + 1) * WINDOW, :] = jnp.concatenate(cols, axis=1).astype(BF16)

    cc = _dot(hb, win_ref[:, 1024:1792])
    c3buf_ref[C3_HALO:C3_HALO + TM, :] = cc[:, 0:256] * cc[:, 512:768]
    conv = c3w_ref[0:1, :] * c3buf_ref[C3_HALO - 2:C3_HALO - 2 + TM, :]
    conv = conv + c3w_ref[1:2, :] * c3buf_ref[C3_HALO - 1:C3_HALO - 1 + TM, :]
    conv = conv + c3w_ref[2:3, :] * c3buf_ref[C3_HALO:C3_HALO + TM, :]
    ys_ref[2] = (cc[:, 256:512] * conv).astype(BF16)
    c3buf_ref[0:C3_HALO, :] = c3buf_ref[TM:TM + C3_HALO, :]

    dd = _dot(hb, win_ref[:, 1792:2304])
    c31buf_ref[C31_HALO:C31_HALO + TM, :] = dd[:, 0:256] * _sigmoid(dd[:, 256:512])
    base = C31_HALO - (CONFORMER_CONV - 1)
    for r in range(TM // CONV_ROWS):
        acc = jnp.broadcast_to(c31b_ref[...], (CONV_ROWS, MIX_W))
        for sub in range(SUBLANES):
            taps = [t for t in range(CONFORMER_CONV) if (base + t) % SUBLANES == sub]
            last = (base + taps[-1]) // SUBLANES * SUBLANES
            start = r * CONV_ROWS + sub
            window = c31buf_ref[start:start + last + CONV_ROWS, :]
            part = None
            for tap in taps:
                off = (base + tap) // SUBLANES * SUBLANES
                term = c31w_ref[tap:tap + 1, :] * window[off:off + CONV_ROWS]
                part = term if part is None else part + term
            acc = acc + part
        yn = _layernorm(acc, cng_ref[...], cnb_ref[...])
        ys_ref[3, r * CONV_ROWS:(r + 1) * CONV_ROWS, :] = (yn * _sigmoid(yn)).astype(BF16)

    c31buf_ref[0:C31_HALO, :] = c31buf_ref[TM:TM + C31_HALO, :]

    hb_prev = hb_in[...]
    merged = jnp.zeros((TM, D_MODEL), F32)
    for b in range(N_BRANCH):
        yb = _dot(ys_in[b], wbr_ref[b])
        half_gate = (_dot(hb_prev, wgate_ref[:, b * D_MODEL:(b + 1) * D_MODEL])
                     + bgate_ref[:, b * D_MODEL:(b + 1) * D_MODEL])
        merged = merged + (jnp.tanh(half_gate) * yb + yb)
    xo = xk_in[...] + _dot((0.5 * merged).astype(BF16), wo_ref[...])
    _store_token_major(out_ref, xo, TM)

    ms2 = jnp.mean(xo * xo, axis=-1, keepdims=True)
    h2 = (xo * lax.rsqrt(ms2 + EPS) * n2g_ref[...]).astype(BF16)
    logits = _dot(h2, wr_ref[...]) + br_ref[...]
    bucket_ref[...] = _route_bucket(logits.T)


def _mixing_kernel(token_major_in, *refs):
    common, (ys_a, ys_b, hb_a, hb_b, xk_a, xk_b) = refs[:-6], refs[-6:]
    kprev_ref, vprev_ref, c3buf_ref, c31buf_ref = common[-4:]
    s = pl.program_id(0)
    tile = jnp.minimum(s, N_TILES - 1)
    j = lax.rem(tile, TILES_PER_SEQ)

    @pl.when((j == 0) & (s < N_TILES))
    def _():
        kprev_ref[...] = jnp.zeros_like(kprev_ref)
        vprev_ref[...] = jnp.zeros_like(vprev_ref)
        c3buf_ref[0:C3_HALO, :] = jnp.zeros((C3_HALO, MIX_W), F32)
        c31buf_ref[0:C31_HALO, :] = jnp.zeros((C31_HALO, MIX_W), F32)

    @pl.when(s == 0)
    def _():
        ys_b[...] = jnp.zeros_like(ys_b)
        hb_b[...] = jnp.zeros_like(hb_b)
        xk_b[...] = jnp.zeros_like(xk_b)

    @pl.when(lax.rem(s, 2) == 0)
    def _():
        _mixing_step(token_major_in, j, *common, ys_a, hb_a, xk_a, ys_b, hb_b, xk_b)

    @pl.when(lax.rem(s, 2) == 1)
    def _():
        _mixing_step(token_major_in, j, *common, ys_b, hb_b, xk_b, ys_a, hb_a, xk_a)


def _const_spec(shape):
    zeros = (0,) * len(shape)
    return pl.BlockSpec(shape, lambda s: zeros, pipeline_mode=pl.Buffered(1))


def _mixing_call(x, cos_t, sin_t, n1g, w_in, sgu_g, sgu_b, sgu_w, sgu_bias, qg, kg, sinks,
                 c3w, c31w, c31b, cng, cnb, wbr, wgate, bgate, wo, n2g, wr, br):
    cur = lambda s: jnp.minimum(s, N_TILES - 1)
    prev = lambda s: jnp.maximum(s - 1, 0)
    tok = lambda width: pl.BlockSpec(
        (None, TM, width), lambda s: (cur(s) // TILES_PER_SEQ, lax.rem(cur(s), TILES_PER_SEQ), 0))
    token_major_in = x.ndim == 2
    in_specs = [
        pl.BlockSpec((TM * SUBLANES, LANES), lambda s: (cur(s), 0)) if token_major_in else tok(D_MODEL),
        tok(LANES), tok(LANES),
        _const_spec((1, D_MODEL)), _const_spec((D_MODEL, D_IN)),
        _const_spec((1, MIX_W)), _const_spec((1, MIX_W)),
        _const_spec((SGU_CHUNK, SGU_GROUPS * SGU_CHUNK)), _const_spec((SGU_CHUNK, MIX_W)),
        _const_spec((1, N_Q_HEADS * HEAD_DIM)), _const_spec((1, N_KV_HEADS * HEAD_DIM)),
        pl.BlockSpec(memory_space=pltpu.SMEM),
        _const_spec((SUBLANES, MIX_W)), _const_spec((C31_HALO, MIX_W)), _const_spec((1, MIX_W)),
        _const_spec((1, MIX_W)), _const_spec((1, MIX_W)),
        _const_spec((N_BRANCH, MIX_W, D_MODEL)), _const_spec((D_MODEL, N_BRANCH * D_MODEL)),
        _const_spec((1, N_BRANCH * D_MODEL)), _const_spec((D_MODEL, D_MODEL)),
        _const_spec((1, D_MODEL)), _const_spec((D_MODEL, LANES)), _const_spec((1, LANES)),
    ]
    return pl.pallas_call(
        functools.partial(_mixing_kernel, token_major_in),
        grid=(N_TILES + 1,),
        in_specs=in_specs,
        out_specs=[pl.BlockSpec((TM * SUBLANES, LANES), lambda s: (prev(s), 0)),
                   pl.BlockSpec((None, 1, TM), lambda s: (prev(s), 0, 0))],
        out_shape=[jax.ShapeDtypeStruct((NTOK * SUBLANES, LANES), F32),
                   jax.ShapeDtypeStruct((N_TILES, 1, TM), jnp.int32)],
        scratch_shapes=[
            pltpu.VMEM((WINDOW, LANES), F32), pltpu.VMEM((WINDOW, LANES), F32),
            pltpu.VMEM((C3_HALO + TM, MIX_W), F32), pltpu.VMEM((C31_HALO + TM, MIX_W), F32),
            pltpu.VMEM((N_BRANCH, TM, MIX_W), BF16), pltpu.VMEM((N_BRANCH, TM, MIX_W), BF16),
            pltpu.VMEM((TM, D_MODEL), BF16), pltpu.VMEM((TM, D_MODEL), BF16),
            pltpu.VMEM((TM, D_MODEL), F32), pltpu.VMEM((TM, D_MODEL), F32),
        ],
        compiler_params=pltpu.CompilerParams(
            dimension_semantics=("arbitrary",), vmem_limit_bytes=VMEM_LIMIT),
        name="mixing_block",
    )(x, cos_t, sin_t, n1g, w_in, sgu_g, sgu_b, sgu_w, sgu_bias, qg, kg, sinks,
      c3w, c31w, c31b, cng, cnb, wbr, wgate, bgate, wo, n2g, wr, br)


def _route_bucket(logits_t):
    g = [logits_t[i:i + 1, :] for i in range(N_GROUPS)]
    gmax, gidx = g[0], jnp.zeros(g[0].shape, jnp.int32)
    for i in range(1, N_GROUPS):
        better = g[i] > gmax
        gmax = jnp.where(better, g[i], gmax)
        gidx = jnp.where(better, i, gidx)
    e = []
    for i in range(EXPERTS_PER_GROUP):
        ei = logits_t[8 + i:9 + i, :]
        for grp in range(1, N_GROUPS):
            row = 8 + grp * EXPERTS_PER_GROUP + i
            ei = jnp.where(gidx == grp, logits_t[row:row + 1, :], ei)
        e.append(ei)
    v1, i1 = e[0], jnp.zeros_like(gidx)
    for i in range(1, EXPERTS_PER_GROUP):
        better = e[i] > v1
        v1 = jnp.where(better, e[i], v1)
        i1 = jnp.where(better, i, i1)
    v2, i2 = jnp.full_like(v1, -jnp.inf), jnp.zeros_like(gidx)
    for i in range(EXPERTS_PER_GROUP):
        better = (e[i] > v2) & (i1 != i)
        v2 = jnp.where(better, e[i], v2)
        i2 = jnp.where(better, i, i2)
    lo, hi = jnp.minimum(i1, i2), jnp.maximum(i1, i2)
    pair = jnp.where(lo == 0, hi - 1, jnp.where(lo == 1, hi + 1, 5))
    return gidx * PAIRS_PER_GROUP + pair


def _bucket_experts(b):
    g = b // PAIRS_PER_GROUP
    pair = b - g * PAIRS_PER_GROUP
    lo = jnp.where(pair < 3, 0, jnp.where(pair < 5, 1, 2))
    hi = jnp.where(pair < 3, pair + 1, jnp.where(pair < 5, pair - 1, 3))
    return g, g * EXPERTS_PER_GROUP + lo, g * EXPERTS_PER_GROUP + hi


def _sort_kernel(bucket_ref, pos_ref, tb_ref, meta_ref, carry_ref, base_ref):
    p = pl.program_id(0)
    j = pl.program_id(1)

    @pl.when((p == 0) & (j == 0))
    def _():
        carry_ref[...] = jnp.zeros_like(carry_ref)

    b = bucket_ref[...]
    rows = lax.broadcasted_iota(jnp.int32, (NB_PAD, TM), 0)
    onehot = jnp.where(rows == b, 1.0, 0.0)
    r_i = lax.broadcasted_iota(jnp.int32, (TM, TM), 0)
    c_i = lax.broadcasted_iota(jnp.int32, (TM, TM), 1)
    upper = jnp.where(r_i <= c_i, 1.0, 0.0).astype(BF16)
    prefix = _dot(onehot.astype(BF16), upper)
    total = prefix[:, TM - 1:TM]

    @pl.when(p == 0)
    def _():
        carry_ref[...] += total
        pos_ref[...] = jnp.zeros_like(pos_ref)

    @pl.when((p == 1) & (j == 0))
    def _():
        cnt = carry_ref[...]
        ntile = jnp.floor((cnt + (TMS - 1)) * (1.0 / TMS))
        rr = lax.broadcasted_iota(jnp.int32, (NB_PAD, NB_PAD), 0)
        cc = lax.broadcasted_iota(jnp.int32, (NB_PAD, NB_PAD), 1)
        strict_lower = jnp.where(cc < rr, 1.0, 0.0).astype(BF16)
        ntile_b = jnp.broadcast_to(ntile, (NB_PAD, LANES))
        tstart = _dot(strict_lower, ntile_b.astype(BF16))
        base_ref[...] = tstart[:, 0:1] * TMS
        carry_ref[...] = jnp.zeros_like(carry_ref)
        nused = jnp.broadcast_to(jnp.sum(ntile_b, axis=0, keepdims=True), (NB_PAD, LANES))
        col = lax.broadcasted_iota(jnp.int32, (NB_PAD, LANES), 1)
        meta = jnp.where(col == 0, tstart * TMS, jnp.where(col == 1, jnp.broadcast_to(cnt, (NB_PAD, LANES)), nused))
        meta_ref[...] = meta.astype(jnp.int32)
        tile = lax.broadcasted_iota(jnp.int32, (NB_PAD, TB_LEN), 1).astype(F32)
        row_ok = lax.broadcasted_iota(jnp.int32, (NB_PAD, TB_LEN), 0) < N_BUCKETS
        started = jnp.where(row_ok & (tstart[:, 0:1] <= tile), 1.0, 0.0)
        tb_ref[...] = (jnp.sum(started, axis=0, keepdims=True) - 1.0).astype(jnp.int32)

    @pl.when(p == 1)
    def _():
        slot = base_ref[...] + carry_ref[...] + prefix - 1.0
        pos_ref[...] = jnp.sum(onehot * slot, axis=0, keepdims=True).astype(jnp.int32)
        carry_ref[...] += total


def _sort_call(bucket):
    return pl.pallas_call(
        _sort_kernel,
        grid=(2, bucket.shape[0]),
        in_specs=[pl.BlockSpec((None, 1, TM), lambda p, j: (j, 0, 0))],
        out_specs=[
            pl.BlockSpec((None, None, 1, TM), lambda p, j: (p, j, 0, 0)),
            pl.BlockSpec((1, TB_LEN), lambda p, j: (0, 0)),
            pl.BlockSpec((NB_PAD, LANES), lambda p, j: (0, 0)),
        ],
        out_shape=[
            jax.ShapeDtypeStruct((2,) + bucket.shape, jnp.int32),
            jax.ShapeDtypeStruct((1, TB_LEN), jnp.int32),
            jax.ShapeDtypeStruct((NB_PAD, LANES), jnp.int32),
        ],
        scratch_shapes=[pltpu.VMEM((NB_PAD, 1), F32), pltpu.VMEM((NB_PAD, 1), F32)],
        compiler_params=pltpu.CompilerParams(dimension_semantics=("arbitrary", "arbitrary")),
        name="bucket_sort",
    )(bucket)


def _dispatch_kernel(base_ref, cnt_ref, nused_ref, pos_ref, x_ref, inv_ref, xs_hbm, ztile, sem, zsem):
    c = pl.program_id(0)

    def fill(s, carry):
        inv_ref[s] = NTOK + (s & (JUNK_ROWS - 1))
        return carry

    def zero_tile(t):
        return pltpu.make_async_copy(ztile, xs_hbm.at[pl.ds(t * (TMS * SUBLANES), TMS * SUBLANES)], zsem)

    @pl.when(c == 0)
    def _():
        ztile[...] = jnp.zeros_like(ztile)
        for wait in (False, True):
            for k in range(N_BUCKETS):
                n_tiles = (cnt_ref[k] + (TMS - 1)) // TMS

                @pl.when(n_tiles > 0)
                def _():
                    copy = zero_tile(base_ref[k] // TMS + n_tiles - 1)
                    copy.wait() if wait else copy.start()

            def unused(t, carry):
                copy = zero_tile(t)
                copy.wait() if wait else copy.start()
                return carry
            lax.fori_loop(nused_ref[0], NT_MAX, unused, 0)
        for k in range(N_BUCKETS):
            n_slots = ((cnt_ref[k] + (TMS - 1)) // TMS) * TMS
            lax.fori_loop(base_ref[k] + cnt_ref[k], base_ref[k] + n_slots, fill, 0)
        lax.fori_loop(nused_ref[0] * TMS, NT_MAX * TMS, fill, 0)

    def place(g, carry):
        for u in range(ISSUE_UNROLL):
            i = g * ISSUE_UNROLL + u
            slot = pos_ref[0, i]
            inv_ref[slot] = c * TM + i
            pltpu.make_async_copy(x_ref.at[pl.ds(i * SUBLANES, SUBLANES)],
                                  xs_hbm.at[pl.ds(slot * SUBLANES, SUBLANES)], sem).start(priority=u % 2)
        return carry
    lax.fori_loop(0, TM // ISSUE_UNROLL, place, 0)
    for _ in range(TM // WAIT_ROWS):
        pltpu.make_async_copy(x_ref.at[pl.ds(0, WAIT_ROWS * SUBLANES)],
                              xs_hbm.at[pl.ds(0, WAIT_ROWS * SUBLANES)], sem).wait()


def _dispatch_call(base, cnt, nused, pos, x_tm):
    return pl.pallas_call(
        _dispatch_kernel,
        grid_spec=pltpu.PrefetchScalarGridSpec(
            num_scalar_prefetch=3,
            grid=(pos.shape[0],),
            in_specs=[pl.BlockSpec((None, 1, TM), lambda c, *_: (c, 0, 0), memory_space=pltpu.SMEM),
                      pl.BlockSpec((TM * SUBLANES, LANES), lambda c, *_: (c, 0))],
            out_specs=[pl.BlockSpec(memory_space=pltpu.SMEM), pl.BlockSpec(memory_space=pl.ANY)],
            scratch_shapes=[pltpu.VMEM((TMS * SUBLANES, LANES), F32),
                            pltpu.SemaphoreType.DMA(()), pltpu.SemaphoreType.DMA(())],
        ),
        out_shape=[jax.ShapeDtypeStruct((NT_MAX * TMS,), jnp.int32),
                   jax.ShapeDtypeStruct((NT_MAX * TMS * SUBLANES, LANES), F32)],
        compiler_params=pltpu.CompilerParams(dimension_semantics=("arbitrary",), has_side_effects=True),
        name="moe_dispatch",
    )(base, cnt, nused, pos, x_tm)


def _load_token_major(ref, rows):
    return jnp.concatenate([ref[pl.ds(c, rows, stride=SUBLANES), :] for c in range(SUBLANES)], axis=1)


def _store_token_major(ref, value, rows):
    for c in range(SUBLANES):
        ref[pl.ds(c, rows, stride=SUBLANES), :] = value[:, c * LANES:(c + 1) * LANES]


def _to_rows_kernel(x_ref, out_ref):
    out_ref[...] = _load_token_major(x_ref, TM)


def _to_rows_call(x_tm, ntok):
    return pl.pallas_call(
        _to_rows_kernel,
        grid=(ntok // TM,),
        in_specs=[pl.BlockSpec((TM * SUBLANES, LANES), lambda i: (i, 0))],
        out_specs=pl.BlockSpec((TM, D_MODEL), lambda i: (i, 0)),
        out_shape=jax.ShapeDtypeStruct((ntok, D_MODEL), F32),
        compiler_params=pltpu.CompilerParams(dimension_semantics=("arbitrary",)),
        name="token_major_to_rows",
    )(x_tm)


def _moe_kernel(tb_ref, nused_ref, inv_ref, x_ref, n2g_ref, wr_ref, br_ref,
                wg_lo_ref, wu_lo_ref, wd_lo_ref, wg_hi_ref, wu_hi_ref, wd_hi_ref,
                out_hbm, obuf, ssem):
    i = pl.program_id(0)
    nused = nused_ref[0]
    slot = lax.rem(i, 2)

    def scatter_row(r, s):
        tok = inv_ref[0, r]
        return pltpu.make_async_copy(obuf.at[s, pl.ds(r * SUBLANES, SUBLANES)],
                                     out_hbm.at[pl.ds(tok * SUBLANES, SUBLANES)], ssem.at[s])

    def start_rows(row_copy):
        def body(g, carry):
            for u in range(ISSUE_UNROLL):
                row_copy(g * ISSUE_UNROLL + u).start(priority=u % 2)
            return carry
        lax.fori_loop(0, TMS // ISSUE_UNROLL, body, 0)

    def wait_tile(sem, s):
        for _ in range(TMS // WAIT_ROWS):
            pltpu.make_async_copy(obuf.at[s, pl.ds(0, WAIT_ROWS * SUBLANES)],
                                  out_hbm.at[pl.ds(0, WAIT_ROWS * SUBLANES)], sem.at[s]).wait()

    @pl.when(i == 0)
    def _():
        obuf[...] = jnp.zeros_like(obuf)
        for s in range(2):
            junk = pltpu.make_async_copy(
                obuf.at[s], out_hbm.at[pl.ds((NTOK + s * TMS) * SUBLANES, TMS * SUBLANES)], ssem.at[s])
            junk.start()
            junk.wait()

    @pl.when(i < nused)
    def _():
        @pl.when(i >= 2)
        def _():
            wait_tile(ssem, slot)
        x = _load_token_major(x_ref, TMS)
        ms = jnp.mean(x * x, axis=-1, keepdims=True)
        hb = (x * lax.rsqrt(ms + EPS) * n2g_ref[...]).astype(BF16)
        logits = _dot(hb, wr_ref[...]) + br_ref[...]
        g, e_lo, e_hi = _bucket_experts(tb_ref[i])
        lane = lax.broadcasted_iota(jnp.int32, (TMS, LANES), 1)
        pick = lambda l: jnp.sum(jnp.where(lane == l, logits, 0.0), axis=-1, keepdims=True)
        lg, l_lo, l_hi = pick(g), pick(8 + e_lo), pick(8 + e_hi)
        gsum = jnp.sum(jnp.where(lane < N_GROUPS, jnp.exp(logits - lg), 0.0), axis=-1, keepdims=True)
        g_w = 1.0 / gsum
        w_lo = g_w / (1.0 + jnp.exp(l_hi - l_lo))
        w_hi = g_w / (1.0 + jnp.exp(l_lo - l_hi))

        def expert(wg_ref, wu_ref, wd_ref, w):
            a = _dot(hb, wg_ref[...])
            hid = a * _sigmoid(a) * _dot(hb, wu_ref[...]) * w
            return _dot(hid.astype(BF16), wd_ref[...])
        y = (x + expert(wg_lo_ref, wu_lo_ref, wd_lo_ref, w_lo)
             + expert(wg_hi_ref, wu_hi_ref, wd_hi_ref, w_hi))
        _store_token_major(obuf.at[slot], y, TMS)

        start_rows(lambda r: scatter_row(r, slot))

    @pl.when(i == nused - 1)
    def _():
        wait_tile(ssem, slot)

        @pl.when(i >= 1)
        def _():
            wait_tile(ssem, 1 - slot)


def _moe_call(tile_bucket, nused, inv, xs_tm, n2g, wr, br, wg, wu, wd):
    def used(i, nu):
        return jnp.minimum(i, nu[0] - 1)

    def w_spec(shape, which):
        def index(i, tb, nu):
            return (_bucket_experts(tb[used(i, nu)])[which], 0, 0)
        return pl.BlockSpec((None,) + shape, index)

    const = lambda shape: pl.BlockSpec(shape, lambda i, tb, nu: (0, 0))
    up, down = (D_MODEL, D_EXPERT), (D_EXPERT, D_MODEL)
    inv3 = inv.reshape(NT_MAX, 1, TMS)
    return pl.pallas_call(
        _moe_kernel,
        grid_spec=pltpu.PrefetchScalarGridSpec(
            num_scalar_prefetch=2,
            grid=(NT_MAX,),
            in_specs=[
                pl.BlockSpec((None, 1, TMS), lambda i, tb, nu: (i, 0, 0), memory_space=pltpu.SMEM),
                pl.BlockSpec((TMS * SUBLANES, LANES), lambda i, tb, nu: (used(i, nu), 0)),
                const((1, D_MODEL)), const((D_MODEL, LANES)), const((1, LANES)),
                w_spec(up, 1), w_spec(up, 1), w_spec(down, 1),
                w_spec(up, 2), w_spec(up, 2), w_spec(down, 2),
            ],
            out_specs=pl.BlockSpec(memory_space=pl.ANY),
            scratch_shapes=[pltpu.VMEM((2, TMS * SUBLANES, LANES), F32), pltpu.SemaphoreType.DMA((2,))],
        ),
        out_shape=jax.ShapeDtypeStruct(((NTOK + JUNK_ROWS) * SUBLANES, LANES), F32),
        compiler_params=pltpu.CompilerParams(
            dimension_semantics=("arbitrary",), vmem_limit_bytes=VMEM_LIMIT, has_side_effects=True),
        name="hier_moe",
    )(tile_bucket, nused, inv3, xs_tm, n2g, wr, br, wg, wu, wd, wg, wu, wd)


def _rope_tables(positions):
    inv = ROPE_THETA ** (-jnp.arange(0, ROT_DIM, 2, dtype=F32) / ROT_DIM)
    inv_head = jnp.concatenate([inv, inv, jnp.zeros((HEAD_DIM - ROT_DIM,), F32)])
    inv_lane = jnp.concatenate([inv_head, inv_head])
    ang = positions.astype(F32)[..., None] * inv_lane
    return jnp.cos(ang), jnp.sin(ang)


def kernel(x, positions, norm1_g, w_in, sgu_ln_g, sgu_ln_b, sgu_w, sgu_b, q_norm_g, k_norm_g, sinks, conv3_w, conv31_w, conv31_b, cnorm_g, cnorm_b, w_branch, w_gate, b_gate, w_o, norm2_g, w_group, b_group, w_expert, b_expert, w_e_gate, w_e_up, w_e_down):
    bsz, seq, d = x.shape
    assert (bsz, seq, d) == (BATCH, SEQ, D_MODEL)
    ntok = bsz * seq
    depth = norm1_g.shape[0]
    cos_t, sin_t = _rope_tables(positions)
    row = lambda t: t.reshape(1, -1)
    for l in range(depth):
        sgu_wcat = jnp.transpose(sgu_w[l], (1, 0, 2)).reshape(SGU_CHUNK, SGU_GROUPS * SGU_CHUNK)
        sgu_bias = jnp.repeat(sgu_b[l].T, MIX_W // SGU_GROUPS, axis=1)
        c3w = jnp.zeros((SUBLANES, MIX_W), F32).at[:SHORT_CONV].set(conv3_w[l])
        c31w = jnp.zeros((C31_HALO, MIX_W), F32).at[:CONFORMER_CONV].set(conv31_w[l])
        wr = jnp.zeros((d, LANES), F32).at[:, 0:N_GROUPS].set(w_group[l]).at[:, 8:8 + N_EXPERTS].set(w_expert[l])
        wr = wr.astype(BF16)
        br = jnp.zeros((1, LANES), F32).at[0, 0:N_GROUPS].set(b_group[l]).at[0, 8:8 + N_EXPERTS].set(b_expert[l])
        x, bucket = _mixing_call(
            x, cos_t, sin_t, row(norm1_g[l]), w_in[l].astype(BF16),
            row(sgu_ln_g[l]), row(sgu_ln_b[l]), sgu_wcat, sgu_bias,
            row(jnp.tile(q_norm_g[l], N_Q_HEADS)), row(jnp.tile(k_norm_g[l], N_KV_HEADS)), sinks[l],
            c3w, c31w, row(conv31_b[l]), row(cnorm_g[l]), row(cnorm_b[l]),
            w_branch[l].astype(BF16), (0.5 * w_gate[l]).astype(BF16), row(0.5 * b_gate[l]), w_o[l].astype(BF16),
            row(norm2_g[l]), wr, br)
        pos2, tile_bucket, meta = _sort_call(bucket)
        nused = meta[0, 2:3]
        inv, xs = _dispatch_call(meta[:, 0], meta[:, 1], nused, pos2[1], x)
        x = _moe_call(tile_bucket[0], nused, inv, xs, row(norm2_g[l]), wr, br,
                      w_e_gate[l].astype(BF16), w_e_up[l].astype(BF16), w_e_down[l].astype(BF16))
    return _to_rows_call(x, ntok).reshape(bsz, seq, d)
```

```python
import functools

import jax
import jax.numpy as jnp
from jax import lax
from jax.experimental import pallas as pl
from jax.experimental.pallas import tpu as pltpu

D_MODEL = 1024
N_BRANCH = 4
MIX_W = 256
SGU_GROUPS = 4
SGU_CHUNK = 128
N_Q_HEADS = 4
N_KV_HEADS = 2
HEAD_DIM = 64
WINDOW = 128
ROT_DIM = HEAD_DIM // 4
ROPE_THETA = 500000.0
SHORT_CONV = 3
CONFORMER_CONV = 31
N_GROUPS = 4
EXPERTS_PER_GROUP = 4
N_EXPERTS = N_GROUPS * EXPERTS_PER_GROUP
D_EXPERT = 256
EPS = 1e-6
D_IN = 2304

BATCH = 8
SEQ = 4096
PAIRS_PER_GROUP = EXPERTS_PER_GROUP * (EXPERTS_PER_GROUP - 1) // 2
N_BUCKETS = N_GROUPS * PAIRS_PER_GROUP
NB_PAD = 32

LANES = 128
SUBLANES = 8
TM = 512
TMS = 256
NTOK = BATCH * SEQ
N_TILES = NTOK // TM
TILES_PER_SEQ = SEQ // TM
NT_MAX = NTOK // TMS + N_BUCKETS
JUNK_ROWS = 2 * TMS
TB_LEN = 256
SORT_SUB = 4
ISSUE_UNROLL = 8
WAIT_ROWS = 128
C3_HALO = SUBLANES
C31_HALO = 32
CONV_ROWS = 64
VMEM_LIMIT = 56 * 1024 * 1024
NEG = -1e30

F32 = jnp.float32
BF16 = jnp.bfloat16


def _dot(a, b):
    return jnp.dot(a, b, preferred_element_type=F32)


def _gelu_tanh(x):
    c = 0.7978845608028654
    return 0.5 * x * (1.0 + jnp.tanh(c * (x + 0.044715 * (x * x * x))))


def _sigmoid(x):
    return 0.5 * jnp.tanh(0.5 * x) + 0.5


def _layernorm(x, g, b):
    mu = jnp.mean(x, axis=-1, keepdims=True)
    xc = x - mu
    var = jnp.mean(xc * xc, axis=-1, keepdims=True)
    return xc * lax.rsqrt(var + EPS) * g + b


def _head_meansq(t, width):
    r = lax.broadcasted_iota(jnp.int32, (width, width), 0) // HEAD_DIM
    c = lax.broadcasted_iota(jnp.int32, (width, width), 1) // HEAD_DIM
    bd = jnp.where(r == c, 1.0 / HEAD_DIM, 0.0).astype(BF16)
    t2 = t * t
    hi = t2.astype(BF16)
    lo = (t2 - hi.astype(F32)).astype(BF16)
    return _dot(hi, bd) + _dot(lo, bd)


def _rope(t, c, s1, s2):
    w = t.shape[-1]
    half = ROT_DIM // 2
    return t * c + pltpu.roll(t, w - half, axis=1) * s1 + pltpu.roll(t, half, axis=1) * s2


def _mixing_step(token_major_in, j, x_ref, cos_ref, sin_ref, n1g_ref, win_ref,
                 sgu_g_ref, sgu_b_ref, sgu_w_ref, sgu_bias_ref,
                 qg_ref, kg_ref, sinks_ref,
                 c3w_ref, c31w_ref, c31b_ref, cng_ref, cnb_ref,
                 wbr_ref, wgate_ref, bgate_ref, wo_ref,
                 n2g_ref, wr_ref, br_ref,
                 out_ref, bucket_ref,
                 kprev_ref, vprev_ref, c3buf_ref, c31buf_ref,
                 ys_ref, hb_ref, xk_ref, ys_in, hb_in, xk_in):
    x = _load_token_major(x_ref, TM) if token_major_in else x_ref[...]
    xk_ref[...] = x
    ms = jnp.mean(x * x, axis=-1, keepdims=True)
    hb = (x * lax.rsqrt(ms + EPS) * n1g_ref[...]).astype(BF16)
    hb_ref[...] = hb

    uv = _dot(hb, win_ref[:, 0:512])
    u = _gelu_tanh(uv[:, 0:MIX_W])
    v = _layernorm(_gelu_tanh(uv[:, MIX_W:2 * MIX_W]), sgu_g_ref[...], sgu_b_ref[...])
    tt = lax.broadcasted_iota(jnp.int32, (SGU_CHUNK, SGU_GROUPS * SGU_CHUNK), 0)
    ss = lax.broadcasted_iota(jnp.int32, (SGU_CHUNK, SGU_GROUPS * SGU_CHUNK), 1) % SGU_CHUNK
    wcat = jnp.where(ss <= tt, sgu_w_ref[...], 0.0).astype(BF16)
    lane_grp = lax.broadcasted_iota(jnp.int32, (SGU_CHUNK, MIX_W), 1) // (MIX_W // SGU_GROUPS)
    for c in range(TM // SGU_CHUNK):
        rows = slice(c * SGU_CHUNK, (c + 1) * SGU_CHUNK)
        vc = v[rows]
        vstack = jnp.concatenate(
            [jnp.where(lane_grp == g, vc, 0.0) for g in range(SGU_GROUPS)], axis=0).astype(BF16)
        z = _dot(wcat, vstack) + sgu_bias_ref[...]
        ys_ref[0, rows, :] = (u[rows] * z).astype(BF16)

    qkv = _dot(hb, win_ref[:, 512:1024])
    rc, sin_t = cos_ref[...], sin_ref[...]
    head_dim_idx = lax.broadcasted_iota(jnp.int32, (1, LANES), 1) % HEAD_DIM
    rs1 = jnp.where(head_dim_idx < ROT_DIM // 2, -sin_t, 0.0)
    rs2 = jnp.where((head_dim_idx >= ROT_DIM // 2) & (head_dim_idx < ROT_DIM), sin_t, 0.0)
    q = qkv[:, 0:256]
    q = q * lax.rsqrt(_head_meansq(q, 256) + EPS) * qg_ref[...]
    q = _rope(q, jnp.concatenate([rc, rc], axis=1), jnp.concatenate([rs1, rs1], axis=1),
              jnp.concatenate([rs2, rs2], axis=1)) * (HEAD_DIM ** -0.5)
    k = qkv[:, 256:384]
    k = k * lax.rsqrt(_head_meansq(k, 128) + EPS) * kg_ref[...]
    k = _rope(k, rc, rs1, rs2)
    vv = qkv[:, 384:512]
    kfull = jnp.concatenate([kprev_ref[...], k], axis=0)
    vfull = jnp.concatenate([vprev_ref[...], vv], axis=0)
    kprev_ref[...] = k[TM - WINDOW:TM]
    vprev_ref[...] = vv[TM - WINDOW:TM]
    low_full = lax.broadcasted_iota(jnp.int32, (WINDOW + TM, LANES), 1) < HEAD_DIM
    krot = pltpu.roll(kfull, HEAD_DIM, axis=1)
    vrot = pltpu.roll(vfull, HEAD_DIM, axis=1)
    kdup = [jnp.where(low_full, kfull, krot).astype(BF16), jnp.where(low_full, krot, kfull).astype(BF16)]
    vdup = [jnp.where(low_full, vfull, vrot).astype(BF16), jnp.where(low_full, vrot, vfull).astype(BF16)]
    low = lax.broadcasted_iota(jnp.int32, (WINDOW, LANES), 1) < HEAD_DIM
    row2 = lax.broadcasted_iota(jnp.int32, (2 * WINDOW, 2 * WINDOW), 0)
    qi = row2 % WINDOW
    ki = lax.broadcasted_iota(jnp.int32, (2 * WINDOW, 2 * WINDOW), 1)
    in_prev = (ki < WINDOW) & (ki > qi)
    in_cur = (ki >= WINDOW) & (ki - WINDOW <= qi)
    is_g0 = lax.broadcasted_iota(jnp.int32, (2 * WINDOW, 1), 0) < WINDOW
    for n in range(TM // WINDOW):
        if n == 0:
            valid = (in_prev & (j > 0)) | in_cur
        else:
            valid = in_prev | in_cur
        cols = []
        for h in range(N_KV_HEADS):
            qcol = q[n * WINDOW:(n + 1) * WINDOW, h * LANES:(h + 1) * LANES]
            qs = jnp.concatenate([jnp.where(low, qcol, 0.0), jnp.where(low, 0.0, qcol)], axis=0).astype(BF16)
            kk = kdup[h][n * WINDOW:(n + 2) * WINDOW]
            sc = lax.dot_general(qs, kk, (((1,), (1,)), ((), ())), preferred_element_type=F32)
            sc = jnp.where(valid, sc, NEG)
            sink = jnp.where(is_g0, sinks_ref[2 * h], sinks_ref[2 * h + 1])
            m = jnp.maximum(jnp.max(sc, axis=-1, keepdims=True), sink)
            p = jnp.exp(sc - m)
            denom = jnp.sum(p, axis=-1, keepdims=True) + jnp.exp(sink - m)
            o = _dot(p.astype(BF16), vdup[h][n * WINDOW:(n + 2) * WINDOW]) / denom
            cols.append(jnp.where(low, o[0:WINDOW], o[WINDOW:2 * WINDOW]))
        ys_ref[1, n * WINDOW:(n + 1) * WINDOW, :] = jnp.concatenate(cols, axis=1).astype(BF16)

    cc = _dot(hb, win_ref[:, 1024:1792])
    c3buf_ref[C3_HALO:C3_HALO + TM, :] = cc[:, 0:256] * cc[:, 512:768]
    conv = c3w_ref[0:1, :] * c3buf_ref[C3_HALO - 2:C3_HALO - 2 + TM, :]
    conv = conv + c3w_ref[1:2, :] * c3buf_ref[C3_HALO - 1:C3_HALO - 1 + TM, :]
    conv = conv + c3w_ref[2:3, :] * c3buf_ref[C3_HALO:C3_HALO + TM, :]
    ys_ref[2] = (cc[:, 256:512] * conv).astype(BF16)
    c3buf_ref[0:C3_HALO, :] = c3buf_ref[TM:TM + C3_HALO, :]

    dd = _dot(hb, win_ref[:, 1792:2304])
    c31buf_ref[C31_HALO:C31_HALO + TM, :] = dd[:, 0:256] * _sigmoid(dd[:, 256:512])
    base = C31_HALO - (CONFORMER_CONV - 1)
    for r in range(TM // CONV_ROWS):
        acc = jnp.broadcast_to(c31b_ref[...], (CONV_ROWS, MIX_W))
        for sub in range(SUBLANES):
            taps = [t for t in range(CONFORMER_CONV) if (base + t) % SUBLANES == sub]
            last = (base + taps[-1]) // SUBLANES * SUBLANES
            start = r * CONV_ROWS + sub
            window = c31buf_ref[start:start + last + CONV_ROWS, :]
            part = None
            for tap in taps:
                off = (base + tap) // SUBLANES * SUBLANES
                term = c31w_ref[tap:tap + 1, :] * window[off:off + CONV_ROWS]
                part = term if part is None else part + term
            acc = acc + part
        yn = _layernorm(acc, cng_ref[...], cnb_ref[...])
        ys_ref[3, r * CONV_ROWS:(r + 1) * CONV_ROWS, :] = (yn * _sigmoid(yn)).astype(BF16)

    c31buf_ref[0:C31_HALO, :] = c31buf_ref[TM:TM + C31_HALO, :]

    hb_prev = hb_in[...]
    merged = jnp.zeros((TM, D_MODEL), F32)
    for b in range(N_BRANCH):
        yb = _dot(ys_in[b], wbr_ref[b])
        half_gate = (_dot(hb_prev, wgate_ref[:, b * D_MODEL:(b + 1) * D_MODEL])
                     + bgate_ref[:, b * D_MODEL:(b + 1) * D_MODEL])
        merged = merged + (jnp.tanh(half_gate) * yb + yb)
    xo = xk_in[...] + _dot((0.5 * merged).astype(BF16), wo_ref[...])
    _store_token_major(out_ref, xo, TM)

    ms2 = jnp.mean(xo * xo, axis=-1, keepdims=True)
    h2 = (xo * lax.rsqrt(ms2 + EPS) * n2g_ref[...]).astype(BF16)
    logits = _dot(h2, wr_ref[...]) + br_ref[...]
    bucket_ref[...] = _route_bucket(logits.T)


def _mixing_kernel(token_major_in, *refs):
    common, (ys_a, ys_b, hb_a, hb_b, xk_a, xk_b) = refs[:-6], refs[-6:]
    kprev_ref, vprev_ref, c3buf_ref, c31buf_ref = common[-4:]
    s = pl.program_id(0)
    tile = jnp.minimum(s, N_TILES - 1)
    j = lax.rem(tile, TILES_PER_SEQ)

    @pl.when((j == 0) & (s < N_TILES))
    def _():
        kprev_ref[...] = jnp.zeros_like(kprev_ref)
        vprev_ref[...] = jnp.zeros_like(vprev_ref)
        c3buf_ref[0:C3_HALO, :] = jnp.zeros((C3_HALO, MIX_W), F32)
        c31buf_ref[0:C31_HALO, :] = jnp.zeros((C31_HALO, MIX_W), F32)

    @pl.when(s == 0)
    def _():
        ys_b[...] = jnp.zeros_like(ys_b)
        hb_b[...] = jnp.zeros_like(hb_b)
        xk_b[...] = jnp.zeros_like(xk_b)

    @pl.when(lax.rem(s, 2) == 0)
    def _():
        _mixing_step(token_major_in, j, *common, ys_a, hb_a, xk_a, ys_b, hb_b, xk_b)

    @pl.when(lax.rem(s, 2) == 1)
    def _():
        _mixing_step(token_major_in, j, *common, ys_b, hb_b, xk_b, ys_a, hb_a, xk_a)


def _const_spec(shape):
    zeros = (0,) * len(shape)
    return pl.BlockSpec(shape, lambda s: zeros, pipeline_mode=pl.Buffered(1))


def _mixing_call(x, cos_t, sin_t, n1g, w_in, sgu_g, sgu_b, sgu_w, sgu_bias, qg, kg, sinks,
                 c3w, c31w, c31b, cng, cnb, wbr, wgate, bgate, wo, n2g, wr, br):
    cur = lambda s: jnp.minimum(s, N_TILES - 1)
    prev = lambda s: jnp.maximum(s - 1, 0)
    tok = lambda width: pl.BlockSpec(
        (None, TM, width), lambda s: (cur(s) // TILES_PER_SEQ, lax.rem(cur(s), TILES_PER_SEQ), 0))
    token_major_in = x.ndim == 2
    in_specs = [
        pl.BlockSpec((TM * SUBLANES, LANES), lambda s: (cur(s), 0)) if token_major_in else tok(D_MODEL),
        tok(LANES), tok(LANES),
        _const_spec((1, D_MODEL)), _const_spec((D_MODEL, D_IN)),
        _const_spec((1, MIX_W)), _const_spec((1, MIX_W)),
        _const_spec((SGU_CHUNK, SGU_GROUPS * SGU_CHUNK)), _const_spec((SGU_CHUNK, MIX_W)),
        _const_spec((1, N_Q_HEADS * HEAD_DIM)), _const_spec((1, N_KV_HEADS * HEAD_DIM)),
        pl.BlockSpec(memory_space=pltpu.SMEM),
        _const_spec((SUBLANES, MIX_W)), _const_spec((C31_HALO, MIX_W)), _const_spec((1, MIX_W)),
        _const_spec((1, MIX_W)), _const_spec((1, MIX_W)),
        _const_spec((N_BRANCH, MIX_W, D_MODEL)), _const_spec((D_MODEL, N_BRANCH * D_MODEL)),
        _const_spec((1, N_BRANCH * D_MODEL)), _const_spec((D_MODEL, D_MODEL)),
        _const_spec((1, D_MODEL)), _const_spec((D_MODEL, LANES)), _const_spec((1, LANES)),
    ]
    return pl.pallas_call(
        functools.partial(_mixing_kernel, token_major_in),
        grid=(N_TILES + 1,),
        in_specs=in_specs,
        out_specs=[pl.BlockSpec((TM * SUBLANES, LANES), lambda s: (prev(s), 0)),
                   pl.BlockSpec((None, 1, TM), lambda s: (prev(s), 0, 0))],
        out_shape=[jax.ShapeDtypeStruct((NTOK * SUBLANES, LANES), F32),
                   jax.ShapeDtypeStruct((N_TILES, 1, TM), jnp.int32)],
        scratch_shapes=[
            pltpu.VMEM((WINDOW, LANES), F32), pltpu.VMEM((WINDOW, LANES), F32),
            pltpu.VMEM((C3_HALO + TM, MIX_W), F32), pltpu.VMEM((C31_HALO + TM, MIX_W), F32),
            pltpu.VMEM((N_BRANCH, TM, MIX_W), BF16), pltpu.VMEM((N_BRANCH, TM, MIX_W), BF16),
            pltpu.VMEM((TM, D_MODEL), BF16), pltpu.VMEM((TM, D_MODEL), BF16),
            pltpu.VMEM((TM, D_MODEL), F32), pltpu.VMEM((TM, D_MODEL), F32),
        ],
        compiler_params=pltpu.CompilerParams(
            dimension_semantics=("arbitrary",), vmem_limit_bytes=VMEM_LIMIT),
        name="mixing_block",
    )(x, cos_t, sin_t, n1g, w_in, sgu_g, sgu_b, sgu_w, sgu_bias, qg, kg, sinks,
      c3w, c31w, c31b, cng, cnb, wbr, wgate, bgate, wo, n2g, wr, br)


def _route_bucket(logits_t):
    g = [logits_t[i:i + 1, :] for i in range(N_GROUPS)]
    gmax, gidx = g[0], jnp.zeros(g[0].shape, jnp.int32)
    for i in range(1, N_GROUPS):
        better = g[i] > gmax
        gmax = jnp.where(better, g[i], gmax)
        gidx = jnp.where(better, i, gidx)
    e = []
    for i in range(EXPERTS_PER_GROUP):
        ei = logits_t[8 + i:9 + i, :]
        for grp in range(1, N_GROUPS):
            row = 8 + grp * EXPERTS_PER_GROUP + i
            ei = jnp.where(gidx == grp, logits_t[row:row + 1, :], ei)
        e.append(ei)
    v1, i1 = e[0], jnp.zeros_like(gidx)
    for i in range(1, EXPERTS_PER_GROUP):
        better = e[i] > v1
        v1 = jnp.where(better, e[i], v1)
        i1 = jnp.where(better, i, i1)
    v2, i2 = jnp.full_like(v1, -jnp.inf), jnp.zeros_like(gidx)
    for i in range(EXPERTS_PER_GROUP):
        better = (e[i] > v2) & (i1 != i)
        v2 = jnp.where(better, e[i], v2)
        i2 = jnp.where(better, i, i2)
    lo, hi = jnp.minimum(i1, i2), jnp.maximum(i1, i2)
    pair = jnp.where(lo == 0, hi - 1, jnp.where(lo == 1, hi + 1, 5))
    return gidx * PAIRS_PER_GROUP + pair


def _bucket_experts(b):
    g = b // PAIRS_PER_GROUP
    pair = b - g * PAIRS_PER_GROUP
    lo = jnp.where(pair < 3, 0, jnp.where(pair < 5, 1, 2))
    hi = jnp.where(pair < 3, pair + 1, jnp.where(pair < 5, pair - 1, 3))
    return g, g * EXPERTS_PER_GROUP + lo, g * EXPERTS_PER_GROUP + hi


def _sort_kernel(bucket_ref, pos_ref, tb_ref, meta_ref, carry_ref, base_ref, upper_ref):
    p = pl.program_id(0)
    j = pl.program_id(1)
    rows = lax.broadcasted_iota(jnp.int32, (NB_PAD, TM), 0)

    def one_hot(q):
        return jnp.where(rows == bucket_ref[q], 1.0, 0.0)

    @pl.when((p == 0) & (j == 0))
    def _():
        carry_ref[...] = jnp.zeros_like(carry_ref)
        r_i = lax.broadcasted_iota(jnp.int32, (TM, TM), 0)
        c_i = lax.broadcasted_iota(jnp.int32, (TM, TM), 1)
        upper_ref[...] = jnp.where(r_i <= c_i, 1.0, 0.0).astype(BF16)

    @pl.when(p == 0)
    def _():
        for q in range(SORT_SUB):
            carry_ref[...] += jnp.sum(one_hot(q), axis=1, keepdims=True)
        pos_ref[...] = jnp.zeros_like(pos_ref)

    @pl.when((p == 1) & (j == 0))
    def _():
        cnt = carry_ref[...]
        ntile = jnp.floor((cnt + (TMS - 1)) * (1.0 / TMS))
        rr = lax.broadcasted_iota(jnp.int32, (NB_PAD, NB_PAD), 0)
        cc = lax.broadcasted_iota(jnp.int32, (NB_PAD, NB_PAD), 1)
        strict_lower = jnp.where(cc < rr, 1.0, 0.0).astype(BF16)
        ntile_b = jnp.broadcast_to(ntile, (NB_PAD, LANES))
        tstart = _dot(strict_lower, ntile_b.astype(BF16))
        base_ref[...] = tstart[:, 0:1] * TMS
        carry_ref[...] = jnp.zeros_like(carry_ref)
        nused = jnp.broadcast_to(jnp.sum(ntile_b, axis=0, keepdims=True), (NB_PAD, LANES))
        col = lax.broadcasted_iota(jnp.int32, (NB_PAD, LANES), 1)
        meta = jnp.where(col == 0, tstart * TMS, jnp.where(col == 1, jnp.broadcast_to(cnt, (NB_PAD, LANES)), nused))
        meta_ref[...] = meta.astype(jnp.int32)
        tile = lax.broadcasted_iota(jnp.int32, (NB_PAD, TB_LEN), 1).astype(F32)
        row_ok = lax.broadcasted_iota(jnp.int32, (NB_PAD, TB_LEN), 0) < N_BUCKETS
        started = jnp.where(row_ok & (tstart[:, 0:1] <= tile), 1.0, 0.0)
        tb_ref[...] = (jnp.sum(started, axis=0, keepdims=True) - 1.0).astype(jnp.int32)

    @pl.when(p == 1)
    def _():
        for q in range(SORT_SUB):
            onehot = one_hot(q)
            prefix = _dot(onehot.astype(BF16), upper_ref[...])
            slot = base_ref[...] + carry_ref[...] + prefix - 1.0
            pos_ref[q] = jnp.sum(onehot * slot, axis=0, keepdims=True).astype(jnp.int32)
            carry_ref[...] += prefix[:, TM - 1:TM]


def _sort_call(bucket):
    return pl.pallas_call(
        _sort_kernel,
        grid=(2, bucket.shape[0] // SORT_SUB),
        in_specs=[pl.BlockSpec((SORT_SUB, 1, TM), lambda p, j: (j, 0, 0))],
        out_specs=[
            pl.BlockSpec((None, SORT_SUB, 1, TM), lambda p, j: (p, j, 0, 0)),
            pl.BlockSpec((1, TB_LEN), lambda p, j: (0, 0)),
            pl.BlockSpec((NB_PAD, LANES), lambda p, j: (0, 0)),
        ],
        out_shape=[
            jax.ShapeDtypeStruct((2,) + bucket.shape, jnp.int32),
            jax.ShapeDtypeStruct((1, TB_LEN), jnp.int32),
            jax.ShapeDtypeStruct((NB_PAD, LANES), jnp.int32),
        ],
        scratch_shapes=[pltpu.VMEM((NB_PAD, 1), F32), pltpu.VMEM((NB_PAD, 1), F32),
                        pltpu.VMEM((TM, TM), BF16)],
        compiler_params=pltpu.CompilerParams(dimension_semantics=("arbitrary", "arbitrary")),
        name="bucket_sort",
    )(bucket)


def _dispatch_kernel(base_ref, cnt_ref, nused_ref, pos_ref, x_ref, inv_ref, xs_hbm, ztile, sem, zsem):
    c = pl.program_id(0)

    def fill(s, carry):
        inv_ref[s] = NTOK + (s & (JUNK_ROWS - 1))
        return carry

    def zero_tile(t):
        return pltpu.make_async_copy(ztile, xs_hbm.at[pl.ds(t * (TMS * SUBLANES), TMS * SUBLANES)], zsem)

    @pl.when(c == 0)
    def _():
        ztile[...] = jnp.zeros_like(ztile)
        for wait in (False, True):
            for k in range(N_BUCKETS):
                n_tiles = (cnt_ref[k] + (TMS - 1)) // TMS

                @pl.when(n_tiles > 0)
                def _():
                    copy = zero_tile(base_ref[k] // TMS + n_tiles - 1)
                    copy.wait() if wait else copy.start()

            def unused(t, carry):
                copy = zero_tile(t)
                copy.wait() if wait else copy.start()
                return carry
            lax.fori_loop(nused_ref[0], NT_MAX, unused, 0)
        for k in range(N_BUCKETS):
            n_slots = ((cnt_ref[k] + (TMS - 1)) // TMS) * TMS
            lax.fori_loop(base_ref[k] + cnt_ref[k], base_ref[k] + n_slots, fill, 0)
        lax.fori_loop(nused_ref[0] * TMS, NT_MAX * TMS, fill, 0)

    def place(g, carry):
        for u in range(ISSUE_UNROLL):
            i = g * ISSUE_UNROLL + u
            slot = pos_ref[0, i]
            inv_ref[slot] = c * TM + i
            pltpu.make_async_copy(x_ref.at[pl.ds(i * SUBLANES, SUBLANES)],
                                  xs_hbm.at[pl.ds(slot * SUBLANES, SUBLANES)], sem).start(priority=u % 2)
        return carry
    lax.fori_loop(0, TM // ISSUE_UNROLL, place, 0)
    for _ in range(TM // WAIT_ROWS):
        pltpu.make_async_copy(x_ref.at[pl.ds(0, WAIT_ROWS * SUBLANES)],
                              xs_hbm.at[pl.ds(0, WAIT_ROWS * SUBLANES)], sem).wait()


def _dispatch_call(base, cnt, nused, pos, x_tm):
    return pl.pallas_call(
        _dispatch_kernel,
        grid_spec=pltpu.PrefetchScalarGridSpec(
            num_scalar_prefetch=3,
            grid=(pos.shape[0],),
            in_specs=[pl.BlockSpec((None, 1, TM), lambda c, *_: (c, 0, 0), memory_space=pltpu.SMEM),
                      pl.BlockSpec((TM * SUBLANES, LANES), lambda c, *_: (c, 0))],
            out_specs=[pl.BlockSpec(memory_space=pltpu.SMEM), pl.BlockSpec(memory_space=pl.ANY)],
            scratch_shapes=[pltpu.VMEM((TMS * SUBLANES, LANES), F32),
                            pltpu.SemaphoreType.DMA(()), pltpu.SemaphoreType.DMA(())],
        ),
        out_shape=[jax.ShapeDtypeStruct((NT_MAX * TMS,), jnp.int32),
                   jax.ShapeDtypeStruct((NT_MAX * TMS * SUBLANES, LANES), F32)],
        compiler_params=pltpu.CompilerParams(dimension_semantics=("arbitrary",), has_side_effects=True),
        name="moe_dispatch",
    )(base, cnt, nused, pos, x_tm)


def _load_token_major(ref, rows):
    return jnp.concatenate([ref[pl.ds(c, rows, stride=SUBLANES), :] for c in range(SUBLANES)], axis=1)


def _store_token_major(ref, value, rows):
    for c in range(SUBLANES):
        ref[pl.ds(c, rows, stride=SUBLANES), :] = value[:, c * LANES:(c + 1) * LANES]


def _to_rows_kernel(x_ref, out_ref):
    out_ref[...] = _load_token_major(x_ref, TM)


def _to_rows_call(x_tm, ntok):
    return pl.pallas_call(
        _to_rows_kernel,
        grid=(ntok // TM,),
        in_specs=[pl.BlockSpec((TM * SUBLANES, LANES), lambda i: (i, 0))],
        out_specs=pl.BlockSpec((TM, D_MODEL), lambda i: (i, 0)),
        out_shape=jax.ShapeDtypeStruct((ntok, D_MODEL), F32),
        compiler_params=pltpu.CompilerParams(dimension_semantics=("arbitrary",)),
        name="token_major_to_rows",
    )(x_tm)


def _moe_kernel(tb_ref, nused_ref, inv_ref, x_ref, n2g_ref, wr_ref, br_ref,
                wg_lo_ref, wu_lo_ref, wd_lo_ref, wg_hi_ref, wu_hi_ref, wd_hi_ref,
                out_hbm, obuf, ssem):
    i = pl.program_id(0)
    nused = nused_ref[0]
    slot = lax.rem(i, 2)

    def scatter_row(r, s):
        tok = inv_ref[0, r]
        return pltpu.make_async_copy(obuf.at[s, pl.ds(r * SUBLANES, SUBLANES)],
                                     out_hbm.at[pl.ds(tok * SUBLANES, SUBLANES)], ssem.at[s])

    def start_rows(row_copy):
        def body(g, carry):
            for u in range(ISSUE_UNROLL):
                row_copy(g * ISSUE_UNROLL + u).start(priority=u % 2)
            return carry
        lax.fori_loop(0, TMS // ISSUE_UNROLL, body, 0)

    def wait_tile(sem, s):
        for _ in range(TMS // WAIT_ROWS):
            pltpu.make_async_copy(obuf.at[s, pl.ds(0, WAIT_ROWS * SUBLANES)],
                                  out_hbm.at[pl.ds(0, WAIT_ROWS * SUBLANES)], sem.at[s]).wait()

    @pl.when(i == 0)
    def _():
        obuf[...] = jnp.zeros_like(obuf)
        for s in range(2):
            junk = pltpu.make_async_copy(
                obuf.at[s], out_hbm.at[pl.ds((NTOK + s * TMS) * SUBLANES, TMS * SUBLANES)], ssem.at[s])
            junk.start()
            junk.wait()

    @pl.when(i < nused)
    def _():
        @pl.when(i >= 2)
        def _():
            wait_tile(ssem, slot)
        x = _load_token_major(x_ref, TMS)
        ms = jnp.mean(x * x, axis=-1, keepdims=True)
        hb = (x * lax.rsqrt(ms + EPS) * n2g_ref[...]).astype(BF16)
        logits = _dot(hb, wr_ref[...]) + br_ref[...]
        g, e_lo, e_hi = _bucket_experts(tb_ref[i])
        lane = lax.broadcasted_iota(jnp.int32, (TMS, LANES), 1)
        pick = lambda l: jnp.sum(jnp.where(lane == l, logits, 0.0), axis=-1, keepdims=True)
        lg, l_lo, l_hi = pick(g), pick(8 + e_lo), pick(8 + e_hi)
        gsum = jnp.sum(jnp.where(lane < N_GROUPS, jnp.exp(logits - lg), 0.0), axis=-1, keepdims=True)
        g_w = 1.0 / gsum
        w_lo = g_w / (1.0 + jnp.exp(l_hi - l_lo))
        w_hi = g_w / (1.0 + jnp.exp(l_lo - l_hi))

        def expert(wg_ref, wu_ref, wd_ref, w):
            a = _dot(hb, wg_ref[...])
            hid = a * _sigmoid(a) * _dot(hb, wu_ref[...]) * w
            return _dot(hid.astype(BF16), wd_ref[...])
        y = (x + expert(wg_lo_ref, wu_lo_ref, wd_lo_ref, w_lo)
             + expert(wg_hi_ref, wu_hi_ref, wd_hi_ref, w_hi))
        _store_token_major(obuf.at[slot], y, TMS)

        start_rows(lambda r: scatter_row(r, slot))

    @pl.when(i == nused - 1)
    def _():
        wait_tile(ssem, slot)

        @pl.when(i >= 1)
        def _():
            wait_tile(ssem, 1 - slot)


def _moe_call(tile_bucket, nused, inv, xs_tm, n2g, wr, br, wg, wu, wd):
    def used(i, nu):
        return jnp.minimum(i, nu[0] - 1)

    def w_spec(shape, which):
        def index(i, tb, nu):
            return (_bucket_experts(tb[used(i, nu)])[which], 0, 0)
        return pl.BlockSpec((None,) + shape, index)

    const = lambda shape: pl.BlockSpec(shape, lambda i, tb, nu: (0, 0))
    up, down = (D_MODEL, D_EXPERT), (D_EXPERT, D_MODEL)
    inv3 = inv.reshape(NT_MAX, 1, TMS)
    return pl.pallas_call(
        _moe_kernel,
        grid_spec=pltpu.PrefetchScalarGridSpec(
            num_scalar_prefetch=2,
            grid=(NT_MAX,),
            in_specs=[
                pl.BlockSpec((None, 1, TMS), lambda i, tb, nu: (i, 0, 0), memory_space=pltpu.SMEM),
                pl.BlockSpec((TMS * SUBLANES, LANES), lambda i, tb, nu: (used(i, nu), 0)),
                const((1, D_MODEL)), const((D_MODEL, LANES)), const((1, LANES)),
                w_spec(up, 1), w_spec(up, 1), w_spec(down, 1),
                w_spec(up, 2), w_spec(up, 2), w_spec(down, 2),
            ],
            out_specs=pl.BlockSpec(memory_space=pl.ANY),
            scratch_shapes=[pltpu.VMEM((2, TMS * SUBLANES, LANES), F32), pltpu.SemaphoreType.DMA((2,))],
        ),
        out_shape=jax.ShapeDtypeStruct(((NTOK + JUNK_ROWS) * SUBLANES, LANES), F32),
        compiler_params=pltpu.CompilerParams(
            dimension_semantics=("arbitrary",), vmem_limit_bytes=VMEM_LIMIT, has_side_effects=True),
        name="hier_moe",
    )(tile_bucket, nused, inv3, xs_tm, n2g, wr, br, wg, wu, wd, wg, wu, wd)


def _rope_tables(positions):
    inv = ROPE_THETA ** (-jnp.arange(0, ROT_DIM, 2, dtype=F32) / ROT_DIM)
    ang = positions.astype(F32)[..., None] * inv
    half = ROT_DIM // 2
    dim = jnp.arange(LANES) % HEAD_DIM
    rotated = dim < ROT_DIM
    expand = ((jnp.arange(half)[:, None] == dim[None, :] % half) & rotated[None, :]).astype(F32)
    spread = lambda t: jnp.dot(t.reshape(-1, half), expand,
                               precision=lax.Precision.HIGHEST).reshape(t.shape[:-1] + (LANES,))
    return spread(jnp.cos(ang)) + (~rotated).astype(F32), spread(jnp.sin(ang))


def kernel(x, positions, norm1_g, w_in, sgu_ln_g, sgu_ln_b, sgu_w, sgu_b, q_norm_g, k_norm_g, sinks, conv3_w, conv31_w, conv31_b, cnorm_g, cnorm_b, w_branch, w_gate, b_gate, w_o, norm2_g, w_group, b_group, w_expert, b_expert, w_e_gate, w_e_up, w_e_down):
    bsz, seq, d = x.shape
    assert (bsz, seq, d) == (BATCH, SEQ, D_MODEL)
    ntok = bsz * seq
    depth = norm1_g.shape[0]
    cos_t, sin_t = _rope_tables(positions)
    row = lambda t: t.reshape(1, -1)
    for l in range(depth):
        sgu_wcat = jnp.transpose(sgu_w[l], (1, 0, 2)).reshape(SGU_CHUNK, SGU_GROUPS * SGU_CHUNK)
        sgu_bias = jnp.repeat(sgu_b[l].T, MIX_W // SGU_GROUPS, axis=1)
        c3w = jnp.zeros((SUBLANES, MIX_W), F32).at[:SHORT_CONV].set(conv3_w[l])
        c31w = jnp.zeros((C31_HALO, MIX_W), F32).at[:CONFORMER_CONV].set(conv31_w[l])
        wr = jnp.zeros((d, LANES), F32).at[:, 0:N_GROUPS].set(w_group[l]).at[:, 8:8 + N_EXPERTS].set(w_expert[l])
        wr = wr.astype(BF16)
        br = jnp.zeros((1, LANES), F32).at[0, 0:N_GROUPS].set(b_group[l]).at[0, 8:8 + N_EXPERTS].set(b_expert[l])
        x, bucket = _mixing_call(
            x, cos_t, sin_t, row(norm1_g[l]), w_in[l].astype(BF16),
            row(sgu_ln_g[l]), row(sgu_ln_b[l]), sgu_wcat, sgu_bias,
            row(jnp.tile(q_norm_g[l], N_Q_HEADS)), row(jnp.tile(k_norm_g[l], N_KV_HEADS)), sinks[l],
            c3w, c31w, row(conv31_b[l]), row(cnorm_g[l]), row(cnorm_b[l]),
            w_branch[l].astype(BF16), (0.5 * w_gate[l]).astype(BF16), row(0.5 * b_gate[l]), w_o[l].astype(BF16),
            row(norm2_g[l]), wr, br)
        pos2, tile_bucket, meta = _sort_call(bucket)
        nused = meta[0, 2:3]
        inv, xs = _dispatch_call(meta[:, 0], meta[:, 1], nused, pos2[1], x)
        x = _moe_call(tile_bucket[0], nused, inv, xs, row(norm2_g[l]), wr, br,
                      w_e_gate[l].astype(BF16), w_e_up[l].astype(BF16), w_e_down[l].astype(BF16))
    return _to_rows_call(x, ntok).reshape(bsz, seq, d)
```

```python
import functools

import jax
import jax.numpy as jnp
from jax import lax
from jax.experimental import pallas as pl
from jax.experimental.pallas import tpu as pltpu

D_MODEL = 1024
N_BRANCH = 4
MIX_W = 256
SGU_GROUPS = 4
SGU_CHUNK = 128
N_Q_HEADS = 4
N_KV_HEADS = 2
HEAD_DIM = 64
WINDOW = 128
ROT_DIM = HEAD_DIM // 4
ROPE_THETA = 500000.0
SHORT_CONV = 3
CONFORMER_CONV = 31
N_GROUPS = 4
EXPERTS_PER_GROUP = 4
N_EXPERTS = N_GROUPS * EXPERTS_PER_GROUP
D_EXPERT = 256
EPS = 1e-6
D_IN = 2304

BATCH = 8
SEQ = 4096
PAIRS_PER_GROUP = EXPERTS_PER_GROUP * (EXPERTS_PER_GROUP - 1) // 2
N_BUCKETS = N_GROUPS * PAIRS_PER_GROUP
NB_PAD = 32

LANES = 128
SUBLANES = 8
TM = 512
TMS = 256
NTOK = BATCH * SEQ
N_TILES = NTOK // TM
TILES_PER_SEQ = SEQ // TM
NT_MAX = NTOK // TMS + N_BUCKETS
JUNK_ROWS = 2 * TMS
TB_LEN = 256
N_MIX_IN = 24
NEW_TILES_MAX = TM // TMS + 1
SORT_SUB = 4
ISSUE_UNROLL = 8
WAIT_ROWS = 128
C3_HALO = SUBLANES
C31_HALO = 32
CONV_ROWS = 64
VMEM_LIMIT = 56 * 1024 * 1024
NEG = -1e30

F32 = jnp.float32
BF16 = jnp.bfloat16


def _dot(a, b):
    return jnp.dot(a, b, preferred_element_type=F32)


def _gelu_tanh(x):
    c = 0.7978845608028654
    return 0.5 * x * (1.0 + jnp.tanh(c * (x + 0.044715 * (x * x * x))))


def _sigmoid(x):
    return 0.5 * jnp.tanh(0.5 * x) + 0.5


def _layernorm(x, g, b):
    mu = jnp.mean(x, axis=-1, keepdims=True)
    xc = x - mu
    var = jnp.mean(xc * xc, axis=-1, keepdims=True)
    return xc * lax.rsqrt(var + EPS) * g + b


def _head_meansq(t, width):
    r = lax.broadcasted_iota(jnp.int32, (width, width), 0) // HEAD_DIM
    c = lax.broadcasted_iota(jnp.int32, (width, width), 1) // HEAD_DIM
    bd = jnp.where(r == c, 1.0 / HEAD_DIM, 0.0).astype(BF16)
    t2 = t * t
    hi = t2.astype(BF16)
    lo = (t2 - hi.astype(F32)).astype(BF16)
    return _dot(hi, bd) + _dot(lo, bd)


def _rope(t, c, s1, s2):
    w = t.shape[-1]
    half = ROT_DIM // 2
    return t * c + pltpu.roll(t, w - half, axis=1) * s1 + pltpu.roll(t, half, axis=1) * s2


def _mixing_step(token_major_in, j, x_ref, cos_ref, sin_ref, n1g_ref, win_ref,
                 sgu_g_ref, sgu_b_ref, sgu_w_ref, sgu_bias_ref,
                 qg_ref, kg_ref, sinks_ref,
                 c3w_ref, c31w_ref, c31b_ref, cng_ref, cnb_ref,
                 wbr_ref, wgate_ref, bgate_ref, wo_ref,
                 n2g_ref, wr_ref, br_ref,
                 out_ref, bucket_ref,
                 kprev_ref, vprev_ref, c3buf_ref, c31buf_ref,
                 ys_ref, hb_ref, xk_ref, ys_in, hb_in, xk_in):
    x = _load_token_major(x_ref, TM) if token_major_in else x_ref[...]
    xk_ref[...] = x
    ms = jnp.mean(x * x, axis=-1, keepdims=True)
    hb = (x * lax.rsqrt(ms + EPS) * n1g_ref[...]).astype(BF16)
    hb_ref[...] = hb

    uv = _dot(hb, win_ref[:, 0:512])
    u = _gelu_tanh(uv[:, 0:MIX_W])
    v = _layernorm(_gelu_tanh(uv[:, MIX_W:2 * MIX_W]), sgu_g_ref[...], sgu_b_ref[...])
    tt = lax.broadcasted_iota(jnp.int32, (SGU_CHUNK, SGU_GROUPS * SGU_CHUNK), 0)
    ss = lax.broadcasted_iota(jnp.int32, (SGU_CHUNK, SGU_GROUPS * SGU_CHUNK), 1) % SGU_CHUNK
    wcat = jnp.where(ss <= tt, sgu_w_ref[...], 0.0).astype(BF16)
    lane_grp = lax.broadcasted_iota(jnp.int32, (SGU_CHUNK, MIX_W), 1) // (MIX_W // SGU_GROUPS)
    for c in range(TM // SGU_CHUNK):
        rows = slice(c * SGU_CHUNK, (c + 1) * SGU_CHUNK)
        vc = v[rows]
        vstack = jnp.concatenate(
            [jnp.where(lane_grp == g, vc, 0.0) for g in range(SGU_GROUPS)], axis=0).astype(BF16)
        z = _dot(wcat, vstack) + sgu_bias_ref[...]
        ys_ref[0, rows, :] = (u[rows] * z).astype(BF16)

    qkv = _dot(hb, win_ref[:, 512:1024])
    rc, sin_t = cos_ref[...], sin_ref[...]
    head_dim_idx = lax.broadcasted_iota(jnp.int32, (1, LANES), 1) % HEAD_DIM
    rs1 = jnp.where(head_dim_idx < ROT_DIM // 2, -sin_t, 0.0)
    rs2 = jnp.where((head_dim_idx >= ROT_DIM // 2) & (head_dim_idx < ROT_DIM), sin_t, 0.0)
    q = qkv[:, 0:256]
    q = q * lax.rsqrt(_head_meansq(q, 256) + EPS) * qg_ref[...]
    q = _rope(q, jnp.concatenate([rc, rc], axis=1), jnp.concatenate([rs1, rs1], axis=1),
              jnp.concatenate([rs2, rs2], axis=1)) * (HEAD_DIM ** -0.5)
    k = qkv[:, 256:384]
    k = k * lax.rsqrt(_head_meansq(k, 128) + EPS) * kg_ref[...]
    k = _rope(k, rc, rs1, rs2)
    vv = qkv[:, 384:512]
    kfull = jnp.concatenate([kprev_ref[...], k], axis=0)
    vfull = jnp.concatenate([vprev_ref[...], vv], axis=0)
    kprev_ref[...] = k[TM - WINDOW:TM]
    vprev_ref[...] = vv[TM - WINDOW:TM]
    low_full = lax.broadcasted_iota(jnp.int32, (WINDOW + TM, LANES), 1) < HEAD_DIM
    krot = pltpu.roll(kfull, HEAD_DIM, axis=1)
    vrot = pltpu.roll(vfull, HEAD_DIM, axis=1)
    kdup = [jnp.where(low_full, kfull, krot).astype(BF16), jnp.where(low_full, krot, kfull).astype(BF16)]
    vdup = [jnp.where(low_full, vfull, vrot).astype(BF16), jnp.where(low_full, vrot, vfull).astype(BF16)]
    low = lax.broadcasted_iota(jnp.int32, (WINDOW, LANES), 1) < HEAD_DIM
    row2 = lax.broadcasted_iota(jnp.int32, (2 * WINDOW, 2 * WINDOW), 0)
    qi = row2 % WINDOW
    ki = lax.broadcasted_iota(jnp.int32, (2 * WINDOW, 2 * WINDOW), 1)
    in_prev = (ki < WINDOW) & (ki > qi)
    in_cur = (ki >= WINDOW) & (ki - WINDOW <= qi)
    is_g0 = lax.broadcasted_iota(jnp.int32, (2 * WINDOW, 1), 0) < WINDOW
    for n in range(TM // WINDOW):
        if n == 0:
            valid = (in_prev & (j > 0)) | in_cur
        else:
            valid = in_prev | in_cur
        cols = []
        for h in range(N_KV_HEADS):
            qcol = q[n * WINDOW:(n + 1) * WINDOW, h * LANES:(h + 1) * LANES]
            qs = jnp.concatenate([jnp.where(low, qcol, 0.0), jnp.where(low, 0.0, qcol)], axis=0).astype(BF16)
            kk = kdup[h][n * WINDOW:(n + 2) * WINDOW]
            sc = lax.dot_general(qs, kk, (((1,), (1,)), ((), ())), preferred_element_type=F32)
            sc = jnp.where(valid, sc, NEG)
            sink = jnp.where(is_g0, sinks_ref[2 * h], sinks_ref[2 * h + 1])
            m = jnp.maximum(jnp.max(sc, axis=-1, keepdims=True), sink)
            p = jnp.exp(sc - m)
            denom = jnp.sum(p, axis=-1, keepdims=True) + jnp.exp(sink - m)
            o = _dot(p.astype(BF16), vdup[h][n * WINDOW:(n + 2) * WINDOW]) / denom
            cols.append(jnp.where(low, o[0:WINDOW], o[WINDOW:2 * WINDOW]))
        ys_ref[1, n * WINDOW:(n + 1) * WINDOW, :] = jnp.concatenate(cols, axis=1).astype(BF16)

    cc = _dot(hb, win_ref[:, 1024:1792])
    c3buf_ref[C3_HALO:C3_HALO + TM, :] = cc[:, 0:256] * cc[:, 512:768]
    conv = c3w_ref[0:1, :] * c3buf_ref[C3_HALO - 2:C3_HALO - 2 + TM, :]
    conv = conv + c3w_ref[1:2, :] * c3buf_ref[C3_HALO - 1:C3_HALO - 1 + TM, :]
    conv = conv + c3w_ref[2:3, :] * c3buf_ref[C3_HALO:C3_HALO + TM, :]
    ys_ref[2] = (cc[:, 256:512] * conv).astype(BF16)
    c3buf_ref[0:C3_HALO, :] = c3buf_ref[TM:TM + C3_HALO, :]

    dd = _dot(hb, win_ref[:, 1792:2304])
    c31buf_ref[C31_HALO:C31_HALO + TM, :] = dd[:, 0:256] * _sigmoid(dd[:, 256:512])
    base = C31_HALO - (CONFORMER_CONV - 1)
    for r in range(TM // CONV_ROWS):
        acc = jnp.broadcast_to(c31b_ref[...], (CONV_ROWS, MIX_W))
        for sub in range(SUBLANES):
            taps = [t for t in range(CONFORMER_CONV) if (base + t) % SUBLANES == sub]
            last = (base + taps[-1]) // SUBLANES * SUBLANES
            start = r * CONV_ROWS + sub
            window = c31buf_ref[start:start + last + CONV_ROWS, :]
            part = None
            for tap in taps:
                off = (base + tap) // SUBLANES * SUBLANES
                term = c31w_ref[tap:tap + 1, :] * window[off:off + CONV_ROWS]
                part = term if part is None else part + term
            acc = acc + part
        yn = _layernorm(acc, cng_ref[...], cnb_ref[...])
        ys_ref[3, r * CONV_ROWS:(r + 1) * CONV_ROWS, :] = (yn * _sigmoid(yn)).astype(BF16)

    c31buf_ref[0:C31_HALO, :] = c31buf_ref[TM:TM + C31_HALO, :]

    hb_prev = hb_in[...]
    merged = jnp.zeros((TM, D_MODEL), F32)
    for b in range(N_BRANCH):
        yb = _dot(ys_in[b], wbr_ref[b])
        half_gate = (_dot(hb_prev, wgate_ref[:, b * D_MODEL:(b + 1) * D_MODEL])
                     + bgate_ref[:, b * D_MODEL:(b + 1) * D_MODEL])
        merged = merged + (jnp.tanh(half_gate) * yb + yb)
    xo = xk_in[...] + _dot((0.5 * merged).astype(BF16), wo_ref[...])
    _store_token_major(out_ref, xo, TM)

    ms2 = jnp.mean(xo * xo, axis=-1, keepdims=True)
    h2 = (xo * lax.rsqrt(ms2 + EPS) * n2g_ref[...]).astype(BF16)
    logits = _dot(h2, wr_ref[...]) + br_ref[...]
    bucket_ref[...] = _route_bucket(logits.T)


def _mixing_kernel(token_major_in, *refs):
    ins = refs[:N_MIX_IN]
    inv_ref, xs_hbm, tb_ref, meta_ref = refs[N_MIX_IN:N_MIX_IN + 4]
    (kprev_ref, vprev_ref, c3buf_ref, c31buf_ref, ys_a, ys_b, hb_a, hb_b, xk_a, xk_b, ob_a, ob_b,
     bkt_ref, cnt_ref, cur_ref, gal_ref, tbacc_ref, upper_ref, slotv_ref, slot_smem, statev_ref, state_smem,
     ztile_ref, sem, zsem) = refs[N_MIX_IN + 4:]
    carries = (kprev_ref, vprev_ref, c3buf_ref, c31buf_ref)
    s = pl.program_id(0)
    tile = jnp.minimum(s, N_TILES - 1)
    j = lax.rem(tile, TILES_PER_SEQ)

    @pl.when((j == 0) & (s < N_TILES))
    def _():
        kprev_ref[...] = jnp.zeros_like(kprev_ref)
        vprev_ref[...] = jnp.zeros_like(vprev_ref)
        c3buf_ref[0:C3_HALO, :] = jnp.zeros((C3_HALO, MIX_W), F32)
        c31buf_ref[0:C31_HALO, :] = jnp.zeros((C31_HALO, MIX_W), F32)

    @pl.when(s == 0)
    def _():
        ys_b[...] = jnp.zeros_like(ys_b)
        hb_b[...] = jnp.zeros_like(hb_b)
        xk_b[...] = jnp.zeros_like(xk_b)
        for ref in (cnt_ref, cur_ref, gal_ref, tbacc_ref, ztile_ref):
            ref[...] = jnp.zeros_like(ref)
        r_i = lax.broadcasted_iota(jnp.int32, (TM, TM), 0)
        c_i = lax.broadcasted_iota(jnp.int32, (TM, TM), 1)
        upper_ref[...] = jnp.where(r_i <= c_i, 1.0, 0.0).astype(BF16)

    @pl.when(lax.rem(s, 2) == 0)
    def _():
        _mixing_step(token_major_in, j, *ins, ob_a, bkt_ref, *carries, ys_a, hb_a, xk_a, ys_b, hb_b, xk_b)

    @pl.when(lax.rem(s, 2) == 1)
    def _():
        _mixing_step(token_major_in, j, *ins, ob_b, bkt_ref, *carries, ys_b, hb_b, xk_b, ys_a, hb_a, xk_a)

    def wait_rows():
        for _ in range(TM // WAIT_ROWS):
            pltpu.make_async_copy(ob_a.at[pl.ds(0, WAIT_ROWS * SUBLANES)],
                                  xs_hbm.at[pl.ds(0, WAIT_ROWS * SUBLANES)], sem).wait()

    @pl.when(s >= 1)
    def _():
        slotv_ref[...] = _allocate(bkt_ref[...], cnt_ref, cur_ref, gal_ref, tbacc_ref, upper_ref)
        to_smem = pltpu.make_async_copy(slotv_ref, slot_smem, zsem)
        to_smem.start()
        to_smem.wait()

    @pl.when(s >= 2)
    def _():
        wait_rows()

    for parity, ob_ref in ((0, ob_a), (1, ob_b)):
        @pl.when((s >= 1) & (lax.rem(s, 2) == parity))
        def _():
            def place(g, carry):
                for u in range(ISSUE_UNROLL):
                    i = g * ISSUE_UNROLL + u
                    slot = slot_smem[0, i]
                    inv_ref[slot] = (s - 1) * TM + i
                    pltpu.make_async_copy(ob_ref.at[pl.ds(i * SUBLANES, SUBLANES)],
                                          xs_hbm.at[pl.ds(slot * SUBLANES, SUBLANES)], sem).start(priority=u % 2)
                return carry
            lax.fori_loop(0, TM // ISSUE_UNROLL, place, 0)

    @pl.when(s == N_TILES)
    def _():
        wait_rows()
        _finish_sorted_layout(cnt_ref, cur_ref, gal_ref, tbacc_ref, statev_ref, state_smem, ztile_ref,
                              inv_ref, xs_hbm, tb_ref, meta_ref, zsem)


def _allocate(bucket, cnt_ref, cur_ref, gal_ref, tbacc_ref, upper_ref):
    rows = lax.broadcasted_iota(jnp.int32, (NB_PAD, TM), 0)
    onehot = jnp.where(rows == bucket, 1.0, 0.0)
    prefix = _dot(onehot.astype(BF16), upper_ref[...])
    total = prefix[:, TM - 1:TM]
    cnt0, cur0, gal0 = cnt_ref[...], cur_ref[...], gal_ref[...]
    tiles_before = jnp.floor((cnt0 + (TMS - 1)) * (1.0 / TMS))
    tiles_after = jnp.floor((cnt0 + total + (TMS - 1)) * (1.0 / TMS))
    n_new = tiles_after - tiles_before
    rr = lax.broadcasted_iota(jnp.int32, (NB_PAD, NB_PAD), 0)
    cc = lax.broadcasted_iota(jnp.int32, (NB_PAD, NB_PAD), 1)
    strict_lower = jnp.where(cc < rr, 1.0, 0.0).astype(BF16)
    n_new_b = jnp.broadcast_to(n_new, (NB_PAD, LANES))
    first_new = gal0 + _dot(strict_lower, n_new_b.astype(BF16))[:, 0:1]
    rank = cnt0 + prefix - 1.0
    ordinal = jnp.floor(rank * (1.0 / TMS))
    fresh = ordinal - tiles_before
    tile_id = jnp.where(fresh < 0, cur0, first_new + fresh)
    slot = tile_id * TMS + (rank - ordinal * TMS)
    cnt_ref[...] = cnt0 + total
    cur_ref[...] = jnp.where(n_new > 0, first_new + n_new - 1.0, cur0)
    gal_ref[...] = gal0 + jnp.sum(n_new_b, axis=0, keepdims=True)[:, 0:1]
    lane = lax.broadcasted_iota(jnp.int32, (NB_PAD, TB_LEN), 1).astype(F32)
    bucket_id = lax.broadcasted_iota(jnp.int32, (NB_PAD, TB_LEN), 0).astype(F32)
    owner = jnp.zeros((NB_PAD, TB_LEN), F32)
    for extra in range(NEW_TILES_MAX):
        owner = owner + jnp.where((lane == first_new + extra) & (n_new > extra), bucket_id, 0.0)
    tbacc_ref[...] += jnp.sum(owner, axis=0, keepdims=True)
    return jnp.sum(onehot * slot, axis=0, keepdims=True).astype(jnp.int32)


def _finish_sorted_layout(cnt_ref, cur_ref, gal_ref, tbacc_ref, statev_ref, state_smem, ztile_ref,
                          inv_ref, xs_hbm, tb_ref, meta_ref, zsem):
    col = lax.broadcasted_iota(jnp.int32, (NB_PAD, LANES), 1)
    state = jnp.where(col == 0, cnt_ref[...], jnp.where(col == 1, cur_ref[...], gal_ref[...]))
    statev_ref[...] = state.astype(jnp.int32)
    meta_ref[...] = jnp.broadcast_to(gal_ref[...], (NB_PAD, LANES)).astype(jnp.int32)
    tb_ref[...] = tbacc_ref[...].astype(jnp.int32)
    to_smem = pltpu.make_async_copy(statev_ref, state_smem, zsem)
    to_smem.start()
    to_smem.wait()
    nused = state_smem[0, 2]

    def fill(slot, carry):
        inv_ref[slot] = NTOK + (slot & (JUNK_ROWS - 1))
        return carry

    def zero_rows(first_slot, n_rows):
        return pltpu.make_async_copy(ztile_ref.at[pl.ds(0, n_rows * SUBLANES)],
                                     xs_hbm.at[pl.ds(first_slot * SUBLANES, n_rows * SUBLANES)], zsem)

    for wait in (False, True):
        for k in range(N_BUCKETS):
            used_rows = state_smem[k, 0] & (TMS - 1)
            n_pad = jnp.where(used_rows > 0, TMS - used_rows, 0)
            slot = state_smem[k, 1] * TMS + (TMS - n_pad)
            if not wait:
                lax.fori_loop(slot, slot + n_pad, fill, 0)
            for bit in reversed(range(TMS.bit_length() - 1)):
                piece = n_pad & (1 << bit)

                @pl.when(piece != 0)
                def _():
                    copy = zero_rows(slot, 1 << bit)
                    copy.wait() if wait else copy.start()
                slot = slot + piece

        def unused(t, carry):
            copy = zero_rows(t * TMS, TMS)
            copy.wait() if wait else copy.start()
            return carry
        lax.fori_loop(nused, NT_MAX, unused, 0)
    lax.fori_loop(nused * TMS, NT_MAX * TMS, fill, 0)


def _const_spec(shape):
    zeros = (0,) * len(shape)
    return pl.BlockSpec(shape, lambda s: zeros, pipeline_mode=pl.Buffered(1))


def _mixing_call(x, cos_t, sin_t, n1g, w_in, sgu_g, sgu_b, sgu_w, sgu_bias, qg, kg, sinks,
                 c3w, c31w, c31b, cng, cnb, wbr, wgate, bgate, wo, n2g, wr, br):
    cur = lambda s: jnp.minimum(s, N_TILES - 1)
    tok = lambda width: pl.BlockSpec(
        (None, TM, width), lambda s: (cur(s) // TILES_PER_SEQ, lax.rem(cur(s), TILES_PER_SEQ), 0))
    token_major_in = x.ndim == 2
    in_specs = [
        pl.BlockSpec((TM * SUBLANES, LANES), lambda s: (cur(s), 0)) if token_major_in else tok(D_MODEL),
        tok(LANES), tok(LANES),
        _const_spec((1, D_MODEL)), _const_spec((D_MODEL, D_IN)),
        _const_spec((1, MIX_W)), _const_spec((1, MIX_W)),
        _const_spec((SGU_CHUNK, SGU_GROUPS * SGU_CHUNK)), _const_spec((SGU_CHUNK, MIX_W)),
        _const_spec((1, N_Q_HEADS * HEAD_DIM)), _const_spec((1, N_KV_HEADS * HEAD_DIM)),
        pl.BlockSpec(memory_space=pltpu.SMEM),
        _const_spec((SUBLANES, MIX_W)), _const_spec((C31_HALO, MIX_W)), _const_spec((1, MIX_W)),
        _const_spec((1, MIX_W)), _const_spec((1, MIX_W)),
        _const_spec((N_BRANCH, MIX_W, D_MODEL)), _const_spec((D_MODEL, N_BRANCH * D_MODEL)),
        _const_spec((1, N_BRANCH * D_MODEL)), _const_spec((D_MODEL, D_MODEL)),
        _const_spec((1, D_MODEL)), _const_spec((D_MODEL, LANES)), _const_spec((1, LANES)),
    ]
    return pl.pallas_call(
        functools.partial(_mixing_kernel, token_major_in),
        grid=(N_TILES + 1,),
        in_specs=in_specs,
        out_specs=[pl.BlockSpec(memory_space=pltpu.SMEM), pl.BlockSpec(memory_space=pl.ANY),
                   pl.BlockSpec((1, TB_LEN), lambda s: (0, 0)), pl.BlockSpec((NB_PAD, LANES), lambda s: (0, 0))],
        out_shape=[jax.ShapeDtypeStruct((NT_MAX * TMS,), jnp.int32),
                   jax.ShapeDtypeStruct((NT_MAX * TMS * SUBLANES, LANES), F32),
                   jax.ShapeDtypeStruct((1, TB_LEN), jnp.int32),
                   jax.ShapeDtypeStruct((NB_PAD, LANES), jnp.int32)],
        scratch_shapes=[
            pltpu.VMEM((WINDOW, LANES), F32), pltpu.VMEM((WINDOW, LANES), F32),
            pltpu.VMEM((C3_HALO + TM, MIX_W), F32), pltpu.VMEM((C31_HALO + TM, MIX_W), F32),
            pltpu.VMEM((N_BRANCH, TM, MIX_W), BF16), pltpu.VMEM((N_BRANCH, TM, MIX_W), BF16),
            pltpu.VMEM((TM, D_MODEL), BF16), pltpu.VMEM((TM, D_MODEL), BF16),
            pltpu.VMEM((TM, D_MODEL), F32), pltpu.VMEM((TM, D_MODEL), F32),
            pltpu.VMEM((TM * SUBLANES, LANES), F32), pltpu.VMEM((TM * SUBLANES, LANES), F32),
            pltpu.VMEM((1, TM), jnp.int32),
            pltpu.VMEM((NB_PAD, 1), F32), pltpu.VMEM((NB_PAD, 1), F32), pltpu.VMEM((NB_PAD, 1), F32),
            pltpu.VMEM((1, TB_LEN), F32), pltpu.VMEM((TM, TM), BF16),
            pltpu.VMEM((1, TM), jnp.int32), pltpu.SMEM((1, TM), jnp.int32),
            pltpu.VMEM((NB_PAD, LANES), jnp.int32), pltpu.SMEM((NB_PAD, LANES), jnp.int32),
            pltpu.VMEM((TMS * SUBLANES, LANES), F32),
            pltpu.SemaphoreType.DMA(()), pltpu.SemaphoreType.DMA(()),
        ],
        compiler_params=pltpu.CompilerParams(
            dimension_semantics=("arbitrary",), vmem_limit_bytes=VMEM_LIMIT, has_side_effects=True),
        name="mixing_block",
    )(x, cos_t, sin_t, n1g, w_in, sgu_g, sgu_b, sgu_w, sgu_bias, qg, kg, sinks,
      c3w, c31w, c31b, cng, cnb, wbr, wgate, bgate, wo, n2g, wr, br)


def _route_bucket(logits_t):
    g = [logits_t[i:i + 1, :] for i in range(N_GROUPS)]
    gmax, gidx = g[0], jnp.zeros(g[0].shape, jnp.int32)
    for i in range(1, N_GROUPS):
        better = g[i] > gmax
        gmax = jnp.where(better, g[i], gmax)
        gidx = jnp.where(better, i, gidx)
    e = []
    for i in range(EXPERTS_PER_GROUP):
        ei = logits_t[8 + i:9 + i, :]
        for grp in range(1, N_GROUPS):
            row = 8 + grp * EXPERTS_PER_GROUP + i
            ei = jnp.where(gidx == grp, logits_t[row:row + 1, :], ei)
        e.append(ei)
    v1, i1 = e[0], jnp.zeros_like(gidx)
    for i in range(1, EXPERTS_PER_GROUP):
        better = e[i] > v1
        v1 = jnp.where(better, e[i], v1)
        i1 = jnp.where(better, i, i1)
    v2, i2 = jnp.full_like(v1, -jnp.inf), jnp.zeros_like(gidx)
    for i in range(EXPERTS_PER_GROUP):
        better = (e[i] > v2) & (i1 != i)
        v2 = jnp.where(better, e[i], v2)
        i2 = jnp.where(better, i, i2)
    lo, hi = jnp.minimum(i1, i2), jnp.maximum(i1, i2)
    pair = jnp.where(lo == 0, hi - 1, jnp.where(lo == 1, hi + 1, 5))
    return gidx * PAIRS_PER_GROUP + pair


def _bucket_experts(b):
    g = b // PAIRS_PER_GROUP
    pair = b - g * PAIRS_PER_GROUP
    lo = jnp.where(pair < 3, 0, jnp.where(pair < 5, 1, 2))
    hi = jnp.where(pair < 3, pair + 1, jnp.where(pair < 5, pair - 1, 3))
    return g, g * EXPERTS_PER_GROUP + lo, g * EXPERTS_PER_GROUP + hi


def _sort_kernel(bucket_ref, pos_ref, tb_ref, meta_ref, carry_ref, base_ref, upper_ref):
    p = pl.program_id(0)
    j = pl.program_id(1)
    rows = lax.broadcasted_iota(jnp.int32, (NB_PAD, TM), 0)

    def one_hot(q):
        return jnp.where(rows == bucket_ref[q], 1.0, 0.0)

    @pl.when((p == 0) & (j == 0))
    def _():
        carry_ref[...] = jnp.zeros_like(carry_ref)
        r_i = lax.broadcasted_iota(jnp.int32, (TM, TM), 0)
        c_i = lax.broadcasted_iota(jnp.int32, (TM, TM), 1)
        upper_ref[...] = jnp.where(r_i <= c_i, 1.0, 0.0).astype(BF16)

    @pl.when(p == 0)
    def _():
        for q in range(SORT_SUB):
            carry_ref[...] += jnp.sum(one_hot(q), axis=1, keepdims=True)
        pos_ref[...] = jnp.zeros_like(pos_ref)

    @pl.when((p == 1) & (j == 0))
    def _():
        cnt = carry_ref[...]
        ntile = jnp.floor((cnt + (TMS - 1)) * (1.0 / TMS))
        rr = lax.broadcasted_iota(jnp.int32, (NB_PAD, NB_PAD), 0)
        cc = lax.broadcasted_iota(jnp.int32, (NB_PAD, NB_PAD), 1)
        strict_lower = jnp.where(cc < rr, 1.0, 0.0).astype(BF16)
        ntile_b = jnp.broadcast_to(ntile, (NB_PAD, LANES))
        tstart = _dot(strict_lower, ntile_b.astype(BF16))
        base_ref[...] = tstart[:, 0:1] * TMS
        carry_ref[...] = jnp.zeros_like(carry_ref)
        nused = jnp.broadcast_to(jnp.sum(ntile_b, axis=0, keepdims=True), (NB_PAD, LANES))
        col = lax.broadcasted_iota(jnp.int32, (NB_PAD, LANES), 1)
        meta = jnp.where(col == 0, tstart * TMS, jnp.where(col == 1, jnp.broadcast_to(cnt, (NB_PAD, LANES)), nused))
        meta_ref[...] = meta.astype(jnp.int32)
        tile = lax.broadcasted_iota(jnp.int32, (NB_PAD, TB_LEN), 1).astype(F32)
        row_ok = lax.broadcasted_iota(jnp.int32, (NB_PAD, TB_LEN), 0) < N_BUCKETS
        started = jnp.where(row_ok & (tstart[:, 0:1] <= tile), 1.0, 0.0)
        tb_ref[...] = (jnp.sum(started, axis=0, keepdims=True) - 1.0).astype(jnp.int32)

    @pl.when(p == 1)
    def _():
        for q in range(SORT_SUB):
            onehot = one_hot(q)
            prefix = _dot(onehot.astype(BF16), upper_ref[...])
            slot = base_ref[...] + carry_ref[...] + prefix - 1.0
            pos_ref[q] = jnp.sum(onehot * slot, axis=0, keepdims=True).astype(jnp.int32)
            carry_ref[...] += prefix[:, TM - 1:TM]


def _sort_call(bucket):
    return pl.pallas_call(
        _sort_kernel,
        grid=(2, bucket.shape[0] // SORT_SUB),
        in_specs=[pl.BlockSpec((SORT_SUB, 1, TM), lambda p, j: (j, 0, 0))],
        out_specs=[
            pl.BlockSpec((None, SORT_SUB, 1, TM), lambda p, j: (p, j, 0, 0)),
            pl.BlockSpec((1, TB_LEN), lambda p, j: (0, 0)),
            pl.BlockSpec((NB_PAD, LANES), lambda p, j: (0, 0)),
        ],
        out_shape=[
            jax.ShapeDtypeStruct((2,) + bucket.shape, jnp.int32),
            jax.ShapeDtypeStruct((1, TB_LEN), jnp.int32),
            jax.ShapeDtypeStruct((NB_PAD, LANES), jnp.int32),
        ],
        scratch_shapes=[pltpu.VMEM((NB_PAD, 1), F32), pltpu.VMEM((NB_PAD, 1), F32),
                        pltpu.VMEM((TM, TM), BF16)],
        compiler_params=pltpu.CompilerParams(dimension_semantics=("arbitrary", "arbitrary")),
        name="bucket_sort",
    )(bucket)


def _dispatch_kernel(base_ref, cnt_ref, nused_ref, pos_ref, x_ref, inv_ref, xs_hbm, ztile, sem, zsem):
    c = pl.program_id(0)

    def fill(s, carry):
        inv_ref[s] = NTOK + (s & (JUNK_ROWS - 1))
        return carry

    def zero_tile(t):
        return pltpu.make_async_copy(ztile, xs_hbm.at[pl.ds(t * (TMS * SUBLANES), TMS * SUBLANES)], zsem)

    @pl.when(c == 0)
    def _():
        ztile[...] = jnp.zeros_like(ztile)
        for wait in (False, True):
            for k in range(N_BUCKETS):
                n_tiles = (cnt_ref[k] + (TMS - 1)) // TMS

                @pl.when(n_tiles > 0)
                def _():
                    copy = zero_tile(base_ref[k] // TMS + n_tiles - 1)
                    copy.wait() if wait else copy.start()

            def unused(t, carry):
                copy = zero_tile(t)
                copy.wait() if wait else copy.start()
                return carry
            lax.fori_loop(nused_ref[0], NT_MAX, unused, 0)
        for k in range(N_BUCKETS):
            n_slots = ((cnt_ref[k] + (TMS - 1)) // TMS) * TMS
            lax.fori_loop(base_ref[k] + cnt_ref[k], base_ref[k] + n_slots, fill, 0)
        lax.fori_loop(nused_ref[0] * TMS, NT_MAX * TMS, fill, 0)

    def place(g, carry):
        for u in range(ISSUE_UNROLL):
            i = g * ISSUE_UNROLL + u
            slot = pos_ref[0, i]
            inv_ref[slot] = c * TM + i
            pltpu.make_async_copy(x_ref.at[pl.ds(i * SUBLANES, SUBLANES)],
                                  xs_hbm.at[pl.ds(slot * SUBLANES, SUBLANES)], sem).start(priority=u % 2)
        return carry
    lax.fori_loop(0, TM // ISSUE_UNROLL, place, 0)
    for _ in range(TM // WAIT_ROWS):
        pltpu.make_async_copy(x_ref.at[pl.ds(0, WAIT_ROWS * SUBLANES)],
                              xs_hbm.at[pl.ds(0, WAIT_ROWS * SUBLANES)], sem).wait()


def _dispatch_call(base, cnt, nused, pos, x_tm):
    return pl.pallas_call(
        _dispatch_kernel,
        grid_spec=pltpu.PrefetchScalarGridSpec(
            num_scalar_prefetch=3,
            grid=(pos.shape[0],),
            in_specs=[pl.BlockSpec((None, 1, TM), lambda c, *_: (c, 0, 0), memory_space=pltpu.SMEM),
                      pl.BlockSpec((TM * SUBLANES, LANES), lambda c, *_: (c, 0))],
            out_specs=[pl.BlockSpec(memory_space=pltpu.SMEM), pl.BlockSpec(memory_space=pl.ANY)],
            scratch_shapes=[pltpu.VMEM((TMS * SUBLANES, LANES), F32),
                            pltpu.SemaphoreType.DMA(()), pltpu.SemaphoreType.DMA(())],
        ),
        out_shape=[jax.ShapeDtypeStruct((NT_MAX * TMS,), jnp.int32),
                   jax.ShapeDtypeStruct((NT_MAX * TMS * SUBLANES, LANES), F32)],
        compiler_params=pltpu.CompilerParams(dimension_semantics=("arbitrary",), has_side_effects=True),
        name="moe_dispatch",
    )(base, cnt, nused, pos, x_tm)


def _load_token_major(ref, rows):
    return jnp.concatenate([ref[pl.ds(c, rows, stride=SUBLANES), :] for c in range(SUBLANES)], axis=1)


def _store_token_major(ref, value, rows):
    for c in range(SUBLANES):
        ref[pl.ds(c, rows, stride=SUBLANES), :] = value[:, c * LANES:(c + 1) * LANES]


def _to_rows_kernel(x_ref, out_ref):
    out_ref[...] = _load_token_major(x_ref, TM)


def _to_rows_call(x_tm, ntok):
    return pl.pallas_call(
        _to_rows_kernel,
        grid=(ntok // TM,),
        in_specs=[pl.BlockSpec((TM * SUBLANES, LANES), lambda i: (i, 0))],
        out_specs=pl.BlockSpec((TM, D_MODEL), lambda i: (i, 0)),
        out_shape=jax.ShapeDtypeStruct((ntok, D_MODEL), F32),
        compiler_params=pltpu.CompilerParams(dimension_semantics=("arbitrary",)),
        name="token_major_to_rows",
    )(x_tm)


def _moe_kernel(tb_ref, nused_ref, inv_ref, x_ref, n2g_ref, wr_ref, br_ref,
                wg_lo_ref, wu_lo_ref, wd_lo_ref, wg_hi_ref, wu_hi_ref, wd_hi_ref,
                out_hbm, obuf, ssem):
    i = pl.program_id(0)
    nused = nused_ref[0]
    slot = lax.rem(i, 2)

    def scatter_row(r, s):
        tok = inv_ref[0, r]
        return pltpu.make_async_copy(obuf.at[s, pl.ds(r * SUBLANES, SUBLANES)],
                                     out_hbm.at[pl.ds(tok * SUBLANES, SUBLANES)], ssem.at[s])

    def start_rows(row_copy):
        def body(g, carry):
            for u in range(ISSUE_UNROLL):
                row_copy(g * ISSUE_UNROLL + u).start(priority=u % 2)
            return carry
        lax.fori_loop(0, TMS // ISSUE_UNROLL, body, 0)

    def wait_tile(sem, s):
        for _ in range(TMS // WAIT_ROWS):
            pltpu.make_async_copy(obuf.at[s, pl.ds(0, WAIT_ROWS * SUBLANES)],
                                  out_hbm.at[pl.ds(0, WAIT_ROWS * SUBLANES)], sem.at[s]).wait()

    @pl.when(i == 0)
    def _():
        obuf[...] = jnp.zeros_like(obuf)
        for s in range(2):
            junk = pltpu.make_async_copy(
                obuf.at[s], out_hbm.at[pl.ds((NTOK + s * TMS) * SUBLANES, TMS * SUBLANES)], ssem.at[s])
            junk.start()
            junk.wait()

    @pl.when(i < nused)
    def _():
        @pl.when(i >= 2)
        def _():
            wait_tile(ssem, slot)
        x = _load_token_major(x_ref, TMS)
        ms = jnp.mean(x * x, axis=-1, keepdims=True)
        hb = (x * lax.rsqrt(ms + EPS) * n2g_ref[...]).astype(BF16)
        logits = _dot(hb, wr_ref[...]) + br_ref[...]
        g, e_lo, e_hi = _bucket_experts(tb_ref[i])
        lane = lax.broadcasted_iota(jnp.int32, (TMS, LANES), 1)
        pick = lambda l: jnp.sum(jnp.where(lane == l, logits, 0.0), axis=-1, keepdims=True)
        lg, l_lo, l_hi = pick(g), pick(8 + e_lo), pick(8 + e_hi)
        gsum = jnp.sum(jnp.where(lane < N_GROUPS, jnp.exp(logits - lg), 0.0), axis=-1, keepdims=True)
        g_w = 1.0 / gsum
        w_lo = g_w / (1.0 + jnp.exp(l_hi - l_lo))
        w_hi = g_w / (1.0 + jnp.exp(l_lo - l_hi))

        def expert(wg_ref, wu_ref, wd_ref, w):
            a = _dot(hb, wg_ref[...])
            hid = a * _sigmoid(a) * _dot(hb, wu_ref[...]) * w
            return _dot(hid.astype(BF16), wd_ref[...])
        y = (x + expert(wg_lo_ref, wu_lo_ref, wd_lo_ref, w_lo)
             + expert(wg_hi_ref, wu_hi_ref, wd_hi_ref, w_hi))
        _store_token_major(obuf.at[slot], y, TMS)

        start_rows(lambda r: scatter_row(r, slot))

    @pl.when(i == nused - 1)
    def _():
        wait_tile(ssem, slot)

        @pl.when(i >= 1)
        def _():
            wait_tile(ssem, 1 - slot)


def _moe_call(tile_bucket, nused, inv, xs_tm, n2g, wr, br, wg, wu, wd):
    def used(i, nu):
        return jnp.minimum(i, nu[0] - 1)

    def w_spec(shape, which):
        def index(i, tb, nu):
            return (_bucket_experts(tb[used(i, nu)])[which], 0, 0)
        return pl.BlockSpec((None,) + shape, index)

    const = lambda shape: pl.BlockSpec(shape, lambda i, tb, nu: (0, 0))
    up, down = (D_MODEL, D_EXPERT), (D_EXPERT, D_MODEL)
    inv3 = inv.reshape(NT_MAX, 1, TMS)
    return pl.pallas_call(
        _moe_kernel,
        grid_spec=pltpu.PrefetchScalarGridSpec(
            num_scalar_prefetch=2,
            grid=(NT_MAX,),
            in_specs=[
                pl.BlockSpec((None, 1, TMS), lambda i, tb, nu: (i, 0, 0), memory_space=pltpu.SMEM),
                pl.BlockSpec((TMS * SUBLANES, LANES), lambda i, tb, nu: (used(i, nu), 0)),
                const((1, D_MODEL)), const((D_MODEL, LANES)), const((1, LANES)),
                w_spec(up, 1), w_spec(up, 1), w_spec(down, 1),
                w_spec(up, 2), w_spec(up, 2), w_spec(down, 2),
            ],
            out_specs=pl.BlockSpec(memory_space=pl.ANY),
            scratch_shapes=[pltpu.VMEM((2, TMS * SUBLANES, LANES), F32), pltpu.SemaphoreType.DMA((2,))],
        ),
        out_shape=jax.ShapeDtypeStruct(((NTOK + JUNK_ROWS) * SUBLANES, LANES), F32),
        compiler_params=pltpu.CompilerParams(
            dimension_semantics=("arbitrary",), vmem_limit_bytes=VMEM_LIMIT, has_side_effects=True),
        name="hier_moe",
    )(tile_bucket, nused, inv3, xs_tm, n2g, wr, br, wg, wu, wd, wg, wu, wd)


def _rope_tables(positions):
    inv = ROPE_THETA ** (-jnp.arange(0, ROT_DIM, 2, dtype=F32) / ROT_DIM)
    ang = positions.astype(F32)[..., None] * inv
    half = ROT_DIM // 2
    dim = jnp.arange(LANES) % HEAD_DIM
    rotated = dim < ROT_DIM
    expand = ((jnp.arange(half)[:, None] == dim[None, :] % half) & rotated[None, :]).astype(F32)
    spread = lambda t: jnp.dot(t.reshape(-1, half), expand,
                               precision=lax.Precision.HIGHEST).reshape(t.shape[:-1] + (LANES,))
    return spread(jnp.cos(ang)) + (~rotated).astype(F32), spread(jnp.sin(ang))


def kernel(x, positions, norm1_g, w_in, sgu_ln_g, sgu_ln_b, sgu_w, sgu_b, q_norm_g, k_norm_g, sinks, conv3_w, conv31_w, conv31_b, cnorm_g, cnorm_b, w_branch, w_gate, b_gate, w_o, norm2_g, w_group, b_group, w_expert, b_expert, w_e_gate, w_e_up, w_e_down):
    bsz, seq, d = x.shape
    assert (bsz, seq, d) == (BATCH, SEQ, D_MODEL)
    ntok = bsz * seq
    depth = norm1_g.shape[0]
    cos_t, sin_t = _rope_tables(positions)
    row = lambda t: t.reshape(1, -1)
    for l in range(depth):
        sgu_wcat = jnp.transpose(sgu_w[l], (1, 0, 2)).reshape(SGU_CHUNK, SGU_GROUPS * SGU_CHUNK)
        sgu_bias = jnp.repeat(sgu_b[l].T, MIX_W // SGU_GROUPS, axis=1)
        c3w = jnp.zeros((SUBLANES, MIX_W), F32).at[:SHORT_CONV].set(conv3_w[l])
        c31w = jnp.zeros((C31_HALO, MIX_W), F32).at[:CONFORMER_CONV].set(conv31_w[l])
        wr = jnp.zeros((d, LANES), F32).at[:, 0:N_GROUPS].set(w_group[l]).at[:, 8:8 + N_EXPERTS].set(w_expert[l])
        wr = wr.astype(BF16)
        br = jnp.zeros((1, LANES), F32).at[0, 0:N_GROUPS].set(b_group[l]).at[0, 8:8 + N_EXPERTS].set(b_expert[l])
        inv, xs, tile_bucket, meta = _mixing_call(
            x, cos_t, sin_t, row(norm1_g[l]), w_in[l].astype(BF16),
            row(sgu_ln_g[l]), row(sgu_ln_b[l]), sgu_wcat, sgu_bias,
            row(jnp.tile(q_norm_g[l], N_Q_HEADS)), row(jnp.tile(k_norm_g[l], N_KV_HEADS)), sinks[l],
            c3w, c31w, row(conv31_b[l]), row(cnorm_g[l]), row(cnorm_b[l]),
            w_branch[l].astype(BF16), (0.5 * w_gate[l]).astype(BF16), row(0.5 * b_gate[l]), w_o[l].astype(BF16),
            row(norm2_g[l]), wr, br)
        x = _moe_call(tile_bucket[0], meta[0, 0:1], inv, xs, row(norm2_g[l]), wr, br,
                      w_e_gate[l].astype(BF16), w_e_up[l].astype(BF16), w_e_down[l].astype(BF16))
    return _to_rows_call(x, ntok).reshape(bsz, seq, d)
```

```python
import functools

import jax
import jax.numpy as jnp
from jax import lax
from jax.experimental import pallas as pl
from jax.experimental.pallas import tpu as pltpu

D_MODEL = 1024
N_BRANCH = 4
MIX_W = 256
SGU_GROUPS = 4
SGU_CHUNK = 128
N_Q_HEADS = 4
N_KV_HEADS = 2
HEAD_DIM = 64
WINDOW = 128
ROT_DIM = HEAD_DIM // 4
ROPE_THETA = 500000.0
SHORT_CONV = 3
CONFORMER_CONV = 31
N_GROUPS = 4
EXPERTS_PER_GROUP = 4
N_EXPERTS = N_GROUPS * EXPERTS_PER_GROUP
D_EXPERT = 256
EPS = 1e-6
D_IN = 2304

BATCH = 8
SEQ = 4096
PAIRS_PER_GROUP = EXPERTS_PER_GROUP * (EXPERTS_PER_GROUP - 1) // 2
N_BUCKETS = N_GROUPS * PAIRS_PER_GROUP
NB_PAD = 32

LANES = 128
SUBLANES = 8
TM = 512
TMS = 256
NTOK = BATCH * SEQ
N_TILES = NTOK // TM
TILES_PER_SEQ = SEQ // TM
NT_MAX = NTOK // TMS + N_BUCKETS
JUNK_ROWS = 2 * TMS
TB_LEN = 256
N_MIX_IN = 24
NEW_TILES_MAX = TM // TMS + 1
SORT_SUB = 4
ISSUE_UNROLL = 8
WAIT_ROWS = 128
C3_HALO = SUBLANES
C31_HALO = 32
CONV_ROWS = 64
VMEM_LIMIT = 56 * 1024 * 1024
NEG = -1e30

F32 = jnp.float32
BF16 = jnp.bfloat16


def _dot(a, b):
    return jnp.dot(a, b, preferred_element_type=F32)


def _gelu_tanh(x):
    c = 0.7978845608028654
    return 0.5 * x * (1.0 + jnp.tanh(c * (x + 0.044715 * (x * x * x))))


def _sigmoid(x):
    return 0.5 * jnp.tanh(0.5 * x) + 0.5


def _layernorm(x, g, b):
    mu = jnp.mean(x, axis=-1, keepdims=True)
    xc = x - mu
    var = jnp.mean(xc * xc, axis=-1, keepdims=True)
    return xc * lax.rsqrt(var + EPS) * g + b


def _head_meansq(t, width):
    r = lax.broadcasted_iota(jnp.int32, (width, width), 0) // HEAD_DIM
    c = lax.broadcasted_iota(jnp.int32, (width, width), 1) // HEAD_DIM
    bd = jnp.where(r == c, 1.0 / HEAD_DIM, 0.0).astype(BF16)
    t2 = t * t
    hi = t2.astype(BF16)
    lo = (t2 - hi.astype(F32)).astype(BF16)
    return _dot(hi, bd) + _dot(lo, bd)


def _rope(t, c, s1, s2):
    w = t.shape[-1]
    half = ROT_DIM // 2
    return t * c + pltpu.roll(t, w - half, axis=1) * s1 + pltpu.roll(t, half, axis=1) * s2


def _mixing_step(token_major_in, j, x_ref, cos_ref, sin_ref, n1g_ref, win_ref,
                 sgu_g_ref, sgu_b_ref, sgu_w_ref, sgu_bias_ref,
                 qg_ref, kg_ref, sinks_ref,
                 c3w_ref, c31w_ref, c31b_ref, cng_ref, cnb_ref,
                 wbr_ref, wgate_ref, bgate_ref, wo_ref,
                 n2g_ref, wr_ref, br_ref,
                 out_ref, slots_ref, alloc_state, alloc_valid,
                 kprev_ref, vprev_ref, c3buf_ref, c31buf_ref,
                 ys_ref, hb_ref, xk_ref, ys_in, hb_in, xk_in):
    x = _load_token_major(x_ref, TM) if token_major_in else x_ref[...]
    xk_ref[...] = x
    ms = jnp.mean(x * x, axis=-1, keepdims=True)
    hb = (x * lax.rsqrt(ms + EPS) * n1g_ref[...]).astype(BF16)
    hb_ref[...] = hb

    uv = _dot(hb, win_ref[:, 0:512])
    u = _gelu_tanh(uv[:, 0:MIX_W])
    v = _layernorm(_gelu_tanh(uv[:, MIX_W:2 * MIX_W]), sgu_g_ref[...], sgu_b_ref[...])
    tt = lax.broadcasted_iota(jnp.int32, (SGU_CHUNK, SGU_GROUPS * SGU_CHUNK), 0)
    ss = lax.broadcasted_iota(jnp.int32, (SGU_CHUNK, SGU_GROUPS * SGU_CHUNK), 1) % SGU_CHUNK
    wcat = jnp.where(ss <= tt, sgu_w_ref[...], 0.0).astype(BF16)
    lane_grp = lax.broadcasted_iota(jnp.int32, (SGU_CHUNK, MIX_W), 1) // (MIX_W // SGU_GROUPS)
    for c in range(TM // SGU_CHUNK):
        rows = slice(c * SGU_CHUNK, (c + 1) * SGU_CHUNK)
        vc = v[rows]
        vstack = jnp.concatenate(
            [jnp.where(lane_grp == g, vc, 0.0) for g in range(SGU_GROUPS)], axis=0).astype(BF16)
        z = _dot(wcat, vstack) + sgu_bias_ref[...]
        ys_ref[0, rows, :] = (u[rows] * z).astype(BF16)

    qkv = _dot(hb, win_ref[:, 512:1024])
    rc, sin_t = cos_ref[...], sin_ref[...]
    head_dim_idx = lax.broadcasted_iota(jnp.int32, (1, LANES), 1) % HEAD_DIM
    rs1 = jnp.where(head_dim_idx < ROT_DIM // 2, -sin_t, 0.0)
    rs2 = jnp.where((head_dim_idx >= ROT_DIM // 2) & (head_dim_idx < ROT_DIM), sin_t, 0.0)
    q = qkv[:, 0:256]
    q = q * lax.rsqrt(_head_meansq(q, 256) + EPS) * qg_ref[...]
    q = _rope(q, jnp.concatenate([rc, rc], axis=1), jnp.concatenate([rs1, rs1], axis=1),
              jnp.concatenate([rs2, rs2], axis=1)) * (HEAD_DIM ** -0.5)
    k = qkv[:, 256:384]
    k = k * lax.rsqrt(_head_meansq(k, 128) + EPS) * kg_ref[...]
    k = _rope(k, rc, rs1, rs2)
    vv = qkv[:, 384:512]
    kfull = jnp.concatenate([kprev_ref[...], k], axis=0)
    vfull = jnp.concatenate([vprev_ref[...], vv], axis=0)
    kprev_ref[...] = k[TM - WINDOW:TM]
    vprev_ref[...] = vv[TM - WINDOW:TM]
    low_full = lax.broadcasted_iota(jnp.int32, (WINDOW + TM, LANES), 1) < HEAD_DIM
    krot = pltpu.roll(kfull, HEAD_DIM, axis=1)
    vrot = pltpu.roll(vfull, HEAD_DIM, axis=1)
    kdup = [jnp.where(low_full, kfull, krot).astype(BF16), jnp.where(low_full, krot, kfull).astype(BF16)]
    vdup = [jnp.where(low_full, vfull, vrot).astype(BF16), jnp.where(low_full, vrot, vfull).astype(BF16)]
    low = lax.broadcasted_iota(jnp.int32, (WINDOW, LANES), 1) < HEAD_DIM
    row2 = lax.broadcasted_iota(jnp.int32, (2 * WINDOW, 2 * WINDOW), 0)
    qi = row2 % WINDOW
    ki = lax.broadcasted_iota(jnp.int32, (2 * WINDOW, 2 * WINDOW), 1)
    in_prev = (ki < WINDOW) & (ki > qi)
    in_cur = (ki >= WINDOW) & (ki - WINDOW <= qi)
    is_g0 = lax.broadcasted_iota(jnp.int32, (2 * WINDOW, 1), 0) < WINDOW
    for n in range(TM // WINDOW):
        if n == 0:
            valid = (in_prev & (j > 0)) | in_cur
        else:
            valid = in_prev | in_cur
        cols = []
        for h in range(N_KV_HEADS):
            qcol = q[n * WINDOW:(n + 1) * WINDOW, h * LANES:(h + 1) * LANES]
            qs = jnp.concatenate([jnp.where(low, qcol, 0.0), jnp.where(low, 0.0, qcol)], axis=0).astype(BF16)
            kk = kdup[h][n * WINDOW:(n + 2) * WINDOW]
            sc = lax.dot_general(qs, kk, (((1,), (1,)), ((), ())), preferred_element_type=F32)
            sc = jnp.where(valid, sc, NEG)
            sink = jnp.where(is_g0, sinks_ref[2 * h], sinks_ref[2 * h + 1])
            m = jnp.maximum(jnp.max(sc, axis=-1, keepdims=True), sink)
            p = jnp.exp(sc - m)
            denom = jnp.sum(p, axis=-1, keepdims=True) + jnp.exp(sink - m)
            o = _dot(p.astype(BF16), vdup[h][n * WINDOW:(n + 2) * WINDOW]) / denom
            cols.append(jnp.where(low, o[0:WINDOW], o[WINDOW:2 * WINDOW]))
        ys_ref[1, n * WINDOW:(n + 1) * WINDOW, :] = jnp.concatenate(cols, axis=1).astype(BF16)

    cc = _dot(hb, win_ref[:, 1024:1792])
    c3buf_ref[C3_HALO:C3_HALO + TM, :] = cc[:, 0:256] * cc[:, 512:768]
    conv = c3w_ref[0:1, :] * c3buf_ref[C3_HALO - 2:C3_HALO - 2 + TM, :]
    conv = conv + c3w_ref[1:2, :] * c3buf_ref[C3_HALO - 1:C3_HALO - 1 + TM, :]
    conv = conv + c3w_ref[2:3, :] * c3buf_ref[C3_HALO:C3_HALO + TM, :]
    ys_ref[2] = (cc[:, 256:512] * conv).astype(BF16)
    c3buf_ref[0:C3_HALO, :] = c3buf_ref[TM:TM + C3_HALO, :]

    dd = _dot(hb, win_ref[:, 1792:2304])
    c31buf_ref[C31_HALO:C31_HALO + TM, :] = dd[:, 0:256] * _sigmoid(dd[:, 256:512])
    base = C31_HALO - (CONFORMER_CONV - 1)
    for r in range(TM // CONV_ROWS):
        acc = jnp.broadcast_to(c31b_ref[...], (CONV_ROWS, MIX_W))
        for sub in range(SUBLANES):
            taps = [t for t in range(CONFORMER_CONV) if (base + t) % SUBLANES == sub]
            last = (base + taps[-1]) // SUBLANES * SUBLANES
            start = r * CONV_ROWS + sub
            window = c31buf_ref[start:start + last + CONV_ROWS, :]
            part = None
            for tap in taps:
                off = (base + tap) // SUBLANES * SUBLANES
                term = c31w_ref[tap:tap + 1, :] * window[off:off + CONV_ROWS]
                part = term if part is None else part + term
            acc = acc + part
        yn = _layernorm(acc, cng_ref[...], cnb_ref[...])
        ys_ref[3, r * CONV_ROWS:(r + 1) * CONV_ROWS, :] = (yn * _sigmoid(yn)).astype(BF16)

    c31buf_ref[0:C31_HALO, :] = c31buf_ref[TM:TM + C31_HALO, :]

    hb_prev = hb_in[...]
    merged = jnp.zeros((TM, D_MODEL), F32)
    for b in range(N_BRANCH):
        yb = _dot(ys_in[b], wbr_ref[b])
        half_gate = (_dot(hb_prev, wgate_ref[:, b * D_MODEL:(b + 1) * D_MODEL])
                     + bgate_ref[:, b * D_MODEL:(b + 1) * D_MODEL])
        merged = merged + (jnp.tanh(half_gate) * yb + yb)
    xo = xk_in[...] + _dot((0.5 * merged).astype(BF16), wo_ref[...])
    _store_token_major(out_ref, xo, TM)

    ms2 = jnp.mean(xo * xo, axis=-1, keepdims=True)
    h2 = (xo * lax.rsqrt(ms2 + EPS) * n2g_ref[...]).astype(BF16)
    logits = _dot(h2, wr_ref[...]) + br_ref[...]
    slots_ref[...] = _allocate(_route_bucket(logits.T), alloc_valid, *alloc_state)


def _mixing_kernel(token_major_in, *refs):
    ins = refs[:N_MIX_IN]
    inv_ref, xs_hbm, tb_ref, meta_ref = refs[N_MIX_IN:N_MIX_IN + 4]
    (kprev_ref, vprev_ref, c3buf_ref, c31buf_ref, ys_a, ys_b, hb_a, hb_b, xk_a, xk_b, ob_ref,
     cnt_ref, cur_ref, gal_ref, tbacc_ref, upper_ref, slotv_a, slotv_b, slot_smem, statev_ref, state_smem,
     ztile_ref, sem, csem, zsem) = refs[N_MIX_IN + 4:]
    carries = (kprev_ref, vprev_ref, c3buf_ref, c31buf_ref)
    alloc_state = (cnt_ref, cur_ref, gal_ref, tbacc_ref, upper_ref)
    s = pl.program_id(0)
    ob_merge = ob_ref.at[lax.rem(s + 2, 3)]
    ob_issue = ob_ref.at[lax.rem(s + 1, 3)]
    tile = jnp.minimum(s, N_TILES - 1)
    j = lax.rem(tile, TILES_PER_SEQ)

    @pl.when((j == 0) & (s < N_TILES))
    def _():
        kprev_ref[...] = jnp.zeros_like(kprev_ref)
        vprev_ref[...] = jnp.zeros_like(vprev_ref)
        c3buf_ref[0:C3_HALO, :] = jnp.zeros((C3_HALO, MIX_W), F32)
        c31buf_ref[0:C31_HALO, :] = jnp.zeros((C31_HALO, MIX_W), F32)

    @pl.when(s == 0)
    def _():
        ys_b[...] = jnp.zeros_like(ys_b)
        hb_b[...] = jnp.zeros_like(hb_b)
        xk_b[...] = jnp.zeros_like(xk_b)
        for ref in (cnt_ref, cur_ref, gal_ref, tbacc_ref, ztile_ref):
            ref[...] = jnp.zeros_like(ref)
        r_i = lax.broadcasted_iota(jnp.int32, (TM, TM), 0)
        c_i = lax.broadcasted_iota(jnp.int32, (TM, TM), 1)
        upper_ref[...] = jnp.where(r_i <= c_i, 1.0, 0.0).astype(BF16)

    valid = s >= 1

    @pl.when(lax.rem(s, 2) == 0)
    def _():
        _mixing_step(token_major_in, j, *ins, ob_merge, slotv_a, alloc_state, valid, *carries,
                     ys_a, hb_a, xk_a, ys_b, hb_b, xk_b)

    @pl.when(lax.rem(s, 2) == 1)
    def _():
        _mixing_step(token_major_in, j, *ins, ob_merge, slotv_b, alloc_state, valid, *carries,
                     ys_b, hb_b, xk_b, ys_a, hb_a, xk_a)

    def slots_to_smem(parity):
        return pltpu.make_async_copy(slotv_b if parity else slotv_a, slot_smem.at[parity], csem)

    def wait_rows():
        for _ in range(TM // WAIT_ROWS):
            pltpu.make_async_copy(ob_ref.at[0, pl.ds(0, WAIT_ROWS * SUBLANES)],
                                  xs_hbm.at[pl.ds(0, WAIT_ROWS * SUBLANES)], sem).wait()

    def scatter_rows(src, slots, first_token):
        def place(g, carry):
            for u in range(ISSUE_UNROLL):
                i = g * ISSUE_UNROLL + u
                slot = slots[0, i]
                inv_ref[slot] = first_token + i
                pltpu.make_async_copy(src.at[pl.ds(i * SUBLANES, SUBLANES)],
                                      xs_hbm.at[pl.ds(slot * SUBLANES, SUBLANES)], sem).start(priority=u % 2)
            return carry
        lax.fori_loop(0, TM // ISSUE_UNROLL, place, 0)

    for parity in (0, 1):
        @pl.when((s >= 2) & (lax.rem(s, 2) != parity))
        def _():
            slots_to_smem(parity).wait()
    for parity in (0, 1):
        @pl.when((s >= 1) & (lax.rem(s, 2) == parity))
        def _():
            slots_to_smem(parity).start()

    @pl.when(s >= 3)
    def _():
        wait_rows()

    @pl.when(s >= 2)
    def _():
        scatter_rows(ob_issue, slot_smem.at[lax.rem(s + 1, 2)], (s - 2) * TM)

    @pl.when(s == N_TILES)
    def _():
        last = N_TILES - 1
        slots_to_smem(N_TILES % 2).wait()
        wait_rows()
        scatter_rows(ob_ref.at[last % 3], slot_smem.at[N_TILES % 2], last * TM)
        wait_rows()
        _finish_sorted_layout(cnt_ref, cur_ref, gal_ref, tbacc_ref, statev_ref, state_smem, ztile_ref,
                              inv_ref, xs_hbm, tb_ref, meta_ref, zsem)


def _allocate(bucket, valid, cnt_ref, cur_ref, gal_ref, tbacc_ref, upper_ref):
    rows = lax.broadcasted_iota(jnp.int32, (NB_PAD, TM), 0)
    onehot = jnp.where(rows == bucket, 1.0, 0.0)
    prefix = _dot(onehot.astype(BF16), upper_ref[...])
    total = prefix[:, TM - 1:TM]
    cnt0, cur0, gal0 = cnt_ref[...], cur_ref[...], gal_ref[...]
    tiles_before = jnp.floor((cnt0 + (TMS - 1)) * (1.0 / TMS))
    tiles_after = jnp.floor((cnt0 + total + (TMS - 1)) * (1.0 / TMS))
    n_new = tiles_after - tiles_before
    rr = lax.broadcasted_iota(jnp.int32, (NB_PAD, NB_PAD), 0)
    cc = lax.broadcasted_iota(jnp.int32, (NB_PAD, NB_PAD), 1)
    strict_lower = jnp.where(cc < rr, 1.0, 0.0).astype(BF16)
    n_new_b = jnp.broadcast_to(n_new, (NB_PAD, LANES))
    first_new = gal0 + _dot(strict_lower, n_new_b.astype(BF16))[:, 0:1]
    rank = cnt0 + prefix - 1.0
    ordinal = jnp.floor(rank * (1.0 / TMS))
    fresh = ordinal - tiles_before
    tile_id = jnp.where(fresh < 0, cur0, first_new + fresh)
    slot = tile_id * TMS + (rank - ordinal * TMS)
    cnt_ref[...] = jnp.where(valid, cnt0 + total, cnt0)
    cur_ref[...] = jnp.where(valid & (n_new > 0), first_new + n_new - 1.0, cur0)
    gal_ref[...] = jnp.where(valid, gal0 + jnp.sum(n_new_b, axis=0, keepdims=True)[:, 0:1], gal0)
    lane = lax.broadcasted_iota(jnp.int32, (NB_PAD, TB_LEN), 1).astype(F32)
    bucket_id = lax.broadcasted_iota(jnp.int32, (NB_PAD, TB_LEN), 0).astype(F32)
    owner = jnp.zeros((NB_PAD, TB_LEN), F32)
    for extra in range(NEW_TILES_MAX):
        owner = owner + jnp.where((lane == first_new + extra) & (n_new > extra), bucket_id, 0.0)
    tbacc_ref[...] += jnp.where(valid, jnp.sum(owner, axis=0, keepdims=True), 0.0)
    return jnp.sum(onehot * slot, axis=0, keepdims=True).astype(jnp.int32)


def _finish_sorted_layout(cnt_ref, cur_ref, gal_ref, tbacc_ref, statev_ref, state_smem, ztile_ref,
                          inv_ref, xs_hbm, tb_ref, meta_ref, zsem):
    col = lax.broadcasted_iota(jnp.int32, (NB_PAD, LANES), 1)
    state = jnp.where(col == 0, cnt_ref[...], jnp.where(col == 1, cur_ref[...], gal_ref[...]))
    statev_ref[...] = state.astype(jnp.int32)
    meta_ref[...] = jnp.broadcast_to(gal_ref[...], (NB_PAD, LANES)).astype(jnp.int32)
    tb_ref[...] = tbacc_ref[...].astype(jnp.int32)
    to_smem = pltpu.make_async_copy(statev_ref, state_smem, zsem)
    to_smem.start()
    to_smem.wait()
    nused = state_smem[0, 2]

    def fill(slot, carry):
        inv_ref[slot] = NTOK + (slot & (JUNK_ROWS - 1))
        return carry

    def zero_rows(first_slot, n_rows):
        return pltpu.make_async_copy(ztile_ref.at[pl.ds(0, n_rows * SUBLANES)],
                                     xs_hbm.at[pl.ds(first_slot * SUBLANES, n_rows * SUBLANES)], zsem)

    for wait in (False, True):
        for k in range(N_BUCKETS):
            used_rows = state_smem[k, 0] & (TMS - 1)
            n_pad = jnp.where(used_rows > 0, TMS - used_rows, 0)
            slot = state_smem[k, 1] * TMS + (TMS - n_pad)
            if not wait:
                lax.fori_loop(slot, slot + n_pad, fill, 0)
            for bit in reversed(range(TMS.bit_length() - 1)):
                piece = n_pad & (1 << bit)

                @pl.when(piece != 0)
                def _():
                    copy = zero_rows(slot, 1 << bit)
                    copy.wait() if wait else copy.start()
                slot = slot + piece

        def unused(t, carry):
            copy = zero_rows(t * TMS, TMS)
            copy.wait() if wait else copy.start()
            return carry
        lax.fori_loop(nused, NT_MAX, unused, 0)
    lax.fori_loop(nused * TMS, NT_MAX * TMS, fill, 0)


def _const_spec(shape):
    zeros = (0,) * len(shape)
    return pl.BlockSpec(shape, lambda s: zeros, pipeline_mode=pl.Buffered(1))


def _mixing_call(x, cos_t, sin_t, n1g, w_in, sgu_g, sgu_b, sgu_w, sgu_bias, qg, kg, sinks,
                 c3w, c31w, c31b, cng, cnb, wbr, wgate, bgate, wo, n2g, wr, br):
    cur = lambda s: jnp.minimum(s, N_TILES - 1)
    tok = lambda width: pl.BlockSpec(
        (None, TM, width), lambda s: (cur(s) // TILES_PER_SEQ, lax.rem(cur(s), TILES_PER_SEQ), 0))
    token_major_in = x.ndim == 2
    in_specs = [
        pl.BlockSpec((TM * SUBLANES, LANES), lambda s: (cur(s), 0)) if token_major_in else tok(D_MODEL),
        tok(LANES), tok(LANES),
        _const_spec((1, D_MODEL)), _const_spec((D_MODEL, D_IN)),
        _const_spec((1, MIX_W)), _const_spec((1, MIX_W)),
        _const_spec((SGU_CHUNK, SGU_GROUPS * SGU_CHUNK)), _const_spec((SGU_CHUNK, MIX_W)),
        _const_spec((1, N_Q_HEADS * HEAD_DIM)), _const_spec((1, N_KV_HEADS * HEAD_DIM)),
        pl.BlockSpec(memory_space=pltpu.SMEM),
        _const_spec((SUBLANES, MIX_W)), _const_spec((C31_HALO, MIX_W)), _const_spec((1, MIX_W)),
        _const_spec((1, MIX_W)), _const_spec((1, MIX_W)),
        _const_spec((N_BRANCH, MIX_W, D_MODEL)), _const_spec((D_MODEL, N_BRANCH * D_MODEL)),
        _const_spec((1, N_BRANCH * D_MODEL)), _const_spec((D_MODEL, D_MODEL)),
        _const_spec((1, D_MODEL)), _const_spec((D_MODEL, LANES)), _const_spec((1, LANES)),
    ]
    return pl.pallas_call(
        functools.partial(_mixing_kernel, token_major_in),
        grid=(N_TILES + 1,),
        in_specs=in_specs,
        out_specs=[pl.BlockSpec(memory_space=pltpu.SMEM), pl.BlockSpec(memory_space=pl.ANY),
                   pl.BlockSpec((1, TB_LEN), lambda s: (0, 0)), pl.BlockSpec((NB_PAD, LANES), lambda s: (0, 0))],
        out_shape=[jax.ShapeDtypeStruct((NT_MAX * TMS,), jnp.int32),
                   jax.ShapeDtypeStruct((NT_MAX * TMS * SUBLANES, LANES), F32),
                   jax.ShapeDtypeStruct((1, TB_LEN), jnp.int32),
                   jax.ShapeDtypeStruct((NB_PAD, LANES), jnp.int32)],
        scratch_shapes=[
            pltpu.VMEM((WINDOW, LANES), F32), pltpu.VMEM((WINDOW, LANES), F32),
            pltpu.VMEM((C3_HALO + TM, MIX_W), F32), pltpu.VMEM((C31_HALO + TM, MIX_W), F32),
            pltpu.VMEM((N_BRANCH, TM, MIX_W), BF16), pltpu.VMEM((N_BRANCH, TM, MIX_W), BF16),
            pltpu.VMEM((TM, D_MODEL), BF16), pltpu.VMEM((TM, D_MODEL), BF16),
            pltpu.VMEM((TM, D_MODEL), F32), pltpu.VMEM((TM, D_MODEL), F32),
            pltpu.VMEM((3, TM * SUBLANES, LANES), F32),
            pltpu.VMEM((NB_PAD, 1), F32), pltpu.VMEM((NB_PAD, 1), F32), pltpu.VMEM((NB_PAD, 1), F32),
            pltpu.VMEM((1, TB_LEN), F32), pltpu.VMEM((TM, TM), BF16),
            pltpu.VMEM((1, TM), jnp.int32), pltpu.VMEM((1, TM), jnp.int32), pltpu.SMEM((2, 1, TM), jnp.int32),
            pltpu.VMEM((NB_PAD, LANES), jnp.int32), pltpu.SMEM((NB_PAD, LANES), jnp.int32),
            pltpu.VMEM((TMS * SUBLANES, LANES), F32),
            pltpu.SemaphoreType.DMA(()), pltpu.SemaphoreType.DMA(()), pltpu.SemaphoreType.DMA(()),
        ],
        compiler_params=pltpu.CompilerParams(
            dimension_semantics=("arbitrary",), vmem_limit_bytes=VMEM_LIMIT, has_side_effects=True),
        name="mixing_block",
    )(x, cos_t, sin_t, n1g, w_in, sgu_g, sgu_b, sgu_w, sgu_bias, qg, kg, sinks,
      c3w, c31w, c31b, cng, cnb, wbr, wgate, bgate, wo, n2g, wr, br)


def _route_bucket(logits_t):
    g = [logits_t[i:i + 1, :] for i in range(N_GROUPS)]
    gmax, gidx = g[0], jnp.zeros(g[0].shape, jnp.int32)
    for i in range(1, N_GROUPS):
        better = g[i] > gmax
        gmax = jnp.where(better, g[i], gmax)
        gidx = jnp.where(better, i, gidx)
    e = []
    for i in range(EXPERTS_PER_GROUP):
        ei = logits_t[8 + i:9 + i, :]
        for grp in range(1, N_GROUPS):
            row = 8 + grp * EXPERTS_PER_GROUP + i
            ei = jnp.where(gidx == grp, logits_t[row:row + 1, :], ei)
        e.append(ei)
    v1, i1 = e[0], jnp.zeros_like(gidx)
    for i in range(1, EXPERTS_PER_GROUP):
        better = e[i] > v1
        v1 = jnp.where(better, e[i], v1)
        i1 = jnp.where(better, i, i1)
    v2, i2 = jnp.full_like(v1, -jnp.inf), jnp.zeros_like(gidx)
    for i in range(EXPERTS_PER_GROUP):
        better = (e[i] > v2) & (i1 != i)
        v2 = jnp.where(better, e[i], v2)
        i2 = jnp.where(better, i, i2)
    lo, hi = jnp.minimum(i1, i2), jnp.maximum(i1, i2)
    pair = jnp.where(lo == 0, hi - 1, jnp.where(lo == 1, hi + 1, 5))
    return gidx * PAIRS_PER_GROUP + pair


def _bucket_experts(b):
    g = b // PAIRS_PER_GROUP
    pair = b - g * PAIRS_PER_GROUP
    lo = jnp.where(pair < 3, 0, jnp.where(pair < 5, 1, 2))
    hi = jnp.where(pair < 3, pair + 1, jnp.where(pair < 5, pair - 1, 3))
    return g, g * EXPERTS_PER_GROUP + lo, g * EXPERTS_PER_GROUP + hi


def _sort_kernel(bucket_ref, pos_ref, tb_ref, meta_ref, carry_ref, base_ref, upper_ref):
    p = pl.program_id(0)
    j = pl.program_id(1)
    rows = lax.broadcasted_iota(jnp.int32, (NB_PAD, TM), 0)

    def one_hot(q):
        return jnp.where(rows == bucket_ref[q], 1.0, 0.0)

    @pl.when((p == 0) & (j == 0))
    def _():
        carry_ref[...] = jnp.zeros_like(carry_ref)
        r_i = lax.broadcasted_iota(jnp.int32, (TM, TM), 0)
        c_i = lax.broadcasted_iota(jnp.int32, (TM, TM), 1)
        upper_ref[...] = jnp.where(r_i <= c_i, 1.0, 0.0).astype(BF16)

    @pl.when(p == 0)
    def _():
        for q in range(SORT_SUB):
            carry_ref[...] += jnp.sum(one_hot(q), axis=1, keepdims=True)
        pos_ref[...] = jnp.zeros_like(pos_ref)

    @pl.when((p == 1) & (j == 0))
    def _():
        cnt = carry_ref[...]
        ntile = jnp.floor((cnt + (TMS - 1)) * (1.0 / TMS))
        rr = lax.broadcasted_iota(jnp.int32, (NB_PAD, NB_PAD), 0)
        cc = lax.broadcasted_iota(jnp.int32, (NB_PAD, NB_PAD), 1)
        strict_lower = jnp.where(cc < rr, 1.0, 0.0).astype(BF16)
        ntile_b = jnp.broadcast_to(ntile, (NB_PAD, LANES))
        tstart = _dot(strict_lower, ntile_b.astype(BF16))
        base_ref[...] = tstart[:, 0:1] * TMS
        carry_ref[...] = jnp.zeros_like(carry_ref)
        nused = jnp.broadcast_to(jnp.sum(ntile_b, axis=0, keepdims=True), (NB_PAD, LANES))
        col = lax.broadcasted_iota(jnp.int32, (NB_PAD, LANES), 1)
        meta = jnp.where(col == 0, tstart * TMS, jnp.where(col == 1, jnp.broadcast_to(cnt, (NB_PAD, LANES)), nused))
        meta_ref[...] = meta.astype(jnp.int32)
        tile = lax.broadcasted_iota(jnp.int32, (NB_PAD, TB_LEN), 1).astype(F32)
        row_ok = lax.broadcasted_iota(jnp.int32, (NB_PAD, TB_LEN), 0) < N_BUCKETS
        started = jnp.where(row_ok & (tstart[:, 0:1] <= tile), 1.0, 0.0)
        tb_ref[...] = (jnp.sum(started, axis=0, keepdims=True) - 1.0).astype(jnp.int32)

    @pl.when(p == 1)
    def _():
        for q in range(SORT_SUB):
            onehot = one_hot(q)
            prefix = _dot(onehot.astype(BF16), upper_ref[...])
            slot = base_ref[...] + carry_ref[...] + prefix - 1.0
            pos_ref[q] = jnp.sum(onehot * slot, axis=0, keepdims=True).astype(jnp.int32)
            carry_ref[...] += prefix[:, TM - 1:TM]


def _sort_call(bucket):
    return pl.pallas_call(
        _sort_kernel,
        grid=(2, bucket.shape[0] // SORT_SUB),
        in_specs=[pl.BlockSpec((SORT_SUB, 1, TM), lambda p, j: (j, 0, 0))],
        out_specs=[
            pl.BlockSpec((None, SORT_SUB, 1, TM), lambda p, j: (p, j, 0, 0)),
            pl.BlockSpec((1, TB_LEN), lambda p, j: (0, 0)),
            pl.BlockSpec((NB_PAD, LANES), lambda p, j: (0, 0)),
        ],
        out_shape=[
            jax.ShapeDtypeStruct((2,) + bucket.shape, jnp.int32),
            jax.ShapeDtypeStruct((1, TB_LEN), jnp.int32),
            jax.ShapeDtypeStruct((NB_PAD, LANES), jnp.int32),
        ],
        scratch_shapes=[pltpu.VMEM((NB_PAD, 1), F32), pltpu.VMEM((NB_PAD, 1), F32),
                        pltpu.VMEM((TM, TM), BF16)],
        compiler_params=pltpu.CompilerParams(dimension_semantics=("arbitrary", "arbitrary")),
        name="bucket_sort",
    )(bucket)


def _dispatch_kernel(base_ref, cnt_ref, nused_ref, pos_ref, x_ref, inv_ref, xs_hbm, ztile, sem, zsem):
    c = pl.program_id(0)

    def fill(s, carry):
        inv_ref[s] = NTOK + (s & (JUNK_ROWS - 1))
        return carry

    def zero_tile(t):
        return pltpu.make_async_copy(ztile, xs_hbm.at[pl.ds(t * (TMS * SUBLANES), TMS * SUBLANES)], zsem)

    @pl.when(c == 0)
    def _():
        ztile[...] = jnp.zeros_like(ztile)
        for wait in (False, True):
            for k in range(N_BUCKETS):
                n_tiles = (cnt_ref[k] + (TMS - 1)) // TMS

                @pl.when(n_tiles > 0)
                def _():
                    copy = zero_tile(base_ref[k] // TMS + n_tiles - 1)
                    copy.wait() if wait else copy.start()

            def unused(t, carry):
                copy = zero_tile(t)
                copy.wait() if wait else copy.start()
                return carry
            lax.fori_loop(nused_ref[0], NT_MAX, unused, 0)
        for k in range(N_BUCKETS):
            n_slots = ((cnt_ref[k] + (TMS - 1)) // TMS) * TMS
            lax.fori_loop(base_ref[k] + cnt_ref[k], base_ref[k] + n_slots, fill, 0)
        lax.fori_loop(nused_ref[0] * TMS, NT_MAX * TMS, fill, 0)

    def place(g, carry):
        for u in range(ISSUE_UNROLL):
            i = g * ISSUE_UNROLL + u
            slot = pos_ref[0, i]
            inv_ref[slot] = c * TM + i
            pltpu.make_async_copy(x_ref.at[pl.ds(i * SUBLANES, SUBLANES)],
                                  xs_hbm.at[pl.ds(slot * SUBLANES, SUBLANES)], sem).start(priority=u % 2)
        return carry
    lax.fori_loop(0, TM // ISSUE_UNROLL, place, 0)
    for _ in range(TM // WAIT_ROWS):
        pltpu.make_async_copy(x_ref.at[pl.ds(0, WAIT_ROWS * SUBLANES)],
                              xs_hbm.at[pl.ds(0, WAIT_ROWS * SUBLANES)], sem).wait()


def _dispatch_call(base, cnt, nused, pos, x_tm):
    return pl.pallas_call(
        _dispatch_kernel,
        grid_spec=pltpu.PrefetchScalarGridSpec(
            num_scalar_prefetch=3,
            grid=(pos.shape[0],),
            in_specs=[pl.BlockSpec((None, 1, TM), lambda c, *_: (c, 0, 0), memory_space=pltpu.SMEM),
                      pl.BlockSpec((TM * SUBLANES, LANES), lambda c, *_: (c, 0))],
            out_specs=[pl.BlockSpec(memory_space=pltpu.SMEM), pl.BlockSpec(memory_space=pl.ANY)],
            scratch_shapes=[pltpu.VMEM((TMS * SUBLANES, LANES), F32),
                            pltpu.SemaphoreType.DMA(()), pltpu.SemaphoreType.DMA(())],
        ),
        out_shape=[jax.ShapeDtypeStruct((NT_MAX * TMS,), jnp.int32),
                   jax.ShapeDtypeStruct((NT_MAX * TMS * SUBLANES, LANES), F32)],
        compiler_params=pltpu.CompilerParams(dimension_semantics=("arbitrary",), has_side_effects=True),
        name="moe_dispatch",
    )(base, cnt, nused, pos, x_tm)


def _load_token_major(ref, rows):
    return jnp.concatenate([ref[pl.ds(c, rows, stride=SUBLANES), :] for c in range(SUBLANES)], axis=1)


def _store_token_major(ref, value, rows):
    for c in range(SUBLANES):
        ref[pl.ds(c, rows, stride=SUBLANES), :] = value[:, c * LANES:(c + 1) * LANES]


def _to_rows_kernel(x_ref, out_ref):
    out_ref[...] = _load_token_major(x_ref, TM)


def _to_rows_call(x_tm, ntok):
    return pl.pallas_call(
        _to_rows_kernel,
        grid=(ntok // TM,),
        in_specs=[pl.BlockSpec((TM * SUBLANES, LANES), lambda i: (i, 0))],
        out_specs=pl.BlockSpec((TM, D_MODEL), lambda i: (i, 0)),
        out_shape=jax.ShapeDtypeStruct((ntok, D_MODEL), F32),
        compiler_params=pltpu.CompilerParams(dimension_semantics=("arbitrary",)),
        name="token_major_to_rows",
    )(x_tm)


def _moe_kernel(tb_ref, nused_ref, inv_ref, x_ref, n2g_ref, wr_ref, br_ref,
                wg_lo_ref, wu_lo_ref, wd_lo_ref, wg_hi_ref, wu_hi_ref, wd_hi_ref,
                out_hbm, obuf, ssem):
    i = pl.program_id(0)
    nused = nused_ref[0]
    slot = lax.rem(i, 2)

    def scatter_row(r, s):
        tok = inv_ref[0, r]
        return pltpu.make_async_copy(obuf.at[s, pl.ds(r * SUBLANES, SUBLANES)],
                                     out_hbm.at[pl.ds(tok * SUBLANES, SUBLANES)], ssem.at[s])

    def start_rows(row_copy):
        def body(g, carry):
            for u in range(ISSUE_UNROLL):
                row_copy(g * ISSUE_UNROLL + u).start(priority=u % 2)
            return carry
        lax.fori_loop(0, TMS // ISSUE_UNROLL, body, 0)

    def wait_tile(sem, s):
        for _ in range(TMS // WAIT_ROWS):
            pltpu.make_async_copy(obuf.at[s, pl.ds(0, WAIT_ROWS * SUBLANES)],
                                  out_hbm.at[pl.ds(0, WAIT_ROWS * SUBLANES)], sem.at[s]).wait()

    @pl.when(i == 0)
    def _():
        obuf[...] = jnp.zeros_like(obuf)
        for s in range(2):
            junk = pltpu.make_async_copy(
                obuf.at[s], out_hbm.at[pl.ds((NTOK + s * TMS) * SUBLANES, TMS * SUBLANES)], ssem.at[s])
            junk.start()
            junk.wait()

    @pl.when(i < nused)
    def _():
        @pl.when(i >= 2)
        def _():
            wait_tile(ssem, slot)
        x = _load_token_major(x_ref, TMS)
        ms = jnp.mean(x * x, axis=-1, keepdims=True)
        hb = (x * lax.rsqrt(ms + EPS) * n2g_ref[...]).astype(BF16)
        logits = _dot(hb, wr_ref[...]) + br_ref[...]
        g, e_lo, e_hi = _bucket_experts(tb_ref[i])
        lane = lax.broadcasted_iota(jnp.int32, (TMS, LANES), 1)
        pick = lambda l: jnp.sum(jnp.where(lane == l, logits, 0.0), axis=-1, keepdims=True)
        lg, l_lo, l_hi = pick(g), pick(8 + e_lo), pick(8 + e_hi)
        gsum = jnp.sum(jnp.where(lane < N_GROUPS, jnp.exp(logits - lg), 0.0), axis=-1, keepdims=True)
        g_w = 1.0 / gsum
        w_lo = g_w / (1.0 + jnp.exp(l_hi - l_lo))
        w_hi = g_w / (1.0 + jnp.exp(l_lo - l_hi))

        def expert(wg_ref, wu_ref, wd_ref, w):
            a = _dot(hb, wg_ref[...])
            hid = a * _sigmoid(a) * _dot(hb, wu_ref[...]) * w
            return _dot(hid.astype(BF16), wd_ref[...])
        y = (x + expert(wg_lo_ref, wu_lo_ref, wd_lo_ref, w_lo)
             + expert(wg_hi_ref, wu_hi_ref, wd_hi_ref, w_hi))
        _store_token_major(obuf.at[slot], y, TMS)

        start_rows(lambda r: scatter_row(r, slot))

    @pl.when(i == nused - 1)
    def _():
        wait_tile(ssem, slot)

        @pl.when(i >= 1)
        def _():
            wait_tile(ssem, 1 - slot)


def _moe_call(tile_bucket, nused, inv, xs_tm, n2g, wr, br, wg, wu, wd):
    def used(i, nu):
        return jnp.minimum(i, nu[0] - 1)

    def w_spec(shape, which):
        def index(i, tb, nu):
            return (_bucket_experts(tb[used(i, nu)])[which], 0, 0)
        return pl.BlockSpec((None,) + shape, index)

    const = lambda shape: pl.BlockSpec(shape, lambda i, tb, nu: (0, 0))
    up, down = (D_MODEL, D_EXPERT), (D_EXPERT, D_MODEL)
    inv3 = inv.reshape(NT_MAX, 1, TMS)
    return pl.pallas_call(
        _moe_kernel,
        grid_spec=pltpu.PrefetchScalarGridSpec(
            num_scalar_prefetch=2,
            grid=(NT_MAX,),
            in_specs=[
                pl.BlockSpec((None, 1, TMS), lambda i, tb, nu: (i, 0, 0), memory_space=pltpu.SMEM),
                pl.BlockSpec((TMS * SUBLANES, LANES), lambda i, tb, nu: (used(i, nu), 0)),
                const((1, D_MODEL)), const((D_MODEL, LANES)), const((1, LANES)),
                w_spec(up, 1), w_spec(up, 1), w_spec(down, 1),
                w_spec(up, 2), w_spec(up, 2), w_spec(down, 2),
            ],
            out_specs=pl.BlockSpec(memory_space=pl.ANY),
            scratch_shapes=[pltpu.VMEM((2, TMS * SUBLANES, LANES), F32), pltpu.SemaphoreType.DMA((2,))],
        ),
        out_shape=jax.ShapeDtypeStruct(((NTOK + JUNK_ROWS) * SUBLANES, LANES), F32),
        compiler_params=pltpu.CompilerParams(
            dimension_semantics=("arbitrary",), vmem_limit_bytes=VMEM_LIMIT, has_side_effects=True),
        name="hier_moe",
    )(tile_bucket, nused, inv3, xs_tm, n2g, wr, br, wg, wu, wd, wg, wu, wd)


def _rope_tables(positions):
    inv = ROPE_THETA ** (-jnp.arange(0, ROT_DIM, 2, dtype=F32) / ROT_DIM)
    ang = positions.astype(F32)[..., None] * inv
    half = ROT_DIM // 2
    dim = jnp.arange(LANES) % HEAD_DIM
    rotated = dim < ROT_DIM
    expand = ((jnp.arange(half)[:, None] == dim[None, :] % half) & rotated[None, :]).astype(F32)
    spread = lambda t: jnp.dot(t.reshape(-1, half), expand,
                               precision=lax.Precision.HIGHEST).reshape(t.shape[:-1] + (LANES,))
    return spread(jnp.cos(ang)) + (~rotated).astype(F32), spread(jnp.sin(ang))


def kernel(x, positions, norm1_g, w_in, sgu_ln_g, sgu_ln_b, sgu_w, sgu_b, q_norm_g, k_norm_g, sinks, conv3_w, conv31_w, conv31_b, cnorm_g, cnorm_b, w_branch, w_gate, b_gate, w_o, norm2_g, w_group, b_group, w_expert, b_expert, w_e_gate, w_e_up, w_e_down):
    bsz, seq, d = x.shape
    assert (bsz, seq, d) == (BATCH, SEQ, D_MODEL)
    ntok = bsz * seq
    depth = norm1_g.shape[0]
    cos_t, sin_t = _rope_tables(positions)
    row = lambda t: t.reshape(1, -1)
    for l in range(depth):
        sgu_wcat = jnp.transpose(sgu_w[l], (1, 0, 2)).reshape(SGU_CHUNK, SGU_GROUPS * SGU_CHUNK)
        sgu_bias = jnp.repeat(sgu_b[l].T, MIX_W // SGU_GROUPS, axis=1)
        c3w = jnp.zeros((SUBLANES, MIX_W), F32).at[:SHORT_CONV].set(conv3_w[l])
        c31w = jnp.zeros((C31_HALO, MIX_W), F32).at[:CONFORMER_CONV].set(conv31_w[l])
        wr = jnp.zeros((d, LANES), F32).at[:, 0:N_GROUPS].set(w_group[l]).at[:, 8:8 + N_EXPERTS].set(w_expert[l])
        wr = wr.astype(BF16)
        br = jnp.zeros((1, LANES), F32).at[0, 0:N_GROUPS].set(b_group[l]).at[0, 8:8 + N_EXPERTS].set(b_expert[l])
        inv, xs, tile_bucket, meta = _mixing_call(
            x, cos_t, sin_t, row(norm1_g[l]), w_in[l].astype(BF16),
            row(sgu_ln_g[l]), row(sgu_ln_b[l]), sgu_wcat, sgu_bias,
            row(jnp.tile(q_norm_g[l], N_Q_HEADS)), row(jnp.tile(k_norm_g[l], N_KV_HEADS)), sinks[l],
            c3w, c31w, row(conv31_b[l]), row(cnorm_g[l]), row(cnorm_b[l]),
            w_branch[l].astype(BF16), (0.5 * w_gate[l]).astype(BF16), row(0.5 * b_gate[l]), w_o[l].astype(BF16),
            row(norm2_g[l]), wr, br)
        x = _moe_call(tile_bucket[0], meta[0, 0:1], inv, xs, row(norm2_g[l]), wr, br,
                      w_e_gate[l].astype(BF16), w_e_up[l].astype(BF16), w_e_down[l].astype(BF16))
    return _to_rows_call(x, ntok).reshape(bsz, seq, d)
```

```python
import functools

import jax
import jax.numpy as jnp
from jax import lax
from jax.experimental import pallas as pl
from jax.experimental.pallas import tpu as pltpu

D_MODEL = 1024
N_BRANCH = 4
MIX_W = 256
SGU_GROUPS = 4
SGU_CHUNK = 128
N_Q_HEADS = 4
N_KV_HEADS = 2
HEAD_DIM = 64
WINDOW = 128
ROT_DIM = HEAD_DIM // 4
ROPE_THETA = 500000.0
SHORT_CONV = 3
CONFORMER_CONV = 31
N_GROUPS = 4
EXPERTS_PER_GROUP = 4
N_EXPERTS = N_GROUPS * EXPERTS_PER_GROUP
D_EXPERT = 256
EPS = 1e-6
Q_W = N_Q_HEADS * HEAD_DIM
KV_W = N_KV_HEADS * HEAD_DIM
IN_A = 0
IN_B = IN_A + 2 * MIX_W
IN_C = IN_B + (N_Q_HEADS + 2 * N_KV_HEADS) * HEAD_DIM
IN_D = IN_C + 3 * MIX_W
D_IN = IN_D + 2 * MIX_W

BATCH = 8
SEQ = 4096
PAIRS_PER_GROUP = EXPERTS_PER_GROUP * (EXPERTS_PER_GROUP - 1) // 2
N_BUCKETS = N_GROUPS * PAIRS_PER_GROUP
NB_PAD = 32

LANES = 128
SUBLANES = 8
TM = 512
TMS = 256
NTOK = BATCH * SEQ
N_TILES = NTOK // TM
TILES_PER_SEQ = SEQ // TM
NT_MAX = NTOK // TMS + N_BUCKETS
JUNK_ROWS = 2 * TMS
TB_LEN = 256
N_MIX_IN = 24
NEW_TILES_MAX = TM // TMS + 1
ISSUE_UNROLL = 8
WAIT_ROWS = 128
C3_HALO = SUBLANES
C31_HALO = 32
CONV_ROWS = 64
VMEM_LIMIT = 56 * 1024 * 1024
NEG = -1e30

F32 = jnp.float32
BF16 = jnp.bfloat16


def _dot(a, b):
    return jnp.dot(a, b, preferred_element_type=F32)


def _gelu_tanh(x):
    c = 0.7978845608028654
    return 0.5 * x * (1.0 + jnp.tanh(c * (x + 0.044715 * (x * x * x))))


def _sigmoid(x):
    return 0.5 * jnp.tanh(0.5 * x) + 0.5


def _layernorm(x, g, b):
    mu = jnp.mean(x, axis=-1, keepdims=True)
    xc = x - mu
    var = jnp.mean(xc * xc, axis=-1, keepdims=True)
    return xc * lax.rsqrt(var + EPS) * g + b


def _head_meansq(t, width):
    r = lax.broadcasted_iota(jnp.int32, (width, width), 0) // HEAD_DIM
    c = lax.broadcasted_iota(jnp.int32, (width, width), 1) // HEAD_DIM
    bd = jnp.where(r == c, 1.0 / HEAD_DIM, 0.0).astype(BF16)
    t2 = t * t
    hi = t2.astype(BF16)
    lo = (t2 - hi.astype(F32)).astype(BF16)
    return _dot(hi, bd) + _dot(lo, bd)


def _rope(t, c, s1, s2):
    w = t.shape[-1]
    half = ROT_DIM // 2
    return t * c + pltpu.roll(t, w - half, axis=1) * s1 + pltpu.roll(t, half, axis=1) * s2


def _mixing_step(token_major_in, j, x_ref, cos_ref, sin_ref, n1g_ref, win_ref,
                 sgu_g_ref, sgu_b_ref, sgu_w_ref, sgu_bias_ref,
                 qg_ref, kg_ref, sinks_ref,
                 c3w_ref, c31w_ref, c31b_ref, cng_ref, cnb_ref,
                 wbr_ref, wgate_ref, bgate_ref, wo_ref,
                 n2g_ref, wr_ref, br_ref,
                 out_ref, slots_ref, alloc_state, alloc_valid,
                 kprev_ref, vprev_ref, c3buf_ref, c31buf_ref,
                 ys_ref, hb_ref, xk_ref, ys_in, hb_in, xk_in):
    x = _load_token_major(x_ref, TM) if token_major_in else x_ref[...]
    xk_ref[...] = x
    ms = jnp.mean(x * x, axis=-1, keepdims=True)
    hb = (x * lax.rsqrt(ms + EPS) * n1g_ref[...]).astype(BF16)
    hb_ref[...] = hb

    uv = _dot(hb, win_ref[:, IN_A:IN_B])
    u = _gelu_tanh(uv[:, 0:MIX_W])
    v = _layernorm(_gelu_tanh(uv[:, MIX_W:2 * MIX_W]), sgu_g_ref[...], sgu_b_ref[...])
    tt = lax.broadcasted_iota(jnp.int32, (SGU_CHUNK, SGU_GROUPS * SGU_CHUNK), 0)
    ss = lax.broadcasted_iota(jnp.int32, (SGU_CHUNK, SGU_GROUPS * SGU_CHUNK), 1) % SGU_CHUNK
    wcat = jnp.where(ss <= tt, sgu_w_ref[...], 0.0).astype(BF16)
    lane_grp = lax.broadcasted_iota(jnp.int32, (SGU_CHUNK, MIX_W), 1) // (MIX_W // SGU_GROUPS)
    for c in range(TM // SGU_CHUNK):
        rows = slice(c * SGU_CHUNK, (c + 1) * SGU_CHUNK)
        vc = v[rows]
        vstack = jnp.concatenate(
            [jnp.where(lane_grp == g, vc, 0.0) for g in range(SGU_GROUPS)], axis=0).astype(BF16)
        z = _dot(wcat, vstack) + sgu_bias_ref[...]
        ys_ref[0, rows, :] = (u[rows] * z).astype(BF16)

    qkv = _dot(hb, win_ref[:, IN_B:IN_C])
    rc, sin_t = cos_ref[...], sin_ref[...]
    head_dim_idx = lax.broadcasted_iota(jnp.int32, (1, LANES), 1) % HEAD_DIM
    rs1 = jnp.where(head_dim_idx < ROT_DIM // 2, -sin_t, 0.0)
    rs2 = jnp.where((head_dim_idx >= ROT_DIM // 2) & (head_dim_idx < ROT_DIM), sin_t, 0.0)
    q = qkv[:, 0:Q_W]
    q = q * lax.rsqrt(_head_meansq(q, Q_W) + EPS) * qg_ref[...]
    q = _rope(q, jnp.concatenate([rc, rc], axis=1), jnp.concatenate([rs1, rs1], axis=1),
              jnp.concatenate([rs2, rs2], axis=1)) * (HEAD_DIM ** -0.5)
    k = qkv[:, Q_W:Q_W + KV_W]
    k = k * lax.rsqrt(_head_meansq(k, KV_W) + EPS) * kg_ref[...]
    k = _rope(k, rc, rs1, rs2)
    vv = qkv[:, Q_W + KV_W:Q_W + 2 * KV_W]
    kfull = jnp.concatenate([kprev_ref[...], k], axis=0)
    vfull = jnp.concatenate([vprev_ref[...], vv], axis=0)
    kprev_ref[...] = k[TM - WINDOW:TM]
    vprev_ref[...] = vv[TM - WINDOW:TM]
    low_full = lax.broadcasted_iota(jnp.int32, (WINDOW + TM, LANES), 1) < HEAD_DIM
    krot = pltpu.roll(kfull, HEAD_DIM, axis=1)
    vrot = pltpu.roll(vfull, HEAD_DIM, axis=1)
    kdup = [jnp.where(low_full, kfull, krot).astype(BF16), jnp.where(low_full, krot, kfull).astype(BF16)]
    vdup = [jnp.where(low_full, vfull, vrot).astype(BF16), jnp.where(low_full, vrot, vfull).astype(BF16)]
    low = lax.broadcasted_iota(jnp.int32, (WINDOW, LANES), 1) < HEAD_DIM
    row2 = lax.broadcasted_iota(jnp.int32, (2 * WINDOW, 2 * WINDOW), 0)
    qi = row2 % WINDOW
    ki = lax.broadcasted_iota(jnp.int32, (2 * WINDOW, 2 * WINDOW), 1)
    in_prev = (ki < WINDOW) & (ki > qi)
    in_cur = (ki >= WINDOW) & (ki - WINDOW <= qi)
    is_g0 = lax.broadcasted_iota(jnp.int32, (2 * WINDOW, 1), 0) < WINDOW
    for n in range(TM // WINDOW):
        if n == 0:
            valid = (in_prev & (j > 0)) | in_cur
        else:
            valid = in_prev | in_cur
        cols = []
        for h in range(N_KV_HEADS):
            qcol = q[n * WINDOW:(n + 1) * WINDOW, h * LANES:(h + 1) * LANES]
            qs = jnp.concatenate([jnp.where(low, qcol, 0.0), jnp.where(low, 0.0, qcol)], axis=0).astype(BF16)
            kk = kdup[h][n * WINDOW:(n + 2) * WINDOW]
            sc = lax.dot_general(qs, kk, (((1,), (1,)), ((), ())), preferred_element_type=F32)
            sc = jnp.where(valid, sc, NEG)
            sink = jnp.where(is_g0, sinks_ref[2 * h], sinks_ref[2 * h + 1])
            m = jnp.maximum(jnp.max(sc, axis=-1, keepdims=True), sink)
            p = jnp.exp(sc - m)
            denom = jnp.sum(p, axis=-1, keepdims=True) + jnp.exp(sink - m)
            o = _dot(p.astype(BF16), vdup[h][n * WINDOW:(n + 2) * WINDOW]) / denom
            cols.append(jnp.where(low, o[0:WINDOW], o[WINDOW:2 * WINDOW]))
        ys_ref[1, n * WINDOW:(n + 1) * WINDOW, :] = jnp.concatenate(cols, axis=1).astype(BF16)

    cc = _dot(hb, win_ref[:, IN_C:IN_D])
    c3buf_ref[C3_HALO:C3_HALO + TM, :] = cc[:, 0:MIX_W] * cc[:, 2 * MIX_W:3 * MIX_W]
    conv = c3w_ref[0:1, :] * c3buf_ref[C3_HALO - 2:C3_HALO - 2 + TM, :]
    conv = conv + c3w_ref[1:2, :] * c3buf_ref[C3_HALO - 1:C3_HALO - 1 + TM, :]
    conv = conv + c3w_ref[2:3, :] * c3buf_ref[C3_HALO:C3_HALO + TM, :]
    ys_ref[2] = (cc[:, MIX_W:2 * MIX_W] * conv).astype(BF16)
    c3buf_ref[0:C3_HALO, :] = c3buf_ref[TM:TM + C3_HALO, :]

    dd = _dot(hb, win_ref[:, IN_D:D_IN])
    c31buf_ref[C31_HALO:C31_HALO + TM, :] = dd[:, 0:MIX_W] * _sigmoid(dd[:, MIX_W:2 * MIX_W])
    base = C31_HALO - (CONFORMER_CONV - 1)
    for r in range(TM // CONV_ROWS):
        acc = jnp.broadcast_to(c31b_ref[...], (CONV_ROWS, MIX_W))
        for sub in range(SUBLANES):
            taps = [t for t in range(CONFORMER_CONV) if (base + t) % SUBLANES == sub]
            last = (base + taps[-1]) // SUBLANES * SUBLANES
            start = r * CONV_ROWS + sub
            window = c31buf_ref[start:start + last + CONV_ROWS, :]
            part = None
            for tap in taps:
                off = (base + tap) // SUBLANES * SUBLANES
                term = c31w_ref[tap:tap + 1, :] * window[off:off + CONV_ROWS]
                part = term if part is None else part + term
            acc = acc + part
        yn = _layernorm(acc, cng_ref[...], cnb_ref[...])
        ys_ref[3, r * CONV_ROWS:(r + 1) * CONV_ROWS, :] = (yn * _sigmoid(yn)).astype(BF16)

    c31buf_ref[0:C31_HALO, :] = c31buf_ref[TM:TM + C31_HALO, :]

    hb_prev = hb_in[...]
    merged = jnp.zeros((TM, D_MODEL), F32)
    for b in range(N_BRANCH):
        yb = _dot(ys_in[b], wbr_ref[b])
        half_gate = (_dot(hb_prev, wgate_ref[:, b * D_MODEL:(b + 1) * D_MODEL])
                     + bgate_ref[:, b * D_MODEL:(b + 1) * D_MODEL])
        merged = merged + (jnp.tanh(half_gate) * yb + yb)
    xo = xk_in[...] + _dot((0.5 * merged).astype(BF16), wo_ref[...])
    _store_token_major(out_ref, xo, TM)

    ms2 = jnp.mean(xo * xo, axis=-1, keepdims=True)
    h2 = (xo * lax.rsqrt(ms2 + EPS) * n2g_ref[...]).astype(BF16)
    logits = _dot(h2, wr_ref[...]) + br_ref[...]
    slots_ref[...] = _allocate(_route_bucket(logits.T), alloc_valid, *alloc_state)


def _mixing_kernel(token_major_in, *refs):
    ins = refs[:N_MIX_IN]
    inv_ref, xs_hbm, tb_ref, meta_ref = refs[N_MIX_IN:N_MIX_IN + 4]
    (kprev_ref, vprev_ref, c3buf_ref, c31buf_ref, ys_a, ys_b, hb_a, hb_b, xk_a, xk_b, ob_ref,
     cnt_ref, cur_ref, gal_ref, tbacc_ref, upper_ref, slotv_a, slotv_b, slot_smem, statev_ref, state_smem,
     ztile_ref, sem, csem, zsem) = refs[N_MIX_IN + 4:]
    carries = (kprev_ref, vprev_ref, c3buf_ref, c31buf_ref)
    alloc_state = (cnt_ref, cur_ref, gal_ref, tbacc_ref, upper_ref)
    s = pl.program_id(0)
    ob_merge = ob_ref.at[lax.rem(s + 2, 3)]
    ob_issue = ob_ref.at[lax.rem(s + 1, 3)]
    tile = jnp.minimum(s, N_TILES - 1)
    j = lax.rem(tile, TILES_PER_SEQ)

    @pl.when((j == 0) & (s < N_TILES))
    def _():
        kprev_ref[...] = jnp.zeros_like(kprev_ref)
        vprev_ref[...] = jnp.zeros_like(vprev_ref)
        c3buf_ref[0:C3_HALO, :] = jnp.zeros((C3_HALO, MIX_W), F32)
        c31buf_ref[0:C31_HALO, :] = jnp.zeros((C31_HALO, MIX_W), F32)

    @pl.when(s == 0)
    def _():
        ys_b[...] = jnp.zeros_like(ys_b)
        hb_b[...] = jnp.zeros_like(hb_b)
        xk_b[...] = jnp.zeros_like(xk_b)
        for ref in (cnt_ref, cur_ref, gal_ref, tbacc_ref, ztile_ref):
            ref[...] = jnp.zeros_like(ref)
        r_i = lax.broadcasted_iota(jnp.int32, (TM, TM), 0)
        c_i = lax.broadcasted_iota(jnp.int32, (TM, TM), 1)
        upper_ref[...] = jnp.where(r_i <= c_i, 1.0, 0.0).astype(BF16)

    valid = s >= 1

    @pl.when(lax.rem(s, 2) == 0)
    def _():
        _mixing_step(token_major_in, j, *ins, ob_merge, slotv_a, alloc_state, valid, *carries,
                     ys_a, hb_a, xk_a, ys_b, hb_b, xk_b)

    @pl.when(lax.rem(s, 2) == 1)
    def _():
        _mixing_step(token_major_in, j, *ins, ob_merge, slotv_b, alloc_state, valid, *carries,
                     ys_b, hb_b, xk_b, ys_a, hb_a, xk_a)

    def slots_to_smem(parity):
        return pltpu.make_async_copy(slotv_b if parity else slotv_a, slot_smem.at[parity], csem)

    def wait_rows():
        for _ in range(TM // WAIT_ROWS):
            pltpu.make_async_copy(ob_ref.at[0, pl.ds(0, WAIT_ROWS * SUBLANES)],
                                  xs_hbm.at[pl.ds(0, WAIT_ROWS * SUBLANES)], sem).wait()

    def scatter_rows(src, slots, first_token):
        def place(i, carry):
            slot = slots[0, i]
            inv_ref[slot] = first_token + i
            pltpu.make_async_copy(src.at[pl.ds(i * SUBLANES, SUBLANES)],
                                  xs_hbm.at[pl.ds(slot * SUBLANES, SUBLANES)], sem).start()
            return carry
        lax.fori_loop(0, TM, place, 0, unroll=ISSUE_UNROLL)

    for parity in (0, 1):
        @pl.when((s >= 2) & (lax.rem(s, 2) != parity))
        def _():
            slots_to_smem(parity).wait()
    for parity in (0, 1):
        @pl.when((s >= 1) & (lax.rem(s, 2) == parity))
        def _():
            slots_to_smem(parity).start()

    @pl.when(s >= 3)
    def _():
        wait_rows()

    @pl.when(s >= 2)
    def _():
        scatter_rows(ob_issue, slot_smem.at[lax.rem(s + 1, 2)], (s - 2) * TM)

    @pl.when(s == N_TILES)
    def _():
        last = N_TILES - 1
        slots_to_smem(N_TILES % 2).wait()
        wait_rows()
        scatter_rows(ob_ref.at[last % 3], slot_smem.at[N_TILES % 2], last * TM)
        wait_rows()
        _finish_sorted_layout(cnt_ref, cur_ref, gal_ref, tbacc_ref, statev_ref, state_smem, ztile_ref,
                              inv_ref, xs_hbm, tb_ref, meta_ref, zsem)


def _allocate(bucket, valid, cnt_ref, cur_ref, gal_ref, tbacc_ref, upper_ref):
    rows = lax.broadcasted_iota(jnp.int32, (NB_PAD, TM), 0)
    onehot = jnp.where(rows == bucket, 1.0, 0.0)
    prefix = _dot(onehot.astype(BF16), upper_ref[...])
    total = prefix[:, TM - 1:TM]
    cnt0, cur0, gal0 = cnt_ref[...], cur_ref[...], gal_ref[...]
    tiles_before = jnp.floor((cnt0 + (TMS - 1)) * (1.0 / TMS))
    tiles_after = jnp.floor((cnt0 + total + (TMS - 1)) * (1.0 / TMS))
    n_new = tiles_after - tiles_before
    rr = lax.broadcasted_iota(jnp.int32, (NB_PAD, NB_PAD), 0)
    cc = lax.broadcasted_iota(jnp.int32, (NB_PAD, NB_PAD), 1)
    strict_lower = jnp.where(cc < rr, 1.0, 0.0).astype(BF16)
    n_new_b = jnp.broadcast_to(n_new, (NB_PAD, LANES))
    first_new = gal0 + _dot(strict_lower, n_new_b.astype(BF16))[:, 0:1]
    rank = cnt0 + prefix - 1.0
    ordinal = jnp.floor(rank * (1.0 / TMS))
    fresh = ordinal - tiles_before
    tile_id = jnp.where(fresh < 0, cur0, first_new + fresh)
    slot = tile_id * TMS + (rank - ordinal * TMS)
    cnt_ref[...] = jnp.where(valid, cnt0 + total, cnt0)
    cur_ref[...] = jnp.where(valid & (n_new > 0), first_new + n_new - 1.0, cur0)
    gal_ref[...] = jnp.where(valid, gal0 + jnp.sum(n_new_b, axis=0, keepdims=True)[:, 0:1], gal0)
    lane = lax.broadcasted_iota(jnp.int32, (NB_PAD, TB_LEN), 1).astype(F32)
    bucket_id = lax.broadcasted_iota(jnp.int32, (NB_PAD, TB_LEN), 0).astype(F32)
    owner = jnp.zeros((NB_PAD, TB_LEN), F32)
    for extra in range(NEW_TILES_MAX):
        owner = owner + jnp.where((lane == first_new + extra) & (n_new > extra), bucket_id, 0.0)
    tbacc_ref[...] += jnp.where(valid, jnp.sum(owner, axis=0, keepdims=True), 0.0)
    return jnp.sum(onehot * slot, axis=0, keepdims=True).astype(jnp.int32)


def _finish_sorted_layout(cnt_ref, cur_ref, gal_ref, tbacc_ref, statev_ref, state_smem, ztile_ref,
                          inv_ref, xs_hbm, tb_ref, meta_ref, zsem):
    col = lax.broadcasted_iota(jnp.int32, (NB_PAD, LANES), 1)
    state = jnp.where(col == 0, cnt_ref[...], jnp.where(col == 1, cur_ref[...], gal_ref[...]))
    statev_ref[...] = state.astype(jnp.int32)
    meta_ref[...] = jnp.broadcast_to(gal_ref[...], (NB_PAD, LANES)).astype(jnp.int32)
    tb_ref[...] = tbacc_ref[...].astype(jnp.int32)
    to_smem = pltpu.make_async_copy(statev_ref, state_smem, zsem)
    to_smem.start()
    to_smem.wait()
    nused = state_smem[0, 2]

    def fill(slot, carry):
        inv_ref[slot] = NTOK + (slot & (JUNK_ROWS - 1))
        return carry

    def zero_rows(first_slot, n_rows):
        return pltpu.make_async_copy(ztile_ref.at[pl.ds(0, n_rows * SUBLANES)],
                                     xs_hbm.at[pl.ds(first_slot * SUBLANES, n_rows * SUBLANES)], zsem)

    for wait in (False, True):
        for k in range(N_BUCKETS):
            used_rows = state_smem[k, 0] & (TMS - 1)
            n_pad = jnp.where(used_rows > 0, TMS - used_rows, 0)
            slot = state_smem[k, 1] * TMS + (TMS - n_pad)
            if not wait:
                lax.fori_loop(slot, slot + n_pad, fill, 0)
            for bit in reversed(range(TMS.bit_length() - 1)):
                piece = n_pad & (1 << bit)

                @pl.when(piece != 0)
                def _():
                    copy = zero_rows(slot, 1 << bit)
                    copy.wait() if wait else copy.start()
                slot = slot + piece

        def unused(t, carry):
            copy = zero_rows(t * TMS, TMS)
            copy.wait() if wait else copy.start()
            return carry
        lax.fori_loop(nused, NT_MAX, unused, 0)
    lax.fori_loop(nused * TMS, NT_MAX * TMS, fill, 0)


def _const_spec(shape):
    zeros = (0,) * len(shape)
    return pl.BlockSpec(shape, lambda s: zeros, pipeline_mode=pl.Buffered(1))


def _mixing_call(x, cos_t, sin_t, n1g, w_in, sgu_g, sgu_b, sgu_w, sgu_bias, qg, kg, sinks,
                 c3w, c31w, c31b, cng, cnb, wbr, wgate, bgate, wo, n2g, wr, br):
    cur = lambda s: jnp.minimum(s, N_TILES - 1)
    tok = lambda width: pl.BlockSpec(
        (None, TM, width), lambda s: (cur(s) // TILES_PER_SEQ, lax.rem(cur(s), TILES_PER_SEQ), 0))
    token_major_in = x.ndim == 2
    in_specs = [
        pl.BlockSpec((TM * SUBLANES, LANES), lambda s: (cur(s), 0)) if token_major_in else tok(D_MODEL),
        tok(LANES), tok(LANES),
        _const_spec((1, D_MODEL)), _const_spec((D_MODEL, D_IN)),
        _const_spec((1, MIX_W)), _const_spec((1, MIX_W)),
        _const_spec((SGU_CHUNK, SGU_GROUPS * SGU_CHUNK)), _const_spec((SGU_CHUNK, MIX_W)),
        _const_spec((1, N_Q_HEADS * HEAD_DIM)), _const_spec((1, N_KV_HEADS * HEAD_DIM)),
        pl.BlockSpec(memory_space=pltpu.SMEM),
        _const_spec((SUBLANES, MIX_W)), _const_spec((C31_HALO, MIX_W)), _const_spec((1, MIX_W)),
        _const_spec((1, MIX_W)), _const_spec((1, MIX_W)),
        _const_spec((N_BRANCH, MIX_W, D_MODEL)), _const_spec((D_MODEL, N_BRANCH * D_MODEL)),
        _const_spec((1, N_BRANCH * D_MODEL)), _const_spec((D_MODEL, D_MODEL)),
        _const_spec((1, D_MODEL)), _const_spec((D_MODEL, LANES)), _const_spec((1, LANES)),
    ]
    return pl.pallas_call(
        functools.partial(_mixing_kernel, token_major_in),
        grid=(N_TILES + 1,),
        in_specs=in_specs,
        out_specs=[pl.BlockSpec(memory_space=pltpu.SMEM), pl.BlockSpec(memory_space=pl.ANY),
                   pl.BlockSpec((1, TB_LEN), lambda s: (0, 0)), pl.BlockSpec((NB_PAD, LANES), lambda s: (0, 0))],
        out_shape=[jax.ShapeDtypeStruct((NT_MAX * TMS,), jnp.int32),
                   jax.ShapeDtypeStruct((NT_MAX * TMS * SUBLANES, LANES), F32),
                   jax.ShapeDtypeStruct((1, TB_LEN), jnp.int32),
                   jax.ShapeDtypeStruct((NB_PAD, LANES), jnp.int32)],
        scratch_shapes=[
            pltpu.VMEM((WINDOW, LANES), F32), pltpu.VMEM((WINDOW, LANES), F32),
            pltpu.VMEM((C3_HALO + TM, MIX_W), F32), pltpu.VMEM((C31_HALO + TM, MIX_W), F32),
            pltpu.VMEM((N_BRANCH, TM, MIX_W), BF16), pltpu.VMEM((N_BRANCH, TM, MIX_W), BF16),
            pltpu.VMEM((TM, D_MODEL), BF16), pltpu.VMEM((TM, D_MODEL), BF16),
            pltpu.VMEM((TM, D_MODEL), F32), pltpu.VMEM((TM, D_MODEL), F32),
            pltpu.VMEM((3, TM * SUBLANES, LANES), F32),
            pltpu.VMEM((NB_PAD, 1), F32), pltpu.VMEM((NB_PAD, 1), F32), pltpu.VMEM((NB_PAD, 1), F32),
            pltpu.VMEM((1, TB_LEN), F32), pltpu.VMEM((TM, TM), BF16),
            pltpu.VMEM((1, TM), jnp.int32), pltpu.VMEM((1, TM), jnp.int32), pltpu.SMEM((2, 1, TM), jnp.int32),
            pltpu.VMEM((NB_PAD, LANES), jnp.int32), pltpu.SMEM((NB_PAD, LANES), jnp.int32),
            pltpu.VMEM((TMS * SUBLANES, LANES), F32),
            pltpu.SemaphoreType.DMA(()), pltpu.SemaphoreType.DMA(()), pltpu.SemaphoreType.DMA(()),
        ],
        compiler_params=pltpu.CompilerParams(
            dimension_semantics=("arbitrary",), vmem_limit_bytes=VMEM_LIMIT, has_side_effects=True),
        name="mixing_block",
    )(x, cos_t, sin_t, n1g, w_in, sgu_g, sgu_b, sgu_w, sgu_bias, qg, kg, sinks,
      c3w, c31w, c31b, cng, cnb, wbr, wgate, bgate, wo, n2g, wr, br)


def _route_bucket(logits_t):
    g = [logits_t[i:i + 1, :] for i in range(N_GROUPS)]
    gmax, gidx = g[0], jnp.zeros(g[0].shape, jnp.int32)
    for i in range(1, N_GROUPS):
        better = g[i] > gmax
        gmax = jnp.where(better, g[i], gmax)
        gidx = jnp.where(better, i, gidx)
    e = []
    for i in range(EXPERTS_PER_GROUP):
        ei = logits_t[8 + i:9 + i, :]
        for grp in range(1, N_GROUPS):
            row = 8 + grp * EXPERTS_PER_GROUP + i
            ei = jnp.where(gidx == grp, logits_t[row:row + 1, :], ei)
        e.append(ei)
    v1, i1 = e[0], jnp.zeros_like(gidx)
    for i in range(1, EXPERTS_PER_GROUP):
        better = e[i] > v1
        v1 = jnp.where(better, e[i], v1)
        i1 = jnp.where(better, i, i1)
    v2, i2 = jnp.full_like(v1, -jnp.inf), jnp.zeros_like(gidx)
    for i in range(EXPERTS_PER_GROUP):
        better = (e[i] > v2) & (i1 != i)
        v2 = jnp.where(better, e[i], v2)
        i2 = jnp.where(better, i, i2)
    lo, hi = jnp.minimum(i1, i2), jnp.maximum(i1, i2)
    pair = jnp.where(lo == 0, hi - 1, jnp.where(lo == 1, hi + 1, 5))
    return gidx * PAIRS_PER_GROUP + pair


def _bucket_experts(b):
    g = b // PAIRS_PER_GROUP
    pair = b - g * PAIRS_PER_GROUP
    lo = jnp.where(pair < 3, 0, jnp.where(pair < 5, 1, 2))
    hi = jnp.where(pair < 3, pair + 1, jnp.where(pair < 5, pair - 1, 3))
    return g, g * EXPERTS_PER_GROUP + lo, g * EXPERTS_PER_GROUP + hi


def _load_token_major(ref, rows):
    return jnp.concatenate([ref[pl.ds(c, rows, stride=SUBLANES), :] for c in range(SUBLANES)], axis=1)


def _store_token_major(ref, value, rows):
    for c in range(SUBLANES):
        ref[pl.ds(c, rows, stride=SUBLANES), :] = value[:, c * LANES:(c + 1) * LANES]


def _to_rows_kernel(x_ref, out_ref):
    out_ref[...] = _load_token_major(x_ref, TM)


def _to_rows_call(x_tm, ntok):
    return pl.pallas_call(
        _to_rows_kernel,
        grid=(ntok // TM,),
        in_specs=[pl.BlockSpec((TM * SUBLANES, LANES), lambda i: (i, 0))],
        out_specs=pl.BlockSpec((TM, D_MODEL), lambda i: (i, 0)),
        out_shape=jax.ShapeDtypeStruct((ntok, D_MODEL), F32),
        compiler_params=pltpu.CompilerParams(dimension_semantics=("arbitrary",)),
        name="token_major_to_rows",
    )(x_tm)


def _moe_kernel(tb_ref, nused_ref, inv_ref, x_ref, n2g_ref, wr_ref, br_ref,
                wg_lo_ref, wu_lo_ref, wd_lo_ref, wg_hi_ref, wu_hi_ref, wd_hi_ref,
                out_hbm, obuf, ssem):
    i = pl.program_id(0)
    nused = nused_ref[0]
    slot = lax.rem(i, 2)

    def scatter_row(r, s):
        tok = inv_ref[0, r]
        return pltpu.make_async_copy(obuf.at[s, pl.ds(r * SUBLANES, SUBLANES)],
                                     out_hbm.at[pl.ds(tok * SUBLANES, SUBLANES)], ssem.at[s])

    def start_rows(row_copy):
        def body(r, carry):
            row_copy(r).start()
            return carry
        lax.fori_loop(0, TMS, body, 0, unroll=ISSUE_UNROLL)

    def wait_tile(sem, s):
        for _ in range(TMS // WAIT_ROWS):
            pltpu.make_async_copy(obuf.at[s, pl.ds(0, WAIT_ROWS * SUBLANES)],
                                  out_hbm.at[pl.ds(0, WAIT_ROWS * SUBLANES)], sem.at[s]).wait()

    @pl.when(i == 0)
    def _():
        obuf[...] = jnp.zeros_like(obuf)
        for s in range(2):
            junk = pltpu.make_async_copy(
                obuf.at[s], out_hbm.at[pl.ds((NTOK + s * TMS) * SUBLANES, TMS * SUBLANES)], ssem.at[s])
            junk.start()
            junk.wait()

    @pl.when(i < nused)
    def _():
        @pl.when(i >= 2)
        def _():
            wait_tile(ssem, slot)
        x = _load_token_major(x_ref, TMS)
        ms = jnp.mean(x * x, axis=-1, keepdims=True)
        hb = (x * lax.rsqrt(ms + EPS) * n2g_ref[...]).astype(BF16)
        logits = _dot(hb, wr_ref[...]) + br_ref[...]
        g, e_lo, e_hi = _bucket_experts(tb_ref[i])
        lane = lax.broadcasted_iota(jnp.int32, (TMS, LANES), 1)
        pick = lambda l: jnp.sum(jnp.where(lane == l, logits, 0.0), axis=-1, keepdims=True)
        lg, l_lo, l_hi = pick(g), pick(8 + e_lo), pick(8 + e_hi)
        gsum = jnp.sum(jnp.where(lane < N_GROUPS, jnp.exp(logits - lg), 0.0), axis=-1, keepdims=True)
        g_w = 1.0 / gsum
        w_lo = g_w / (1.0 + jnp.exp(l_hi - l_lo))
        w_hi = g_w / (1.0 + jnp.exp(l_lo - l_hi))

        def expert(wg_ref, wu_ref, wd_ref, w):
            a = _dot(hb, wg_ref[...])
            hid = a * _sigmoid(a) * _dot(hb, wu_ref[...]) * w
            return _dot(hid.astype(BF16), wd_ref[...])
        y = (x + expert(wg_lo_ref, wu_lo_ref, wd_lo_ref, w_lo)
             + expert(wg_hi_ref, wu_hi_ref, wd_hi_ref, w_hi))
        _store_token_major(obuf.at[slot], y, TMS)

        start_rows(lambda r: scatter_row(r, slot))

    @pl.when(i == nused - 1)
    def _():
        wait_tile(ssem, slot)

        @pl.when(i >= 1)
        def _():
            wait_tile(ssem, 1 - slot)


def _moe_call(tile_bucket, nused, inv, xs_tm, n2g, wr, br, wg, wu, wd):
    def used(i, nu):
        return jnp.minimum(i, nu[0] - 1)

    def w_spec(shape, which):
        def index(i, tb, nu):
            return (_bucket_experts(tb[used(i, nu)])[which], 0, 0)
        return pl.BlockSpec((None,) + shape, index)

    const = lambda shape: pl.BlockSpec(shape, lambda i, tb, nu: (0, 0))
    up, down = (D_MODEL, D_EXPERT), (D_EXPERT, D_MODEL)
    inv3 = inv.reshape(NT_MAX, 1, TMS)
    return pl.pallas_call(
        _moe_kernel,
        grid_spec=pltpu.PrefetchScalarGridSpec(
            num_scalar_prefetch=2,
            grid=(NT_MAX,),
            in_specs=[
                pl.BlockSpec((None, 1, TMS), lambda i, tb, nu: (i, 0, 0), memory_space=pltpu.SMEM),
                pl.BlockSpec((TMS * SUBLANES, LANES), lambda i, tb, nu: (used(i, nu), 0)),
                const((1, D_MODEL)), const((D_MODEL, LANES)), const((1, LANES)),
                w_spec(up, 1), w_spec(up, 1), w_spec(down, 1),
                w_spec(up, 2), w_spec(up, 2), w_spec(down, 2),
            ],
            out_specs=pl.BlockSpec(memory_space=pl.ANY),
            scratch_shapes=[pltpu.VMEM((2, TMS * SUBLANES, LANES), F32), pltpu.SemaphoreType.DMA((2,))],
        ),
        out_shape=jax.ShapeDtypeStruct(((NTOK + JUNK_ROWS) * SUBLANES, LANES), F32),
        compiler_params=pltpu.CompilerParams(
            dimension_semantics=("arbitrary",), vmem_limit_bytes=VMEM_LIMIT, has_side_effects=True),
        name="hier_moe",
    )(tile_bucket, nused, inv3, xs_tm, n2g, wr, br, wg, wu, wd, wg, wu, wd)


def _rope_tables(positions):
    inv = ROPE_THETA ** (-jnp.arange(0, ROT_DIM, 2, dtype=F32) / ROT_DIM)
    ang = positions.astype(F32)[..., None] * inv
    half = ROT_DIM // 2
    dim = jnp.arange(LANES) % HEAD_DIM
    rotated = dim < ROT_DIM
    expand = ((jnp.arange(half)[:, None] == dim[None, :] % half) & rotated[None, :]).astype(F32)
    spread = lambda t: jnp.dot(t.reshape(-1, half), expand,
                               precision=lax.Precision.HIGHEST).reshape(t.shape[:-1] + (LANES,))
    return spread(jnp.cos(ang)) + (~rotated).astype(F32), spread(jnp.sin(ang))


def kernel(x, positions, norm1_g, w_in, sgu_ln_g, sgu_ln_b, sgu_w, sgu_b, q_norm_g, k_norm_g, sinks, conv3_w, conv31_w, conv31_b, cnorm_g, cnorm_b, w_branch, w_gate, b_gate, w_o, norm2_g, w_group, b_group, w_expert, b_expert, w_e_gate, w_e_up, w_e_down):
    bsz, seq, d = x.shape
    assert (bsz, seq, d) == (BATCH, SEQ, D_MODEL)
    ntok = bsz * seq
    depth = norm1_g.shape[0]
    cos_t, sin_t = _rope_tables(positions)
    row = lambda t: t.reshape(1, -1)
    for l in range(depth):
        sgu_wcat = jnp.transpose(sgu_w[l], (1, 0, 2)).reshape(SGU_CHUNK, SGU_GROUPS * SGU_CHUNK)
        sgu_bias = jnp.repeat(sgu_b[l].T, MIX_W // SGU_GROUPS, axis=1)
        c3w = jnp.zeros((SUBLANES, MIX_W), F32).at[:SHORT_CONV].set(conv3_w[l])
        c31w = jnp.zeros((C31_HALO, MIX_W), F32).at[:CONFORMER_CONV].set(conv31_w[l])
        wr = jnp.zeros((d, LANES), F32).at[:, 0:N_GROUPS].set(w_group[l]).at[:, 8:8 + N_EXPERTS].set(w_expert[l])
        wr = wr.astype(BF16)
        br = jnp.zeros((1, LANES), F32).at[0, 0:N_GROUPS].set(b_group[l]).at[0, 8:8 + N_EXPERTS].set(b_expert[l])
        inv, xs, tile_bucket, meta = _mixing_call(
            x, cos_t, sin_t, row(norm1_g[l]), w_in[l].astype(BF16),
            row(sgu_ln_g[l]), row(sgu_ln_b[l]), sgu_wcat, sgu_bias,
            row(jnp.tile(q_norm_g[l], N_Q_HEADS)), row(jnp.tile(k_norm_g[l], N_KV_HEADS)), sinks[l],
            c3w, c31w, row(conv31_b[l]), row(cnorm_g[l]), row(cnorm_b[l]),
            w_branch[l].astype(BF16), (0.5 * w_gate[l]).astype(BF16), row(0.5 * b_gate[l]), w_o[l].astype(BF16),
            row(norm2_g[l]), wr, br)
        x = _moe_call(tile_bucket[0], meta[0, 0:1], inv, xs, row(norm2_g[l]), wr, br,
                      w_e_gate[l].astype(BF16), w_e_up[l].astype(BF16), w_e_down[l].astype(BF16))
    return _to_rows_call(x, ntok).reshape(bsz, seq, d)
```

```python
import functools

import jax
import jax.numpy as jnp
from jax import lax
from jax.experimental import pallas as pl
from jax.experimental.pallas import tpu as pltpu

D_MODEL = 1024
N_BRANCH = 4
MIX_W = 256
SGU_GROUPS = 4
SGU_CHUNK = 128
N_Q_HEADS = 4
N_KV_HEADS = 2
HEAD_DIM = 64
WINDOW = 128
ROT_DIM = HEAD_DIM // 4
ROPE_THETA = 500000.0
SHORT_CONV = 3
CONFORMER_CONV = 31
N_GROUPS = 4
EXPERTS_PER_GROUP = 4
N_EXPERTS = N_GROUPS * EXPERTS_PER_GROUP
D_EXPERT = 256
EPS = 1e-6
Q_W = N_Q_HEADS * HEAD_DIM
KV_W = N_KV_HEADS * HEAD_DIM
IN_A = 0
IN_B = IN_A + 2 * MIX_W
IN_C = IN_B + (N_Q_HEADS + 2 * N_KV_HEADS) * HEAD_DIM
IN_D = IN_C + 3 * MIX_W
D_IN = IN_D + 2 * MIX_W

BATCH = 8
SEQ = 4096
PAIRS_PER_GROUP = EXPERTS_PER_GROUP * (EXPERTS_PER_GROUP - 1) // 2
N_BUCKETS = N_GROUPS * PAIRS_PER_GROUP
NB_PAD = 32

LANES = 128
SUBLANES = 8
TM = 512
TMS = 256
NTOK = BATCH * SEQ
N_TILES = NTOK // TM
TILES_PER_SEQ = SEQ // TM
NT_MAX = NTOK // TMS + N_BUCKETS
JUNK_ROWS = 2 * TMS
TB_LEN = 256
JUNK_SLOT0 = NT_MAX * TMS
N_MIX_IN = 24
NEW_TILES_MAX = TM // TMS + 1
ISSUE_UNROLL = 8
WAIT_ROWS = 128
C3_HALO = SUBLANES
C31_HALO = 32
CONV_ROWS = 64
VMEM_LIMIT = 56 * 1024 * 1024
NEG = -1e30

F32 = jnp.float32
BF16 = jnp.bfloat16


def _dot(a, b):
    return jnp.dot(a, b, preferred_element_type=F32)


def _gelu_tanh(x):
    c = 0.7978845608028654
    return 0.5 * x * (1.0 + jnp.tanh(c * (x + 0.044715 * (x * x * x))))


def _sigmoid(x):
    return 0.5 * jnp.tanh(0.5 * x) + 0.5


def _layernorm(x, g, b):
    mu = jnp.mean(x, axis=-1, keepdims=True)
    xc = x - mu
    var = jnp.mean(xc * xc, axis=-1, keepdims=True)
    return xc * lax.rsqrt(var + EPS) * g + b


def _head_meansq(t, width):
    r = lax.broadcasted_iota(jnp.int32, (width, width), 0) // HEAD_DIM
    c = lax.broadcasted_iota(jnp.int32, (width, width), 1) // HEAD_DIM
    bd = jnp.where(r == c, 1.0 / HEAD_DIM, 0.0).astype(BF16)
    t2 = t * t
    hi = t2.astype(BF16)
    lo = (t2 - hi.astype(F32)).astype(BF16)
    return _dot(hi, bd) + _dot(lo, bd)


def _rope(t, c, s1, s2):
    w = t.shape[-1]
    half = ROT_DIM // 2
    return t * c + pltpu.roll(t, w - half, axis=1) * s1 + pltpu.roll(t, half, axis=1) * s2


def _mixing_step(token_major_in, j, x_ref, cos_ref, sin_ref, n1g_ref, win_ref,
                 sgu_g_ref, sgu_b_ref, sgu_w_ref, sgu_bias_ref,
                 qg_ref, kg_ref, sinks_ref,
                 c3w_ref, c31w_ref, c31b_ref, cng_ref, cnb_ref,
                 wbr_ref, wgate_ref, bgate_ref, wo_ref,
                 n2g_ref, wr_ref, br_ref,
                 out_ref, slots_ref, alloc_state, alloc_valid, write_inverse,
                 kprev_ref, vprev_ref, c3buf_ref, c31buf_ref,
                 ys_ref, hb_ref, xk_ref, ys_in, hb_in, xk_in):
    write_inverse()
    x = _load_token_major(x_ref, TM) if token_major_in else x_ref[...]
    xk_ref[...] = x
    ms = jnp.mean(x * x, axis=-1, keepdims=True)
    hb = (x * lax.rsqrt(ms + EPS) * n1g_ref[...]).astype(BF16)
    hb_ref[...] = hb

    uv = _dot(hb, win_ref[:, IN_A:IN_B])
    u = _gelu_tanh(uv[:, 0:MIX_W])
    v = _layernorm(_gelu_tanh(uv[:, MIX_W:2 * MIX_W]), sgu_g_ref[...], sgu_b_ref[...])
    tt = lax.broadcasted_iota(jnp.int32, (SGU_CHUNK, SGU_GROUPS * SGU_CHUNK), 0)
    ss = lax.broadcasted_iota(jnp.int32, (SGU_CHUNK, SGU_GROUPS * SGU_CHUNK), 1) % SGU_CHUNK
    wcat = jnp.where(ss <= tt, sgu_w_ref[...], 0.0).astype(BF16)
    lane_grp = lax.broadcasted_iota(jnp.int32, (SGU_CHUNK, MIX_W), 1) // (MIX_W // SGU_GROUPS)
    for c in range(TM // SGU_CHUNK):
        rows = slice(c * SGU_CHUNK, (c + 1) * SGU_CHUNK)
        vc = v[rows]
        vstack = jnp.concatenate(
            [jnp.where(lane_grp == g, vc, 0.0) for g in range(SGU_GROUPS)], axis=0).astype(BF16)
        z = _dot(wcat, vstack) + sgu_bias_ref[...]
        ys_ref[0, rows, :] = (u[rows] * z).astype(BF16)

    qkv = _dot(hb, win_ref[:, IN_B:IN_C])
    rc, sin_t = cos_ref[...], sin_ref[...]
    head_dim_idx = lax.broadcasted_iota(jnp.int32, (1, LANES), 1) % HEAD_DIM
    rs1 = jnp.where(head_dim_idx < ROT_DIM // 2, -sin_t, 0.0)
    rs2 = jnp.where((head_dim_idx >= ROT_DIM // 2) & (head_dim_idx < ROT_DIM), sin_t, 0.0)
    q = qkv[:, 0:Q_W]
    q = q * lax.rsqrt(_head_meansq(q, Q_W) + EPS) * qg_ref[...]
    q = _rope(q, jnp.concatenate([rc, rc], axis=1), jnp.concatenate([rs1, rs1], axis=1),
              jnp.concatenate([rs2, rs2], axis=1)) * (HEAD_DIM ** -0.5)
    k = qkv[:, Q_W:Q_W + KV_W]
    k = k * lax.rsqrt(_head_meansq(k, KV_W) + EPS) * kg_ref[...]
    k = _rope(k, rc, rs1, rs2)
    vv = qkv[:, Q_W + KV_W:Q_W + 2 * KV_W]
    kfull = jnp.concatenate([kprev_ref[...], k], axis=0)
    vfull = jnp.concatenate([vprev_ref[...], vv], axis=0)
    kprev_ref[...] = k[TM - WINDOW:TM]
    vprev_ref[...] = vv[TM - WINDOW:TM]
    low_full = lax.broadcasted_iota(jnp.int32, (WINDOW + TM, LANES), 1) < HEAD_DIM
    krot = pltpu.roll(kfull, HEAD_DIM, axis=1)
    vrot = pltpu.roll(vfull, HEAD_DIM, axis=1)
    kdup = [jnp.where(low_full, kfull, krot).astype(BF16), jnp.where(low_full, krot, kfull).astype(BF16)]
    vdup = [jnp.where(low_full, vfull, vrot).astype(BF16), jnp.where(low_full, vrot, vfull).astype(BF16)]
    low = lax.broadcasted_iota(jnp.int32, (WINDOW, LANES), 1) < HEAD_DIM
    row2 = lax.broadcasted_iota(jnp.int32, (2 * WINDOW, 2 * WINDOW), 0)
    qi = row2 % WINDOW
    ki = lax.broadcasted_iota(jnp.int32, (2 * WINDOW, 2 * WINDOW), 1)
    in_prev = (ki < WINDOW) & (ki > qi)
    in_cur = (ki >= WINDOW) & (ki - WINDOW <= qi)
    is_g0 = lax.broadcasted_iota(jnp.int32, (2 * WINDOW, 1), 0) < WINDOW
    for n in range(TM // WINDOW):
        if n == 0:
            valid = (in_prev & (j > 0)) | in_cur
        else:
            valid = in_prev | in_cur
        cols = []
        for h in range(N_KV_HEADS):
            qcol = q[n * WINDOW:(n + 1) * WINDOW, h * LANES:(h + 1) * LANES]
            qs = jnp.concatenate([jnp.where(low, qcol, 0.0), jnp.where(low, 0.0, qcol)], axis=0).astype(BF16)
            kk = kdup[h][n * WINDOW:(n + 2) * WINDOW]
            sc = lax.dot_general(qs, kk, (((1,), (1,)), ((), ())), preferred_element_type=F32)
            sc = jnp.where(valid, sc, NEG)
            sink = jnp.where(is_g0, sinks_ref[2 * h], sinks_ref[2 * h + 1])
            m = jnp.maximum(jnp.max(sc, axis=-1, keepdims=True), sink)
            p = jnp.exp(sc - m)
            denom = jnp.sum(p, axis=-1, keepdims=True) + jnp.exp(sink - m)
            o = _dot(p.astype(BF16), vdup[h][n * WINDOW:(n + 2) * WINDOW]) / denom
            cols.append(jnp.where(low, o[0:WINDOW], o[WINDOW:2 * WINDOW]))
        ys_ref[1, n * WINDOW:(n + 1) * WINDOW, :] = jnp.concatenate(cols, axis=1).astype(BF16)

    cc = _dot(hb, win_ref[:, IN_C:IN_D])
    c3buf_ref[C3_HALO:C3_HALO + TM, :] = cc[:, 0:MIX_W] * cc[:, 2 * MIX_W:3 * MIX_W]
    conv = c3w_ref[0:1, :] * c3buf_ref[C3_HALO - 2:C3_HALO - 2 + TM, :]
    conv = conv + c3w_ref[1:2, :] * c3buf_ref[C3_HALO - 1:C3_HALO - 1 + TM, :]
    conv = conv + c3w_ref[2:3, :] * c3buf_ref[C3_HALO:C3_HALO + TM, :]
    ys_ref[2] = (cc[:, MIX_W:2 * MIX_W] * conv).astype(BF16)
    c3buf_ref[0:C3_HALO, :] = c3buf_ref[TM:TM + C3_HALO, :]

    dd = _dot(hb, win_ref[:, IN_D:D_IN])
    c31buf_ref[C31_HALO:C31_HALO + TM, :] = dd[:, 0:MIX_W] * _sigmoid(dd[:, MIX_W:2 * MIX_W])
    base = C31_HALO - (CONFORMER_CONV - 1)
    for r in range(TM // CONV_ROWS):
        acc = jnp.broadcast_to(c31b_ref[...], (CONV_ROWS, MIX_W))
        for sub in range(SUBLANES):
            taps = [t for t in range(CONFORMER_CONV) if (base + t) % SUBLANES == sub]
            last = (base + taps[-1]) // SUBLANES * SUBLANES
            start = r * CONV_ROWS + sub
            window = c31buf_ref[start:start + last + CONV_ROWS, :]
            part = None
            for tap in taps:
                off = (base + tap) // SUBLANES * SUBLANES
                term = c31w_ref[tap:tap + 1, :] * window[off:off + CONV_ROWS]
                part = term if part is None else part + term
            acc = acc + part
        yn = _layernorm(acc, cng_ref[...], cnb_ref[...])
        ys_ref[3, r * CONV_ROWS:(r + 1) * CONV_ROWS, :] = (yn * _sigmoid(yn)).astype(BF16)

    c31buf_ref[0:C31_HALO, :] = c31buf_ref[TM:TM + C31_HALO, :]

    hb_prev = hb_in[...]
    merged = jnp.zeros((TM, D_MODEL), F32)
    for b in range(N_BRANCH):
        yb = _dot(ys_in[b], wbr_ref[b])
        half_gate = (_dot(hb_prev, wgate_ref[:, b * D_MODEL:(b + 1) * D_MODEL])
                     + bgate_ref[:, b * D_MODEL:(b + 1) * D_MODEL])
        merged = merged + (jnp.tanh(half_gate) * yb + yb)
    xo = xk_in[...] + _dot((0.5 * merged).astype(BF16), wo_ref[...])
    _store_token_major(out_ref, xo, TM)

    ms2 = jnp.mean(xo * xo, axis=-1, keepdims=True)
    h2 = (xo * lax.rsqrt(ms2 + EPS) * n2g_ref[...]).astype(BF16)
    logits = _dot(h2, wr_ref[...]) + br_ref[...]
    slots_ref[...] = _allocate(_route_bucket(logits.T), alloc_valid, *alloc_state)


def _mixing_kernel(token_major_in, *refs):
    ins = refs[:N_MIX_IN]
    inv_ref, xs_hbm, tb_ref, meta_ref = refs[N_MIX_IN:N_MIX_IN + 4]
    (kprev_ref, vprev_ref, c3buf_ref, c31buf_ref, ys_a, ys_b, hb_a, hb_b, xk_a, xk_b, ob_ref,
     cnt_ref, cur_ref, gal_ref, tbacc_ref, upper_ref, slotv_a, slotv_b, slot_smem, statev_ref, state_smem,
     ztile_ref, sem, csem, zsem) = refs[N_MIX_IN + 4:]
    carries = (kprev_ref, vprev_ref, c3buf_ref, c31buf_ref)
    alloc_state = (cnt_ref, cur_ref, gal_ref, tbacc_ref, upper_ref)
    s = pl.program_id(0)
    ob_merge = ob_ref.at[lax.rem(s + 2, 3)]
    ob_issue = ob_ref.at[lax.rem(s + 1, 3)]
    tile = jnp.minimum(s, N_TILES - 1)
    j = lax.rem(tile, TILES_PER_SEQ)

    @pl.when((j == 0) & (s < N_TILES))
    def _():
        kprev_ref[...] = jnp.zeros_like(kprev_ref)
        vprev_ref[...] = jnp.zeros_like(vprev_ref)
        c3buf_ref[0:C3_HALO, :] = jnp.zeros((C3_HALO, MIX_W), F32)
        c31buf_ref[0:C31_HALO, :] = jnp.zeros((C31_HALO, MIX_W), F32)

    @pl.when(s == 0)
    def _():
        ys_b[...] = jnp.zeros_like(ys_b)
        hb_b[...] = jnp.zeros_like(hb_b)
        xk_b[...] = jnp.zeros_like(xk_b)
        for ref in (cnt_ref, cur_ref, gal_ref, tbacc_ref, ztile_ref):
            ref[...] = jnp.zeros_like(ref)
        r_i = lax.broadcasted_iota(jnp.int32, (TM, TM), 0)
        c_i = lax.broadcasted_iota(jnp.int32, (TM, TM), 1)
        upper_ref[...] = jnp.where(r_i <= c_i, 1.0, 0.0).astype(BF16)

        def junk_slots(i, carry):
            slot_smem[0, 0, i] = JUNK_SLOT0 + i
            slot_smem[1, 0, i] = JUNK_SLOT0 + i
            return carry
        lax.fori_loop(0, TM, junk_slots, 0)

    valid = s >= 1
    slots_old = slot_smem.at[lax.rem(s, 2)]

    def write_inverse():
        for i in range(TM):
            inv_ref[slots_old[0, i]] = (s - 3) * TM + i

    @pl.when(lax.rem(s, 2) == 0)
    def _():
        _mixing_step(token_major_in, j, *ins, ob_merge, slotv_a, alloc_state, valid, write_inverse, *carries,
                     ys_a, hb_a, xk_a, ys_b, hb_b, xk_b)

    @pl.when(lax.rem(s, 2) == 1)
    def _():
        _mixing_step(token_major_in, j, *ins, ob_merge, slotv_b, alloc_state, valid, write_inverse, *carries,
                     ys_b, hb_b, xk_b, ys_a, hb_a, xk_a)

    def slots_to_smem(parity):
        return pltpu.make_async_copy(slotv_b if parity else slotv_a, slot_smem.at[parity], csem)

    def wait_rows():
        for _ in range(TM // WAIT_ROWS):
            pltpu.make_async_copy(ob_ref.at[0, pl.ds(0, WAIT_ROWS * SUBLANES)],
                                  xs_hbm.at[pl.ds(0, WAIT_ROWS * SUBLANES)], sem).wait()

    def scatter_rows(src, slots):
        def place(i, carry):
            slot = slots[0, i]
            pltpu.make_async_copy(src.at[pl.ds(i * SUBLANES, SUBLANES)],
                                  xs_hbm.at[pl.ds(slot * SUBLANES, SUBLANES)], sem).start()
            return carry
        lax.fori_loop(0, TM, place, 0, unroll=ISSUE_UNROLL)

    def invert_rows(slots, first_token):
        def place(i, carry):
            inv_ref[slots[0, i]] = first_token + i
            return carry
        lax.fori_loop(0, TM, place, 0, unroll=ISSUE_UNROLL)

    for parity in (0, 1):
        @pl.when((s >= 2) & (lax.rem(s, 2) != parity))
        def _():
            slots_to_smem(parity).wait()
    for parity in (0, 1):
        @pl.when((s >= 1) & (lax.rem(s, 2) == parity))
        def _():
            slots_to_smem(parity).start()

    @pl.when(s >= 3)
    def _():
        wait_rows()

    @pl.when(s >= 2)
    def _():
        scatter_rows(ob_issue, slot_smem.at[lax.rem(s + 1, 2)])

    @pl.when(s == N_TILES)
    def _():
        last = N_TILES - 1
        slots_to_smem(N_TILES % 2).wait()
        wait_rows()
        scatter_rows(ob_ref.at[last % 3], slot_smem.at[N_TILES % 2])
        invert_rows(slot_smem.at[(N_TILES + 1) % 2], (last - 1) * TM)
        invert_rows(slot_smem.at[N_TILES % 2], last * TM)
        wait_rows()
        _finish_sorted_layout(cnt_ref, cur_ref, gal_ref, tbacc_ref, statev_ref, state_smem, ztile_ref,
                              inv_ref, xs_hbm, tb_ref, meta_ref, zsem)


def _allocate(bucket, valid, cnt_ref, cur_ref, gal_ref, tbacc_ref, upper_ref):
    rows = lax.broadcasted_iota(jnp.int32, (NB_PAD, TM), 0)
    onehot = jnp.where(rows == bucket, 1.0, 0.0)
    prefix = _dot(onehot.astype(BF16), upper_ref[...])
    total = prefix[:, TM - 1:TM]
    cnt0, cur0, gal0 = cnt_ref[...], cur_ref[...], gal_ref[...]
    tiles_before = jnp.floor((cnt0 + (TMS - 1)) * (1.0 / TMS))
    tiles_after = jnp.floor((cnt0 + total + (TMS - 1)) * (1.0 / TMS))
    n_new = tiles_after - tiles_before
    rr = lax.broadcasted_iota(jnp.int32, (NB_PAD, NB_PAD), 0)
    cc = lax.broadcasted_iota(jnp.int32, (NB_PAD, NB_PAD), 1)
    strict_lower = jnp.where(cc < rr, 1.0, 0.0).astype(BF16)
    n_new_b = jnp.broadcast_to(n_new, (NB_PAD, LANES))
    first_new = gal0 + _dot(strict_lower, n_new_b.astype(BF16))[:, 0:1]
    rank = cnt0 + prefix - 1.0
    ordinal = jnp.floor(rank * (1.0 / TMS))
    fresh = ordinal - tiles_before
    tile_id = jnp.where(fresh < 0, cur0, first_new + fresh)
    slot = tile_id * TMS + (rank - ordinal * TMS)
    cnt_ref[...] = jnp.where(valid, cnt0 + total, cnt0)
    cur_ref[...] = jnp.where(valid & (n_new > 0), first_new + n_new - 1.0, cur0)
    gal_ref[...] = jnp.where(valid, gal0 + jnp.sum(n_new_b, axis=0, keepdims=True)[:, 0:1], gal0)
    lane = lax.broadcasted_iota(jnp.int32, (NB_PAD, TB_LEN), 1).astype(F32)
    bucket_id = lax.broadcasted_iota(jnp.int32, (NB_PAD, TB_LEN), 0).astype(F32)
    owner = jnp.zeros((NB_PAD, TB_LEN), F32)
    for extra in range(NEW_TILES_MAX):
        owner = owner + jnp.where((lane == first_new + extra) & (n_new > extra), bucket_id, 0.0)
    tbacc_ref[...] += jnp.where(valid, jnp.sum(owner, axis=0, keepdims=True), 0.0)
    return jnp.sum(onehot * slot, axis=0, keepdims=True).astype(jnp.int32)


def _finish_sorted_layout(cnt_ref, cur_ref, gal_ref, tbacc_ref, statev_ref, state_smem, ztile_ref,
                          inv_ref, xs_hbm, tb_ref, meta_ref, zsem):
    col = lax.broadcasted_iota(jnp.int32, (NB_PAD, LANES), 1)
    state = jnp.where(col == 0, cnt_ref[...], jnp.where(col == 1, cur_ref[...], gal_ref[...]))
    statev_ref[...] = state.astype(jnp.int32)
    meta_ref[...] = jnp.broadcast_to(gal_ref[...], (NB_PAD, LANES)).astype(jnp.int32)
    tb_ref[...] = tbacc_ref[...].astype(jnp.int32)
    to_smem = pltpu.make_async_copy(statev_ref, state_smem, zsem)
    to_smem.start()
    to_smem.wait()
    nused = state_smem[0, 2]

    def fill(slot, carry):
        inv_ref[slot] = NTOK + (slot & (JUNK_ROWS - 1))
        return carry

    def zero_rows(first_slot, n_rows):
        return pltpu.make_async_copy(ztile_ref.at[pl.ds(0, n_rows * SUBLANES)],
                                     xs_hbm.at[pl.ds(first_slot * SUBLANES, n_rows * SUBLANES)], zsem)

    for wait in (False, True):
        for k in range(N_BUCKETS):
            used_rows = state_smem[k, 0] & (TMS - 1)
            n_pad = jnp.where(used_rows > 0, TMS - used_rows, 0)
            slot = state_smem[k, 1] * TMS + (TMS - n_pad)
            if not wait:
                lax.fori_loop(slot, slot + n_pad, fill, 0)
            for bit in reversed(range(TMS.bit_length() - 1)):
                piece = n_pad & (1 << bit)

                @pl.when(piece != 0)
                def _():
                    copy = zero_rows(slot, 1 << bit)
                    copy.wait() if wait else copy.start()
                slot = slot + piece

        def unused(t, carry):
            copy = zero_rows(t * TMS, TMS)
            copy.wait() if wait else copy.start()
            return carry
        lax.fori_loop(nused, NT_MAX, unused, 0)
    lax.fori_loop(nused * TMS, NT_MAX * TMS, fill, 0)


def _const_spec(shape):
    zeros = (0,) * len(shape)
    return pl.BlockSpec(shape, lambda s: zeros, pipeline_mode=pl.Buffered(1))


def _mixing_call(x, cos_t, sin_t, n1g, w_in, sgu_g, sgu_b, sgu_w, sgu_bias, qg, kg, sinks,
                 c3w, c31w, c31b, cng, cnb, wbr, wgate, bgate, wo, n2g, wr, br):
    cur = lambda s: jnp.minimum(s, N_TILES - 1)
    tok = lambda width: pl.BlockSpec(
        (None, TM, width), lambda s: (cur(s) // TILES_PER_SEQ, lax.rem(cur(s), TILES_PER_SEQ), 0))
    token_major_in = x.ndim == 2
    in_specs = [
        pl.BlockSpec((TM * SUBLANES, LANES), lambda s: (cur(s), 0)) if token_major_in else tok(D_MODEL),
        tok(LANES), tok(LANES),
        _const_spec((1, D_MODEL)), _const_spec((D_MODEL, D_IN)),
        _const_spec((1, MIX_W)), _const_spec((1, MIX_W)),
        _const_spec((SGU_CHUNK, SGU_GROUPS * SGU_CHUNK)), _const_spec((SGU_CHUNK, MIX_W)),
        _const_spec((1, N_Q_HEADS * HEAD_DIM)), _const_spec((1, N_KV_HEADS * HEAD_DIM)),
        pl.BlockSpec(memory_space=pltpu.SMEM),
        _const_spec((SUBLANES, MIX_W)), _const_spec((C31_HALO, MIX_W)), _const_spec((1, MIX_W)),
        _const_spec((1, MIX_W)), _const_spec((1, MIX_W)),
        _const_spec((N_BRANCH, MIX_W, D_MODEL)), _const_spec((D_MODEL, N_BRANCH * D_MODEL)),
        _const_spec((1, N_BRANCH * D_MODEL)), _const_spec((D_MODEL, D_MODEL)),
        _const_spec((1, D_MODEL)), _const_spec((D_MODEL, LANES)), _const_spec((1, LANES)),
    ]
    return pl.pallas_call(
        functools.partial(_mixing_kernel, token_major_in),
        grid=(N_TILES + 1,),
        in_specs=in_specs,
        out_specs=[pl.BlockSpec(memory_space=pltpu.SMEM), pl.BlockSpec(memory_space=pl.ANY),
                   pl.BlockSpec((1, TB_LEN), lambda s: (0, 0)), pl.BlockSpec((NB_PAD, LANES), lambda s: (0, 0))],
        out_shape=[jax.ShapeDtypeStruct((JUNK_SLOT0 + TM,), jnp.int32),
                   jax.ShapeDtypeStruct((NT_MAX * TMS * SUBLANES, LANES), F32),
                   jax.ShapeDtypeStruct((1, TB_LEN), jnp.int32),
                   jax.ShapeDtypeStruct((NB_PAD, LANES), jnp.int32)],
        scratch_shapes=[
            pltpu.VMEM((WINDOW, LANES), F32), pltpu.VMEM((WINDOW, LANES), F32),
            pltpu.VMEM((C3_HALO + TM, MIX_W), F32), pltpu.VMEM((C31_HALO + TM, MIX_W), F32),
            pltpu.VMEM((N_BRANCH, TM, MIX_W), BF16), pltpu.VMEM((N_BRANCH, TM, MIX_W), BF16),
            pltpu.VMEM((TM, D_MODEL), BF16), pltpu.VMEM((TM, D_MODEL), BF16),
            pltpu.VMEM((TM, D_MODEL), F32), pltpu.VMEM((TM, D_MODEL), F32),
            pltpu.VMEM((3, TM * SUBLANES, LANES), F32),
            pltpu.VMEM((NB_PAD, 1), F32), pltpu.VMEM((NB_PAD, 1), F32), pltpu.VMEM((NB_PAD, 1), F32),
            pltpu.VMEM((1, TB_LEN), F32), pltpu.VMEM((TM, TM), BF16),
            pltpu.VMEM((1, TM), jnp.int32), pltpu.VMEM((1, TM), jnp.int32), pltpu.SMEM((2, 1, TM), jnp.int32),
            pltpu.VMEM((NB_PAD, LANES), jnp.int32), pltpu.SMEM((NB_PAD, LANES), jnp.int32),
            pltpu.VMEM((TMS * SUBLANES, LANES), F32),
            pltpu.SemaphoreType.DMA(()), pltpu.SemaphoreType.DMA(()), pltpu.SemaphoreType.DMA(()),
        ],
        compiler_params=pltpu.CompilerParams(
            dimension_semantics=("arbitrary",), vmem_limit_bytes=VMEM_LIMIT, has_side_effects=True),
        name="mixing_block",
    )(x, cos_t, sin_t, n1g, w_in, sgu_g, sgu_b, sgu_w, sgu_bias, qg, kg, sinks,
      c3w, c31w, c31b, cng, cnb, wbr, wgate, bgate, wo, n2g, wr, br)


def _route_bucket(logits_t):
    g = [logits_t[i:i + 1, :] for i in range(N_GROUPS)]
    gmax, gidx = g[0], jnp.zeros(g[0].shape, jnp.int32)
    for i in range(1, N_GROUPS):
        better = g[i] > gmax
        gmax = jnp.where(better, g[i], gmax)
        gidx = jnp.where(better, i, gidx)
    e = []
    for i in range(EXPERTS_PER_GROUP):
        ei = logits_t[8 + i:9 + i, :]
        for grp in range(1, N_GROUPS):
            row = 8 + grp * EXPERTS_PER_GROUP + i
            ei = jnp.where(gidx == grp, logits_t[row:row + 1, :], ei)
        e.append(ei)
    v1, i1 = e[0], jnp.zeros_like(gidx)
    for i in range(1, EXPERTS_PER_GROUP):
        better = e[i] > v1
        v1 = jnp.where(better, e[i], v1)
        i1 = jnp.where(better, i, i1)
    v2, i2 = jnp.full_like(v1, -jnp.inf), jnp.zeros_like(gidx)
    for i in range(EXPERTS_PER_GROUP):
        better = (e[i] > v2) & (i1 != i)
        v2 = jnp.where(better, e[i], v2)
        i2 = jnp.where(better, i, i2)
    lo, hi = jnp.minimum(i1, i2), jnp.maximum(i1, i2)
    pair = jnp.where(lo == 0, hi - 1, jnp.where(lo == 1, hi + 1, 5))
    return gidx * PAIRS_PER_GROUP + pair


def _bucket_experts(b):
    g = b // PAIRS_PER_GROUP
    pair = b - g * PAIRS_PER_GROUP
    lo = jnp.where(pair < 3, 0, jnp.where(pair < 5, 1, 2))
    hi = jnp.where(pair < 3, pair + 1, jnp.where(pair < 5, pair - 1, 3))
    return g, g * EXPERTS_PER_GROUP + lo, g * EXPERTS_PER_GROUP + hi


def _load_token_major(ref, rows):
    return jnp.concatenate([ref[pl.ds(c, rows, stride=SUBLANES), :] for c in range(SUBLANES)], axis=1)


def _store_token_major(ref, value, rows):
    for c in range(SUBLANES):
        ref[pl.ds(c, rows, stride=SUBLANES), :] = value[:, c * LANES:(c + 1) * LANES]


def _to_rows_kernel(x_ref, out_ref):
    out_ref[...] = _load_token_major(x_ref, TM)


def _to_rows_call(x_tm, ntok):
    return pl.pallas_call(
        _to_rows_kernel,
        grid=(ntok // TM,),
        in_specs=[pl.BlockSpec((TM * SUBLANES, LANES), lambda i: (i, 0))],
        out_specs=pl.BlockSpec((TM, D_MODEL), lambda i: (i, 0)),
        out_shape=jax.ShapeDtypeStruct((ntok, D_MODEL), F32),
        compiler_params=pltpu.CompilerParams(dimension_semantics=("arbitrary",)),
        name="token_major_to_rows",
    )(x_tm)


def _moe_kernel(tb_ref, nused_ref, inv_ref, x_ref, n2g_ref, wr_ref, br_ref,
                wg_lo_ref, wu_lo_ref, wd_lo_ref, wg_hi_ref, wu_hi_ref, wd_hi_ref,
                out_hbm, obuf, ssem):
    i = pl.program_id(0)
    nused = nused_ref[0]
    slot = lax.rem(i, 2)

    def scatter_row(r, s):
        tok = inv_ref[0, r]
        return pltpu.make_async_copy(obuf.at[s, pl.ds(r * SUBLANES, SUBLANES)],
                                     out_hbm.at[pl.ds(tok * SUBLANES, SUBLANES)], ssem.at[s])

    def start_rows(row_copy):
        def body(r, carry):
            row_copy(r).start()
            return carry
        lax.fori_loop(0, TMS, body, 0, unroll=ISSUE_UNROLL)

    def wait_tile(sem, s):
        for _ in range(TMS // WAIT_ROWS):
            pltpu.make_async_copy(obuf.at[s, pl.ds(0, WAIT_ROWS * SUBLANES)],
                                  out_hbm.at[pl.ds(0, WAIT_ROWS * SUBLANES)], sem.at[s]).wait()

    @pl.when(i == 0)
    def _():
        obuf[...] = jnp.zeros_like(obuf)
        for s in range(2):
            junk = pltpu.make_async_copy(
                obuf.at[s], out_hbm.at[pl.ds((NTOK + s * TMS) * SUBLANES, TMS * SUBLANES)], ssem.at[s])
            junk.start()
            junk.wait()

    @pl.when(i < nused)
    def _():
        @pl.when(i >= 2)
        def _():
            wait_tile(ssem, slot)
        x = _load_token_major(x_ref, TMS)
        ms = jnp.mean(x * x, axis=-1, keepdims=True)
        hb = (x * lax.rsqrt(ms + EPS) * n2g_ref[...]).astype(BF16)
        logits = _dot(hb, wr_ref[...]) + br_ref[...]
        g, e_lo, e_hi = _bucket_experts(tb_ref[i])
        lane = lax.broadcasted_iota(jnp.int32, (TMS, LANES), 1)
        pick = lambda l: jnp.sum(jnp.where(lane == l, logits, 0.0), axis=-1, keepdims=True)
        lg, l_lo, l_hi = pick(g), pick(8 + e_lo), pick(8 + e_hi)
        gsum = jnp.sum(jnp.where(lane < N_GROUPS, jnp.exp(logits - lg), 0.0), axis=-1, keepdims=True)
        g_w = 1.0 / gsum
        w_lo = g_w / (1.0 + jnp.exp(l_hi - l_lo))
        w_hi = g_w / (1.0 + jnp.exp(l_lo - l_hi))

        def expert(wg_ref, wu_ref, wd_ref, w):
            a = _dot(hb, wg_ref[...])
            hid = a * _sigmoid(a) * _dot(hb, wu_ref[...]) * w
            return _dot(hid.astype(BF16), wd_ref[...])
        y = (x + expert(wg_lo_ref, wu_lo_ref, wd_lo_ref, w_lo)
             + expert(wg_hi_ref, wu_hi_ref, wd_hi_ref, w_hi))
        _store_token_major(obuf.at[slot], y, TMS)

        start_rows(lambda r: scatter_row(r, slot))

    @pl.when(i == nused - 1)
    def _():
        wait_tile(ssem, slot)

        @pl.when(i >= 1)
        def _():
            wait_tile(ssem, 1 - slot)


def _moe_call(tile_bucket, nused, inv, xs_tm, n2g, wr, br, wg, wu, wd):
    def used(i, nu):
        return jnp.minimum(i, nu[0] - 1)

    def w_spec(shape, which):
        def index(i, tb, nu):
            return (_bucket_experts(tb[used(i, nu)])[which], 0, 0)
        return pl.BlockSpec((None,) + shape, index)

    const = lambda shape: pl.BlockSpec(shape, lambda i, tb, nu: (0, 0))
    up, down = (D_MODEL, D_EXPERT), (D_EXPERT, D_MODEL)
    inv3 = inv[:JUNK_SLOT0].reshape(NT_MAX, 1, TMS)
    return pl.pallas_call(
        _moe_kernel,
        grid_spec=pltpu.PrefetchScalarGridSpec(
            num_scalar_prefetch=2,
            grid=(NT_MAX,),
            in_specs=[
                pl.BlockSpec((None, 1, TMS), lambda i, tb, nu: (i, 0, 0), memory_space=pltpu.SMEM),
                pl.BlockSpec((TMS * SUBLANES, LANES), lambda i, tb, nu: (used(i, nu), 0)),
                const((1, D_MODEL)), const((D_MODEL, LANES)), const((1, LANES)),
                w_spec(up, 1), w_spec(up, 1), w_spec(down, 1),
                w_spec(up, 2), w_spec(up, 2), w_spec(down, 2),
            ],
            out_specs=pl.BlockSpec(memory_space=pl.ANY),
            scratch_shapes=[pltpu.VMEM((2, TMS * SUBLANES, LANES), F32), pltpu.SemaphoreType.DMA((2,))],
        ),
        out_shape=jax.ShapeDtypeStruct(((NTOK + JUNK_ROWS) * SUBLANES, LANES), F32),
        compiler_params=pltpu.CompilerParams(
            dimension_semantics=("arbitrary",), vmem_limit_bytes=VMEM_LIMIT, has_side_effects=True),
        name="hier_moe",
    )(tile_bucket, nused, inv3, xs_tm, n2g, wr, br, wg, wu, wd, wg, wu, wd)


def _rope_tables(positions):
    inv = ROPE_THETA ** (-jnp.arange(0, ROT_DIM, 2, dtype=F32) / ROT_DIM)
    ang = positions.astype(F32)[..., None] * inv
    half = ROT_DIM // 2
    dim = jnp.arange(LANES) % HEAD_DIM
    rotated = dim < ROT_DIM
    expand = ((jnp.arange(half)[:, None] == dim[None, :] % half) & rotated[None, :]).astype(F32)
    spread = lambda t: jnp.dot(t.reshape(-1, half), expand,
                               precision=lax.Precision.HIGHEST).reshape(t.shape[:-1] + (LANES,))
    return spread(jnp.cos(ang)) + (~rotated).astype(F32), spread(jnp.sin(ang))


def kernel(x, positions, norm1_g, w_in, sgu_ln_g, sgu_ln_b, sgu_w, sgu_b, q_norm_g, k_norm_g, sinks, conv3_w, conv31_w, conv31_b, cnorm_g, cnorm_b, w_branch, w_gate, b_gate, w_o, norm2_g, w_group, b_group, w_expert, b_expert, w_e_gate, w_e_up, w_e_down):
    bsz, seq, d = x.shape
    assert (bsz, seq, d) == (BATCH, SEQ, D_MODEL)
    ntok = bsz * seq
    depth = norm1_g.shape[0]
    cos_t, sin_t = _rope_tables(positions)
    row = lambda t: t.reshape(1, -1)
    for l in range(depth):
        sgu_wcat = jnp.transpose(sgu_w[l], (1, 0, 2)).reshape(SGU_CHUNK, SGU_GROUPS * SGU_CHUNK)
        sgu_bias = jnp.repeat(sgu_b[l].T, MIX_W // SGU_GROUPS, axis=1)
        c3w = jnp.zeros((SUBLANES, MIX_W), F32).at[:SHORT_CONV].set(conv3_w[l])
        c31w = jnp.zeros((C31_HALO, MIX_W), F32).at[:CONFORMER_CONV].set(conv31_w[l])
        wr = jnp.zeros((d, LANES), F32).at[:, 0:N_GROUPS].set(w_group[l]).at[:, 8:8 + N_EXPERTS].set(w_expert[l])
        wr = wr.astype(BF16)
        br = jnp.zeros((1, LANES), F32).at[0, 0:N_GROUPS].set(b_group[l]).at[0, 8:8 + N_EXPERTS].set(b_expert[l])
        inv, xs, tile_bucket, meta = _mixing_call(
            x, cos_t, sin_t, row(norm1_g[l]), w_in[l].astype(BF16),
            row(sgu_ln_g[l]), row(sgu_ln_b[l]), sgu_wcat, sgu_bias,
            row(jnp.tile(q_norm_g[l], N_Q_HEADS)), row(jnp.tile(k_norm_g[l], N_KV_HEADS)), sinks[l],
            c3w, c31w, row(conv31_b[l]), row(cnorm_g[l]), row(cnorm_b[l]),
            w_branch[l].astype(BF16), (0.5 * w_gate[l]).astype(BF16), row(0.5 * b_gate[l]), w_o[l].astype(BF16),
            row(norm2_g[l]), wr, br)
        x = _moe_call(tile_bucket[0], meta[0, 0:1], inv, xs, row(norm2_g[l]), wr, br,
                      w_e_gate[l].astype(BF16), w_e_up[l].astype(BF16), w_e_down[l].astype(BF16))
    return _to_rows_call(x, ntok).reshape(bsz, seq, d)
```

```python
import functools

import jax
import jax.numpy as jnp
from jax import lax
from jax.experimental import pallas as pl
from jax.experimental.pallas import tpu as pltpu

D_MODEL = 1024
N_BRANCH = 4
MIX_W = 256
SGU_GROUPS = 4
SGU_CHUNK = 128
N_Q_HEADS = 4
N_KV_HEADS = 2
HEAD_DIM = 64
WINDOW = 128
ROT_DIM = HEAD_DIM // 4
ROPE_THETA = 500000.0
SHORT_CONV = 3
CONFORMER_CONV = 31
N_GROUPS = 4
EXPERTS_PER_GROUP = 4
N_EXPERTS = N_GROUPS * EXPERTS_PER_GROUP
D_EXPERT = 256
EPS = 1e-6
Q_W = N_Q_HEADS * HEAD_DIM
KV_W = N_KV_HEADS * HEAD_DIM
IN_A = 0
IN_B = IN_A + 2 * MIX_W
IN_C = IN_B + (N_Q_HEADS + 2 * N_KV_HEADS) * HEAD_DIM
IN_D = IN_C + 3 * MIX_W
D_IN = IN_D + 2 * MIX_W

BATCH = 8
SEQ = 4096
PAIRS_PER_GROUP = EXPERTS_PER_GROUP * (EXPERTS_PER_GROUP - 1) // 2
N_BUCKETS = N_GROUPS * PAIRS_PER_GROUP
NB_PAD = 32

LANES = 128
SUBLANES = 8
TM = 512
TMS = 256
NTOK = BATCH * SEQ
N_TILES = NTOK // TM
TILES_PER_SEQ = SEQ // TM
NT_MAX = NTOK // TMS + N_BUCKETS
JUNK_ROWS = 2 * TMS
TB_LEN = 256
JUNK_SLOT0 = NT_MAX * TMS
N_MIX_IN = 24
NEW_TILES_MAX = TM // TMS + 1
OB_PITCH = 12
ISSUE_UNROLL = 8
WAIT_ROWS = 128
C3_HALO = SUBLANES
C31_HALO = 32
CONV_ROWS = 64
VMEM_LIMIT = 56 * 1024 * 1024
NEG = -1e30

F32 = jnp.float32
BF16 = jnp.bfloat16


def _dot(a, b):
    return jnp.dot(a, b, preferred_element_type=F32)


def _gelu_tanh(x):
    c = 0.7978845608028654
    return 0.5 * x * (1.0 + jnp.tanh(c * (x + 0.044715 * (x * x * x))))


def _sigmoid(x):
    return 0.5 * jnp.tanh(0.5 * x) + 0.5


def _layernorm(x, g, b):
    mu = jnp.mean(x, axis=-1, keepdims=True)
    xc = x - mu
    var = jnp.mean(xc * xc, axis=-1, keepdims=True)
    return xc * lax.rsqrt(var + EPS) * g + b


def _head_meansq(t, width):
    r = lax.broadcasted_iota(jnp.int32, (width, width), 0) // HEAD_DIM
    c = lax.broadcasted_iota(jnp.int32, (width, width), 1) // HEAD_DIM
    bd = jnp.where(r == c, 1.0 / HEAD_DIM, 0.0).astype(BF16)
    t2 = t * t
    hi = t2.astype(BF16)
    lo = (t2 - hi.astype(F32)).astype(BF16)
    return _dot(hi, bd) + _dot(lo, bd)


def _rope(t, c, s1, s2):
    w = t.shape[-1]
    half = ROT_DIM // 2
    return t * c + pltpu.roll(t, w - half, axis=1) * s1 + pltpu.roll(t, half, axis=1) * s2


def _mixing_step(token_major_in, j, x_ref, cos_ref, sin_ref, n1g_ref, win_ref,
                 sgu_g_ref, sgu_b_ref, sgu_w_ref, sgu_bias_ref,
                 qg_ref, kg_ref, sinks_ref,
                 c3w_ref, c31w_ref, c31b_ref, cng_ref, cnb_ref,
                 wbr_ref, wgate_ref, bgate_ref, wo_ref,
                 n2g_ref, wr_ref, br_ref,
                 out_ref, slots_ref, alloc_state, alloc_valid, write_inverse,
                 kprev_ref, vprev_ref, c3buf_ref, c31buf_ref,
                 ys_ref, hb_ref, xk_ref, ys_in, hb_in, xk_in):
    write_inverse()
    x = _load_token_major(x_ref, TM) if token_major_in else x_ref[...]
    xk_ref[...] = x
    ms = jnp.mean(x * x, axis=-1, keepdims=True)
    hb = (x * lax.rsqrt(ms + EPS) * n1g_ref[...]).astype(BF16)
    hb_ref[...] = hb

    uv = _dot(hb, win_ref[:, IN_A:IN_B])
    u = _gelu_tanh(uv[:, 0:MIX_W])
    v = _layernorm(_gelu_tanh(uv[:, MIX_W:2 * MIX_W]), sgu_g_ref[...], sgu_b_ref[...])
    tt = lax.broadcasted_iota(jnp.int32, (SGU_CHUNK, SGU_GROUPS * SGU_CHUNK), 0)
    ss = lax.broadcasted_iota(jnp.int32, (SGU_CHUNK, SGU_GROUPS * SGU_CHUNK), 1) % SGU_CHUNK
    wcat = jnp.where(ss <= tt, sgu_w_ref[...], 0.0).astype(BF16)
    lane_grp = lax.broadcasted_iota(jnp.int32, (SGU_CHUNK, MIX_W), 1) // (MIX_W // SGU_GROUPS)
    for c in range(TM // SGU_CHUNK):
        rows = slice(c * SGU_CHUNK, (c + 1) * SGU_CHUNK)
        vc = v[rows]
        vstack = jnp.concatenate(
            [jnp.where(lane_grp == g, vc, 0.0) for g in range(SGU_GROUPS)], axis=0).astype(BF16)
        z = _dot(wcat, vstack) + sgu_bias_ref[...]
        ys_ref[0, rows, :] = (u[rows] * z).astype(BF16)

    qkv = _dot(hb, win_ref[:, IN_B:IN_C])
    rc, sin_t = cos_ref[...], sin_ref[...]
    head_dim_idx = lax.broadcasted_iota(jnp.int32, (1, LANES), 1) % HEAD_DIM
    rs1 = jnp.where(head_dim_idx < ROT_DIM // 2, -sin_t, 0.0)
    rs2 = jnp.where((head_dim_idx >= ROT_DIM // 2) & (head_dim_idx < ROT_DIM), sin_t, 0.0)
    q = qkv[:, 0:Q_W]
    q = q * lax.rsqrt(_head_meansq(q, Q_W) + EPS) * qg_ref[...]
    q = _rope(q, jnp.concatenate([rc, rc], axis=1), jnp.concatenate([rs1, rs1], axis=1),
              jnp.concatenate([rs2, rs2], axis=1)) * (HEAD_DIM ** -0.5)
    k = qkv[:, Q_W:Q_W + KV_W]
    k = k * lax.rsqrt(_head_meansq(k, KV_W) + EPS) * kg_ref[...]
    k = _rope(k, rc, rs1, rs2)
    vv = qkv[:, Q_W + KV_W:Q_W + 2 * KV_W]
    kfull = jnp.concatenate([kprev_ref[...], k], axis=0)
    vfull = jnp.concatenate([vprev_ref[...], vv], axis=0)
    kprev_ref[...] = k[TM - WINDOW:TM]
    vprev_ref[...] = vv[TM - WINDOW:TM]
    low_full = lax.broadcasted_iota(jnp.int32, (WINDOW + TM, LANES), 1) < HEAD_DIM
    krot = pltpu.roll(kfull, HEAD_DIM, axis=1)
    vrot = pltpu.roll(vfull, HEAD_DIM, axis=1)
    kdup = [jnp.where(low_full, kfull, krot).astype(BF16), jnp.where(low_full, krot, kfull).astype(BF16)]
    vdup = [jnp.where(low_full, vfull, vrot).astype(BF16), jnp.where(low_full, vrot, vfull).astype(BF16)]
    low = lax.broadcasted_iota(jnp.int32, (WINDOW, LANES), 1) < HEAD_DIM
    row2 = lax.broadcasted_iota(jnp.int32, (2 * WINDOW, 2 * WINDOW), 0)
    qi = row2 % WINDOW
    ki = lax.broadcasted_iota(jnp.int32, (2 * WINDOW, 2 * WINDOW), 1)
    in_prev = (ki < WINDOW) & (ki > qi)
    in_cur = (ki >= WINDOW) & (ki - WINDOW <= qi)
    is_g0 = lax.broadcasted_iota(jnp.int32, (2 * WINDOW, 1), 0) < WINDOW
    for n in range(TM // WINDOW):
        if n == 0:
            valid = (in_prev & (j > 0)) | in_cur
        else:
            valid = in_prev | in_cur
        cols = []
        for h in range(N_KV_HEADS):
            qcol = q[n * WINDOW:(n + 1) * WINDOW, h * LANES:(h + 1) * LANES]
            qs = jnp.concatenate([jnp.where(low, qcol, 0.0), jnp.where(low, 0.0, qcol)], axis=0).astype(BF16)
            kk = kdup[h][n * WINDOW:(n + 2) * WINDOW]
            sc = lax.dot_general(qs, kk, (((1,), (1,)), ((), ())), preferred_element_type=F32)
            sc = jnp.where(valid, sc, NEG)
            sink = jnp.where(is_g0, sinks_ref[2 * h], sinks_ref[2 * h + 1])
            m = jnp.maximum(jnp.max(sc, axis=-1, keepdims=True), sink)
            p = jnp.exp(sc - m)
            denom = jnp.sum(p, axis=-1, keepdims=True) + jnp.exp(sink - m)
            o = _dot(p.astype(BF16), vdup[h][n * WINDOW:(n + 2) * WINDOW]) / denom
            cols.append(jnp.where(low, o[0:WINDOW], o[WINDOW:2 * WINDOW]))
        ys_ref[1, n * WINDOW:(n + 1) * WINDOW, :] = jnp.concatenate(cols, axis=1).astype(BF16)

    cc = _dot(hb, win_ref[:, IN_C:IN_D])
    c3buf_ref[C3_HALO:C3_HALO + TM, :] = cc[:, 0:MIX_W] * cc[:, 2 * MIX_W:3 * MIX_W]
    conv = c3w_ref[0:1, :] * c3buf_ref[C3_HALO - 2:C3_HALO - 2 + TM, :]
    conv = conv + c3w_ref[1:2, :] * c3buf_ref[C3_HALO - 1:C3_HALO - 1 + TM, :]
    conv = conv + c3w_ref[2:3, :] * c3buf_ref[C3_HALO:C3_HALO + TM, :]
    ys_ref[2] = (cc[:, MIX_W:2 * MIX_W] * conv).astype(BF16)
    c3buf_ref[0:C3_HALO, :] = c3buf_ref[TM:TM + C3_HALO, :]

    dd = _dot(hb, win_ref[:, IN_D:D_IN])
    c31buf_ref[C31_HALO:C31_HALO + TM, :] = dd[:, 0:MIX_W] * _sigmoid(dd[:, MIX_W:2 * MIX_W])
    base = C31_HALO - (CONFORMER_CONV - 1)
    for r in range(TM // CONV_ROWS):
        acc = jnp.broadcast_to(c31b_ref[...], (CONV_ROWS, MIX_W))
        for sub in range(SUBLANES):
            taps = [t for t in range(CONFORMER_CONV) if (base + t) % SUBLANES == sub]
            last = (base + taps[-1]) // SUBLANES * SUBLANES
            start = r * CONV_ROWS + sub
            window = c31buf_ref[start:start + last + CONV_ROWS, :]
            part = None
            for tap in taps:
                off = (base + tap) // SUBLANES * SUBLANES
                term = c31w_ref[tap:tap + 1, :] * window[off:off + CONV_ROWS]
                part = term if part is None else part + term
            acc = acc + part
        yn = _layernorm(acc, cng_ref[...], cnb_ref[...])
        ys_ref[3, r * CONV_ROWS:(r + 1) * CONV_ROWS, :] = (yn * _sigmoid(yn)).astype(BF16)

    c31buf_ref[0:C31_HALO, :] = c31buf_ref[TM:TM + C31_HALO, :]

    hb_prev = hb_in[...]
    merged = jnp.zeros((TM, D_MODEL), F32)
    for b in range(N_BRANCH):
        yb = _dot(ys_in[b], wbr_ref[b])
        half_gate = (_dot(hb_prev, wgate_ref[:, b * D_MODEL:(b + 1) * D_MODEL])
                     + bgate_ref[:, b * D_MODEL:(b + 1) * D_MODEL])
        merged = merged + (jnp.tanh(half_gate) * yb + yb)
    xo = xk_in[...] + _dot((0.5 * merged).astype(BF16), wo_ref[...])
    _store_token_major(out_ref, xo, TM, pitch=OB_PITCH)

    ms2 = jnp.mean(xo * xo, axis=-1, keepdims=True)
    h2 = (xo * lax.rsqrt(ms2 + EPS) * n2g_ref[...]).astype(BF16)
    logits = _dot(h2, wr_ref[...]) + br_ref[...]
    slots_ref[...] = _allocate(_route_bucket(logits.T), alloc_valid, *alloc_state)


def _mixing_kernel(token_major_in, *refs):
    ins = refs[:N_MIX_IN]
    inv_ref, xs_hbm, tb_ref, meta_ref = refs[N_MIX_IN:N_MIX_IN + 4]
    (kprev_ref, vprev_ref, c3buf_ref, c31buf_ref, ys_a, ys_b, hb_a, hb_b, xk_a, xk_b, ob_ref,
     cnt_ref, cur_ref, gal_ref, tbacc_ref, upper_ref, slotv_a, slotv_b, slot_smem, statev_ref, state_smem,
     ztile_ref, sem, csem, zsem) = refs[N_MIX_IN + 4:]
    carries = (kprev_ref, vprev_ref, c3buf_ref, c31buf_ref)
    alloc_state = (cnt_ref, cur_ref, gal_ref, tbacc_ref, upper_ref)
    s = pl.program_id(0)
    ob_merge = ob_ref.at[lax.rem(s + 2, 3)]
    ob_issue = ob_ref.at[lax.rem(s + 1, 3)]
    tile = jnp.minimum(s, N_TILES - 1)
    j = lax.rem(tile, TILES_PER_SEQ)

    @pl.when((j == 0) & (s < N_TILES))
    def _():
        kprev_ref[...] = jnp.zeros_like(kprev_ref)
        vprev_ref[...] = jnp.zeros_like(vprev_ref)
        c3buf_ref[0:C3_HALO, :] = jnp.zeros((C3_HALO, MIX_W), F32)
        c31buf_ref[0:C31_HALO, :] = jnp.zeros((C31_HALO, MIX_W), F32)

    @pl.when(s == 0)
    def _():
        ys_b[...] = jnp.zeros_like(ys_b)
        hb_b[...] = jnp.zeros_like(hb_b)
        xk_b[...] = jnp.zeros_like(xk_b)
        for ref in (cnt_ref, cur_ref, gal_ref, tbacc_ref, ztile_ref):
            ref[...] = jnp.zeros_like(ref)
        r_i = lax.broadcasted_iota(jnp.int32, (TM, TM), 0)
        c_i = lax.broadcasted_iota(jnp.int32, (TM, TM), 1)
        upper_ref[...] = jnp.where(r_i <= c_i, 1.0, 0.0).astype(BF16)

        def junk_slots(i, carry):
            slot_smem[0, 0, i] = JUNK_SLOT0 + i
            slot_smem[1, 0, i] = JUNK_SLOT0 + i
            return carry
        lax.fori_loop(0, TM, junk_slots, 0)

    valid = s >= 1
    slots_old = slot_smem.at[lax.rem(s, 2)]

    def write_inverse():
        for i in range(TM):
            inv_ref[slots_old[0, i]] = (s - 3) * TM + i

    @pl.when(lax.rem(s, 2) == 0)
    def _():
        _mixing_step(token_major_in, j, *ins, ob_merge, slotv_a, alloc_state, valid, write_inverse, *carries,
                     ys_a, hb_a, xk_a, ys_b, hb_b, xk_b)

    @pl.when(lax.rem(s, 2) == 1)
    def _():
        _mixing_step(token_major_in, j, *ins, ob_merge, slotv_b, alloc_state, valid, write_inverse, *carries,
                     ys_b, hb_b, xk_b, ys_a, hb_a, xk_a)

    def slots_to_smem(parity):
        return pltpu.make_async_copy(slotv_b if parity else slotv_a, slot_smem.at[parity], csem)

    def wait_rows():
        for _ in range(TM // WAIT_ROWS):
            pltpu.make_async_copy(ob_ref.at[0, pl.ds(0, WAIT_ROWS * SUBLANES)],
                                  xs_hbm.at[pl.ds(0, WAIT_ROWS * SUBLANES)], sem).wait()

    def scatter_rows(src, slots):
        def place(i, carry):
            slot = slots[0, i]
            pltpu.make_async_copy(src.at[pl.ds(i * OB_PITCH, SUBLANES)],
                                  xs_hbm.at[pl.ds(slot * SUBLANES, SUBLANES)], sem).start()
            return carry
        lax.fori_loop(0, TM, place, 0, unroll=ISSUE_UNROLL)

    def invert_rows(slots, first_token):
        def place(i, carry):
            inv_ref[slots[0, i]] = first_token + i
            return carry
        lax.fori_loop(0, TM, place, 0, unroll=ISSUE_UNROLL)

    for parity in (0, 1):
        @pl.when((s >= 2) & (lax.rem(s, 2) != parity))
        def _():
            slots_to_smem(parity).wait()
    for parity in (0, 1):
        @pl.when((s >= 1) & (lax.rem(s, 2) == parity))
        def _():
            slots_to_smem(parity).start()

    @pl.when(s >= 3)
    def _():
        wait_rows()

    @pl.when(s >= 2)
    def _():
        scatter_rows(ob_issue, slot_smem.at[lax.rem(s + 1, 2)])

    @pl.when(s == N_TILES)
    def _():
        last = N_TILES - 1
        slots_to_smem(N_TILES % 2).wait()
        wait_rows()
        scatter_rows(ob_ref.at[last % 3], slot_smem.at[N_TILES % 2])
        invert_rows(slot_smem.at[(N_TILES + 1) % 2], (last - 1) * TM)
        invert_rows(slot_smem.at[N_TILES % 2], last * TM)
        wait_rows()
        _finish_sorted_layout(cnt_ref, cur_ref, gal_ref, tbacc_ref, statev_ref, state_smem, ztile_ref,
                              inv_ref, xs_hbm, tb_ref, meta_ref, zsem)


def _allocate(bucket, valid, cnt_ref, cur_ref, gal_ref, tbacc_ref, upper_ref):
    rows = lax.broadcasted_iota(jnp.int32, (NB_PAD, TM), 0)
    onehot = jnp.where(rows == bucket, 1.0, 0.0)
    prefix = _dot(onehot.astype(BF16), upper_ref[...])
    total = prefix[:, TM - 1:TM]
    cnt0, cur0, gal0 = cnt_ref[...], cur_ref[...], gal_ref[...]
    tiles_before = jnp.floor((cnt0 + (TMS - 1)) * (1.0 / TMS))
    tiles_after = jnp.floor((cnt0 + total + (TMS - 1)) * (1.0 / TMS))
    n_new = tiles_after - tiles_before
    rr = lax.broadcasted_iota(jnp.int32, (NB_PAD, NB_PAD), 0)
    cc = lax.broadcasted_iota(jnp.int32, (NB_PAD, NB_PAD), 1)
    strict_lower = jnp.where(cc < rr, 1.0, 0.0).astype(BF16)
    n_new_b = jnp.broadcast_to(n_new, (NB_PAD, LANES))
    first_new = gal0 + _dot(strict_lower, n_new_b.astype(BF16))[:, 0:1]
    rank = cnt0 + prefix - 1.0
    ordinal = jnp.floor(rank * (1.0 / TMS))
    fresh = ordinal - tiles_before
    tile_id = jnp.where(fresh < 0, cur0, first_new + fresh)
    slot = tile_id * TMS + (rank - ordinal * TMS)
    cnt_ref[...] = jnp.where(valid, cnt0 + total, cnt0)
    cur_ref[...] = jnp.where(valid & (n_new > 0), first_new + n_new - 1.0, cur0)
    gal_ref[...] = jnp.where(valid, gal0 + jnp.sum(n_new_b, axis=0, keepdims=True)[:, 0:1], gal0)
    lane = lax.broadcasted_iota(jnp.int32, (NB_PAD, TB_LEN), 1).astype(F32)
    bucket_id = lax.broadcasted_iota(jnp.int32, (NB_PAD, TB_LEN), 0).astype(F32)
    owner = jnp.zeros((NB_PAD, TB_LEN), F32)
    for extra in range(NEW_TILES_MAX):
        owner = owner + jnp.where((lane == first_new + extra) & (n_new > extra), bucket_id, 0.0)
    tbacc_ref[...] += jnp.where(valid, jnp.sum(owner, axis=0, keepdims=True), 0.0)
    return jnp.sum(onehot * slot, axis=0, keepdims=True).astype(jnp.int32)


def _finish_sorted_layout(cnt_ref, cur_ref, gal_ref, tbacc_ref, statev_ref, state_smem, ztile_ref,
                          inv_ref, xs_hbm, tb_ref, meta_ref, zsem):
    col = lax.broadcasted_iota(jnp.int32, (NB_PAD, LANES), 1)
    state = jnp.where(col == 0, cnt_ref[...], jnp.where(col == 1, cur_ref[...], gal_ref[...]))
    statev_ref[...] = state.astype(jnp.int32)
    meta_ref[...] = jnp.broadcast_to(gal_ref[...], (NB_PAD, LANES)).astype(jnp.int32)
    tb_ref[...] = tbacc_ref[...].astype(jnp.int32)
    to_smem = pltpu.make_async_copy(statev_ref, state_smem, zsem)
    to_smem.start()
    to_smem.wait()
    nused = state_smem[0, 2]

    def fill(slot, carry):
        inv_ref[slot] = NTOK + (slot & (JUNK_ROWS - 1))
        return carry

    def zero_rows(first_slot, n_rows):
        return pltpu.make_async_copy(ztile_ref.at[pl.ds(0, n_rows * SUBLANES)],
                                     xs_hbm.at[pl.ds(first_slot * SUBLANES, n_rows * SUBLANES)], zsem)

    for wait in (False, True):
        for k in range(N_BUCKETS):
            used_rows = state_smem[k, 0] & (TMS - 1)
            n_pad = jnp.where(used_rows > 0, TMS - used_rows, 0)
            slot = state_smem[k, 1] * TMS + (TMS - n_pad)
            if not wait:
                lax.fori_loop(slot, slot + n_pad, fill, 0)
            for bit in reversed(range(TMS.bit_length() - 1)):
                piece = n_pad & (1 << bit)

                @pl.when(piece != 0)
                def _():
                    copy = zero_rows(slot, 1 << bit)
                    copy.wait() if wait else copy.start()
                slot = slot + piece

        def unused(t, carry):
            copy = zero_rows(t * TMS, TMS)
            copy.wait() if wait else copy.start()
            return carry
        lax.fori_loop(nused, NT_MAX, unused, 0)
    lax.fori_loop(nused * TMS, NT_MAX * TMS, fill, 0)


def _const_spec(shape):
    zeros = (0,) * len(shape)
    return pl.BlockSpec(shape, lambda s: zeros, pipeline_mode=pl.Buffered(1))


def _mixing_call(x, cos_t, sin_t, n1g, w_in, sgu_g, sgu_b, sgu_w, sgu_bias, qg, kg, sinks,
                 c3w, c31w, c31b, cng, cnb, wbr, wgate, bgate, wo, n2g, wr, br):
    cur = lambda s: jnp.minimum(s, N_TILES - 1)
    tok = lambda width: pl.BlockSpec(
        (None, TM, width), lambda s: (cur(s) // TILES_PER_SEQ, lax.rem(cur(s), TILES_PER_SEQ), 0))
    token_major_in = x.ndim == 2
    in_specs = [
        pl.BlockSpec((TM * SUBLANES, LANES), lambda s: (cur(s), 0)) if token_major_in else tok(D_MODEL),
        tok(LANES), tok(LANES),
        _const_spec((1, D_MODEL)), _const_spec((D_MODEL, D_IN)),
        _const_spec((1, MIX_W)), _const_spec((1, MIX_W)),
        _const_spec((SGU_CHUNK, SGU_GROUPS * SGU_CHUNK)), _const_spec((SGU_CHUNK, MIX_W)),
        _const_spec((1, N_Q_HEADS * HEAD_DIM)), _const_spec((1, N_KV_HEADS * HEAD_DIM)),
        pl.BlockSpec(memory_space=pltpu.SMEM),
        _const_spec((SUBLANES, MIX_W)), _const_spec((C31_HALO, MIX_W)), _const_spec((1, MIX_W)),
        _const_spec((1, MIX_W)), _const_spec((1, MIX_W)),
        _const_spec((N_BRANCH, MIX_W, D_MODEL)), _const_spec((D_MODEL, N_BRANCH * D_MODEL)),
        _const_spec((1, N_BRANCH * D_MODEL)), _const_spec((D_MODEL, D_MODEL)),
        _const_spec((1, D_MODEL)), _const_spec((D_MODEL, LANES)), _const_spec((1, LANES)),
    ]
    return pl.pallas_call(
        functools.partial(_mixing_kernel, token_major_in),
        grid=(N_TILES + 1,),
        in_specs=in_specs,
        out_specs=[pl.BlockSpec(memory_space=pltpu.SMEM), pl.BlockSpec(memory_space=pl.ANY),
                   pl.BlockSpec((1, TB_LEN), lambda s: (0, 0)), pl.BlockSpec((NB_PAD, LANES), lambda s: (0, 0))],
        out_shape=[jax.ShapeDtypeStruct((JUNK_SLOT0 + TM,), jnp.int32),
                   jax.ShapeDtypeStruct((NT_MAX * TMS * SUBLANES, LANES), F32),
                   jax.ShapeDtypeStruct((1, TB_LEN), jnp.int32),
                   jax.ShapeDtypeStruct((NB_PAD, LANES), jnp.int32)],
        scratch_shapes=[
            pltpu.VMEM((WINDOW, LANES), F32), pltpu.VMEM((WINDOW, LANES), F32),
            pltpu.VMEM((C3_HALO + TM, MIX_W), F32), pltpu.VMEM((C31_HALO + TM, MIX_W), F32),
            pltpu.VMEM((N_BRANCH, TM, MIX_W), BF16), pltpu.VMEM((N_BRANCH, TM, MIX_W), BF16),
            pltpu.VMEM((TM, D_MODEL), BF16), pltpu.VMEM((TM, D_MODEL), BF16),
            pltpu.VMEM((TM, D_MODEL), F32), pltpu.VMEM((TM, D_MODEL), F32),
            pltpu.VMEM((3, TM * OB_PITCH, LANES), F32),
            pltpu.VMEM((NB_PAD, 1), F32), pltpu.VMEM((NB_PAD, 1), F32), pltpu.VMEM((NB_PAD, 1), F32),
            pltpu.VMEM((1, TB_LEN), F32), pltpu.VMEM((TM, TM), BF16),
            pltpu.VMEM((1, TM), jnp.int32), pltpu.VMEM((1, TM), jnp.int32), pltpu.SMEM((2, 1, TM), jnp.int32),
            pltpu.VMEM((NB_PAD, LANES), jnp.int32), pltpu.SMEM((NB_PAD, LANES), jnp.int32),
            pltpu.VMEM((TMS * SUBLANES, LANES), F32),
            pltpu.SemaphoreType.DMA(()), pltpu.SemaphoreType.DMA(()), pltpu.SemaphoreType.DMA(()),
        ],
        compiler_params=pltpu.CompilerParams(
            dimension_semantics=("arbitrary",), vmem_limit_bytes=VMEM_LIMIT, has_side_effects=True),
        name="mixing_block",
    )(x, cos_t, sin_t, n1g, w_in, sgu_g, sgu_b, sgu_w, sgu_bias, qg, kg, sinks,
      c3w, c31w, c31b, cng, cnb, wbr, wgate, bgate, wo, n2g, wr, br)


def _route_bucket(logits_t):
    g = [logits_t[i:i + 1, :] for i in range(N_GROUPS)]
    gmax, gidx = g[0], jnp.zeros(g[0].shape, jnp.int32)
    for i in range(1, N_GROUPS):
        better = g[i] > gmax
        gmax = jnp.where(better, g[i], gmax)
        gidx = jnp.where(better, i, gidx)
    e = []
    for i in range(EXPERTS_PER_GROUP):
        ei = logits_t[8 + i:9 + i, :]
        for grp in range(1, N_GROUPS):
            row = 8 + grp * EXPERTS_PER_GROUP + i
            ei = jnp.where(gidx == grp, logits_t[row:row + 1, :], ei)
        e.append(ei)
    v1, i1 = e[0], jnp.zeros_like(gidx)
    for i in range(1, EXPERTS_PER_GROUP):
        better = e[i] > v1
        v1 = jnp.where(better, e[i], v1)
        i1 = jnp.where(better, i, i1)
    v2, i2 = jnp.full_like(v1, -jnp.inf), jnp.zeros_like(gidx)
    for i in range(EXPERTS_PER_GROUP):
        better = (e[i] > v2) & (i1 != i)
        v2 = jnp.where(better, e[i], v2)
        i2 = jnp.where(better, i, i2)
    lo, hi = jnp.minimum(i1, i2), jnp.maximum(i1, i2)
    pair = jnp.where(lo == 0, hi - 1, jnp.where(lo == 1, hi + 1, 5))
    return gidx * PAIRS_PER_GROUP + pair


def _bucket_experts(b):
    g = b // PAIRS_PER_GROUP
    pair = b - g * PAIRS_PER_GROUP
    lo = jnp.where(pair < 3, 0, jnp.where(pair < 5, 1, 2))
    hi = jnp.where(pair < 3, pair + 1, jnp.where(pair < 5, pair - 1, 3))
    return g, g * EXPERTS_PER_GROUP + lo, g * EXPERTS_PER_GROUP + hi


def _load_token_major(ref, rows):
    return jnp.concatenate([ref[pl.ds(c, rows, stride=SUBLANES), :] for c in range(SUBLANES)], axis=1)


def _store_token_major(ref, value, rows, pitch=SUBLANES):
    for c in range(SUBLANES):
        ref[pl.ds(c, rows, stride=pitch), :] = value[:, c * LANES:(c + 1) * LANES]


def _to_rows_kernel(x_ref, out_ref):
    out_ref[...] = _load_token_major(x_ref, TM)


def _to_rows_call(x_tm, ntok):
    return pl.pallas_call(
        _to_rows_kernel,
        grid=(ntok // TM,),
        in_specs=[pl.BlockSpec((TM * SUBLANES, LANES), lambda i: (i, 0))],
        out_specs=pl.BlockSpec((TM, D_MODEL), lambda i: (i, 0)),
        out_shape=jax.ShapeDtypeStruct((ntok, D_MODEL), F32),
        compiler_params=pltpu.CompilerParams(dimension_semantics=("arbitrary",)),
        name="token_major_to_rows",
    )(x_tm)


def _moe_kernel(tb_ref, nused_ref, inv_ref, x_ref, n2g_ref, wr_ref, br_ref,
                wg_lo_ref, wu_lo_ref, wd_lo_ref, wg_hi_ref, wu_hi_ref, wd_hi_ref,
                out_hbm, obuf, ssem):
    i = pl.program_id(0)
    nused = nused_ref[0]
    slot = lax.rem(i, 2)

    def scatter_row(r, s):
        tok = inv_ref[0, r]
        return pltpu.make_async_copy(obuf.at[s, pl.ds(r * OB_PITCH, SUBLANES)],
                                     out_hbm.at[pl.ds(tok * SUBLANES, SUBLANES)], ssem.at[s])

    def start_rows(row_copy):
        def body(r, carry):
            row_copy(r).start()
            return carry
        lax.fori_loop(0, TMS, body, 0, unroll=ISSUE_UNROLL)

    def wait_tile(sem, s):
        for _ in range(TMS // WAIT_ROWS):
            pltpu.make_async_copy(obuf.at[s, pl.ds(0, WAIT_ROWS * SUBLANES)],
                                  out_hbm.at[pl.ds(0, WAIT_ROWS * SUBLANES)], sem.at[s]).wait()

    @pl.when(i == 0)
    def _():
        obuf[...] = jnp.zeros_like(obuf)
        for s in range(2):
            junk = pltpu.make_async_copy(
                obuf.at[s, pl.ds(0, TMS * SUBLANES)],
                out_hbm.at[pl.ds((NTOK + s * TMS) * SUBLANES, TMS * SUBLANES)], ssem.at[s])
            junk.start()
            junk.wait()

    @pl.when(i < nused)
    def _():
        @pl.when(i >= 2)
        def _():
            wait_tile(ssem, slot)
        x = _load_token_major(x_ref, TMS)
        ms = jnp.mean(x * x, axis=-1, keepdims=True)
        hb = (x * lax.rsqrt(ms + EPS) * n2g_ref[...]).astype(BF16)
        logits = _dot(hb, wr_ref[...]) + br_ref[...]
        g, e_lo, e_hi = _bucket_experts(tb_ref[i])
        lane = lax.broadcasted_iota(jnp.int32, (TMS, LANES), 1)
        pick = lambda l: jnp.sum(jnp.where(lane == l, logits, 0.0), axis=-1, keepdims=True)
        lg, l_lo, l_hi = pick(g), pick(8 + e_lo), pick(8 + e_hi)
        gsum = jnp.sum(jnp.where(lane < N_GROUPS, jnp.exp(logits - lg), 0.0), axis=-1, keepdims=True)
        g_w = 1.0 / gsum
        w_lo = g_w / (1.0 + jnp.exp(l_hi - l_lo))
        w_hi = g_w / (1.0 + jnp.exp(l_lo - l_hi))

        def expert(wg_ref, wu_ref, wd_ref, w):
            a = _dot(hb, wg_ref[...])
            hid = a * _sigmoid(a) * _dot(hb, wu_ref[...]) * w
            return _dot(hid.astype(BF16), wd_ref[...])
        y = (x + expert(wg_lo_ref, wu_lo_ref, wd_lo_ref, w_lo)
             + expert(wg_hi_ref, wu_hi_ref, wd_hi_ref, w_hi))
        _store_token_major(obuf.at[slot], y, TMS, pitch=OB_PITCH)

        start_rows(lambda r: scatter_row(r, slot))

    @pl.when(i == nused - 1)
    def _():
        wait_tile(ssem, slot)

        @pl.when(i >= 1)
        def _():
            wait_tile(ssem, 1 - slot)


def _moe_call(tile_bucket, nused, inv, xs_tm, n2g, wr, br, wg, wu, wd):
    def used(i, nu):
        return jnp.minimum(i, nu[0] - 1)

    def w_spec(shape, which):
        def index(i, tb, nu):
            return (_bucket_experts(tb[used(i, nu)])[which], 0, 0)
        return pl.BlockSpec((None,) + shape, index)

    const = lambda shape: pl.BlockSpec(shape, lambda i, tb, nu: (0, 0))
    up, down = (D_MODEL, D_EXPERT), (D_EXPERT, D_MODEL)
    inv3 = inv[:JUNK_SLOT0].reshape(NT_MAX, 1, TMS)
    return pl.pallas_call(
        _moe_kernel,
        grid_spec=pltpu.PrefetchScalarGridSpec(
            num_scalar_prefetch=2,
            grid=(NT_MAX,),
            in_specs=[
                pl.BlockSpec((None, 1, TMS), lambda i, tb, nu: (i, 0, 0), memory_space=pltpu.SMEM),
                pl.BlockSpec((TMS * SUBLANES, LANES), lambda i, tb, nu: (used(i, nu), 0)),
                const((1, D_MODEL)), const((D_MODEL, LANES)), const((1, LANES)),
                w_spec(up, 1), w_spec(up, 1), w_spec(down, 1),
                w_spec(up, 2), w_spec(up, 2), w_spec(down, 2),
            ],
            out_specs=pl.BlockSpec(memory_space=pl.ANY),
            scratch_shapes=[pltpu.VMEM((2, TMS * OB_PITCH, LANES), F32), pltpu.SemaphoreType.DMA((2,))],
        ),
        out_shape=jax.ShapeDtypeStruct(((NTOK + JUNK_ROWS) * SUBLANES, LANES), F32),
        compiler_params=pltpu.CompilerParams(
            dimension_semantics=("arbitrary",), vmem_limit_bytes=VMEM_LIMIT, has_side_effects=True),
        name="hier_moe",
    )(tile_bucket, nused, inv3, xs_tm, n2g, wr, br, wg, wu, wd, wg, wu, wd)


def _rope_tables(positions):
    inv = ROPE_THETA ** (-jnp.arange(0, ROT_DIM, 2, dtype=F32) / ROT_DIM)
    ang = positions.astype(F32)[..., None] * inv
    half = ROT_DIM // 2
    dim = jnp.arange(LANES) % HEAD_DIM
    rotated = dim < ROT_DIM
    expand = ((jnp.arange(half)[:, None] == dim[None, :] % half) & rotated[None, :]).astype(F32)
    spread = lambda t: jnp.dot(t.reshape(-1, half), expand,
                               precision=lax.Precision.HIGHEST).reshape(t.shape[:-1] + (LANES,))
    return spread(jnp.cos(ang)) + (~rotated).astype(F32), spread(jnp.sin(ang))


def kernel(x, positions, norm1_g, w_in, sgu_ln_g, sgu_ln_b, sgu_w, sgu_b, q_norm_g, k_norm_g, sinks, conv3_w, conv31_w, conv31_b, cnorm_g, cnorm_b, w_branch, w_gate, b_gate, w_o, norm2_g, w_group, b_group, w_expert, b_expert, w_e_gate, w_e_up, w_e_down):
    bsz, seq, d = x.shape
    assert (bsz, seq, d) == (BATCH, SEQ, D_MODEL)
    ntok = bsz * seq
    depth = norm1_g.shape[0]
    cos_t, sin_t = _rope_tables(positions)
    row = lambda t: t.reshape(1, -1)
    for l in range(depth):
        sgu_wcat = jnp.transpose(sgu_w[l], (1, 0, 2)).reshape(SGU_CHUNK, SGU_GROUPS * SGU_CHUNK)
        sgu_bias = jnp.repeat(sgu_b[l].T, MIX_W // SGU_GROUPS, axis=1)
        c3w = jnp.zeros((SUBLANES, MIX_W), F32).at[:SHORT_CONV].set(conv3_w[l])
        c31w = jnp.zeros((C31_HALO, MIX_W), F32).at[:CONFORMER_CONV].set(conv31_w[l])
        wr = jnp.zeros((d, LANES), F32).at[:, 0:N_GROUPS].set(w_group[l]).at[:, 8:8 + N_EXPERTS].set(w_expert[l])
        wr = wr.astype(BF16)
        br = jnp.zeros((1, LANES), F32).at[0, 0:N_GROUPS].set(b_group[l]).at[0, 8:8 + N_EXPERTS].set(b_expert[l])
        inv, xs, tile_bucket, meta = _mixing_call(
            x, cos_t, sin_t, row(norm1_g[l]), w_in[l].astype(BF16),
            row(sgu_ln_g[l]), row(sgu_ln_b[l]), sgu_wcat, sgu_bias,
            row(jnp.tile(q_norm_g[l], N_Q_HEADS)), row(jnp.tile(k_norm_g[l], N_KV_HEADS)), sinks[l],
            c3w, c31w, row(conv31_b[l]), row(cnorm_g[l]), row(cnorm_b[l]),
            w_branch[l].astype(BF16), (0.5 * w_gate[l]).astype(BF16), row(0.5 * b_gate[l]), w_o[l].astype(BF16),
            row(norm2_g[l]), wr, br)
        x = _moe_call(tile_bucket[0], meta[0, 0:1], inv, xs, row(norm2_g[l]), wr, br,
                      w_e_gate[l].astype(BF16), w_e_up[l].astype(BF16), w_e_down[l].astype(BF16))
    return _to_rows_call(x, ntok).reshape(bsz, seq, d)
```

```python
import functools

import jax
import jax.numpy as jnp
from jax import lax
from jax.experimental import pallas as pl
from jax.experimental.pallas import tpu as pltpu

D_MODEL = 1024
N_BRANCH = 4
MIX_W = 256
SGU_GROUPS = 4
SGU_CHUNK = 128
N_Q_HEADS = 4
N_KV_HEADS = 2
HEAD_DIM = 64
WINDOW = 128
ROT_DIM = HEAD_DIM // 4
ROPE_THETA = 500000.0
SHORT_CONV = 3
CONFORMER_CONV = 31
N_GROUPS = 4
EXPERTS_PER_GROUP = 4
N_EXPERTS = N_GROUPS * EXPERTS_PER_GROUP
D_EXPERT = 256
EPS = 1e-6
Q_W = N_Q_HEADS * HEAD_DIM
KV_W = N_KV_HEADS * HEAD_DIM
IN_A = 0
IN_B = IN_A + 2 * MIX_W
IN_C = IN_B + (N_Q_HEADS + 2 * N_KV_HEADS) * HEAD_DIM
IN_D = IN_C + 3 * MIX_W
D_IN = IN_D + 2 * MIX_W

BATCH = 8
SEQ = 4096
PAIRS_PER_GROUP = EXPERTS_PER_GROUP * (EXPERTS_PER_GROUP - 1) // 2
N_BUCKETS = N_GROUPS * PAIRS_PER_GROUP
NB_PAD = 32

LANES = 128
SUBLANES = 8
TM = 512
TMS = 256
TO_ROWS_TM = 2048
NTOK = BATCH * SEQ
N_TILES = NTOK // TM
TILES_PER_SEQ = SEQ // TM
NT_MAX = NTOK // TMS + N_BUCKETS
JUNK_ROWS = 2 * TMS
TB_LEN = 256
JUNK_SLOT0 = NT_MAX * TMS
N_MIX_IN = 24
NEW_TILES_MAX = TM // TMS + 1
OB_PITCH = 12
ISSUE_UNROLL = 8
WAIT_ROWS = 128
C3_HALO = SUBLANES
C31_HALO = 32
CONV_ROWS = 64
VMEM_LIMIT = 56 * 1024 * 1024
NEG = -1e30

F32 = jnp.float32
BF16 = jnp.bfloat16


def _dot(a, b):
    return jnp.dot(a, b, preferred_element_type=F32)


def _gelu_tanh(x):
    c = 0.7978845608028654
    return 0.5 * x * (1.0 + jnp.tanh(c * (x + 0.044715 * (x * x * x))))


def _sigmoid(x):
    return 0.5 * jnp.tanh(0.5 * x) + 0.5


def _layernorm(x, g, b):
    mu = jnp.mean(x, axis=-1, keepdims=True)
    xc = x - mu
    var = jnp.mean(xc * xc, axis=-1, keepdims=True)
    return xc * lax.rsqrt(var + EPS) * g + b


def _head_meansq(t, width):
    r = lax.broadcasted_iota(jnp.int32, (width, width), 0) // HEAD_DIM
    c = lax.broadcasted_iota(jnp.int32, (width, width), 1) // HEAD_DIM
    bd = jnp.where(r == c, 1.0 / HEAD_DIM, 0.0).astype(BF16)
    t2 = t * t
    hi = t2.astype(BF16)
    lo = (t2 - hi.astype(F32)).astype(BF16)
    return _dot(hi, bd) + _dot(lo, bd)


def _rope(t, c, s1, s2):
    w = t.shape[-1]
    half = ROT_DIM // 2
    return t * c + pltpu.roll(t, w - half, axis=1) * s1 + pltpu.roll(t, half, axis=1) * s2


def _mixing_step(token_major_in, j, x_ref, cos_ref, sin_ref, n1g_ref, win_ref,
                 sgu_g_ref, sgu_b_ref, sgu_w_ref, sgu_bias_ref,
                 qg_ref, kg_ref, sinks_ref,
                 c3w_ref, c31w_ref, c31b_ref, cng_ref, cnb_ref,
                 wbr_ref, wgate_ref, bgate_ref, wo_ref,
                 n2g_ref, wr_ref, br_ref,
                 out_ref, slots_ref, alloc_state, alloc_valid, write_inverse,
                 kprev_ref, vprev_ref, c3buf_ref, c31buf_ref,
                 ys_ref, hb_ref, xk_ref, ys_in, hb_in, xk_in):
    write_inverse()
    x = _load_token_major(x_ref, TM) if token_major_in else x_ref[...]
    xk_ref[...] = x
    ms = jnp.mean(x * x, axis=-1, keepdims=True)
    hb = (x * lax.rsqrt(ms + EPS) * n1g_ref[...]).astype(BF16)
    hb_ref[...] = hb

    uv = _dot(hb, win_ref[:, IN_A:IN_B])
    u = _gelu_tanh(uv[:, 0:MIX_W])
    v = _layernorm(_gelu_tanh(uv[:, MIX_W:2 * MIX_W]), sgu_g_ref[...], sgu_b_ref[...])
    tt = lax.broadcasted_iota(jnp.int32, (SGU_CHUNK, SGU_GROUPS * SGU_CHUNK), 0)
    ss = lax.broadcasted_iota(jnp.int32, (SGU_CHUNK, SGU_GROUPS * SGU_CHUNK), 1) % SGU_CHUNK
    wcat = jnp.where(ss <= tt, sgu_w_ref[...], 0.0).astype(BF16)
    lane_grp = lax.broadcasted_iota(jnp.int32, (SGU_CHUNK, MIX_W), 1) // (MIX_W // SGU_GROUPS)
    for c in range(TM // SGU_CHUNK):
        rows = slice(c * SGU_CHUNK, (c + 1) * SGU_CHUNK)
        vc = v[rows]
        vstack = jnp.concatenate(
            [jnp.where(lane_grp == g, vc, 0.0) for g in range(SGU_GROUPS)], axis=0).astype(BF16)
        z = _dot(wcat, vstack) + sgu_bias_ref[...]
        ys_ref[0, rows, :] = (u[rows] * z).astype(BF16)

    qkv = _dot(hb, win_ref[:, IN_B:IN_C])
    rc, sin_t = cos_ref[...], sin_ref[...]
    head_dim_idx = lax.broadcasted_iota(jnp.int32, (1, LANES), 1) % HEAD_DIM
    rs1 = jnp.where(head_dim_idx < ROT_DIM // 2, -sin_t, 0.0)
    rs2 = jnp.where((head_dim_idx >= ROT_DIM // 2) & (head_dim_idx < ROT_DIM), sin_t, 0.0)
    q = qkv[:, 0:Q_W]
    q = q * lax.rsqrt(_head_meansq(q, Q_W) + EPS) * qg_ref[...]
    q = _rope(q, jnp.concatenate([rc, rc], axis=1), jnp.concatenate([rs1, rs1], axis=1),
              jnp.concatenate([rs2, rs2], axis=1)) * (HEAD_DIM ** -0.5)
    k = qkv[:, Q_W:Q_W + KV_W]
    k = k * lax.rsqrt(_head_meansq(k, KV_W) + EPS) * kg_ref[...]
    k = _rope(k, rc, rs1, rs2)
    vv = qkv[:, Q_W + KV_W:Q_W + 2 * KV_W]
    kfull = jnp.concatenate([kprev_ref[...], k], axis=0)
    vfull = jnp.concatenate([vprev_ref[...], vv], axis=0)
    kprev_ref[...] = k[TM - WINDOW:TM]
    vprev_ref[...] = vv[TM - WINDOW:TM]
    low_full = lax.broadcasted_iota(jnp.int32, (WINDOW + TM, LANES), 1) < HEAD_DIM
    krot = pltpu.roll(kfull, HEAD_DIM, axis=1)
    vrot = pltpu.roll(vfull, HEAD_DIM, axis=1)
    kdup = [jnp.where(low_full, kfull, krot).astype(BF16), jnp.where(low_full, krot, kfull).astype(BF16)]
    vdup = [jnp.where(low_full, vfull, vrot).astype(BF16), jnp.where(low_full, vrot, vfull).astype(BF16)]
    low = lax.broadcasted_iota(jnp.int32, (WINDOW, LANES), 1) < HEAD_DIM
    row2 = lax.broadcasted_iota(jnp.int32, (2 * WINDOW, 2 * WINDOW), 0)
    qi = row2 % WINDOW
    ki = lax.broadcasted_iota(jnp.int32, (2 * WINDOW, 2 * WINDOW), 1)
    in_prev = (ki < WINDOW) & (ki > qi)
    in_cur = (ki >= WINDOW) & (ki - WINDOW <= qi)
    is_g0 = lax.broadcasted_iota(jnp.int32, (2 * WINDOW, 1), 0) < WINDOW
    for n in range(TM // WINDOW):
        if n == 0:
            valid = (in_prev & (j > 0)) | in_cur
        else:
            valid = in_prev | in_cur
        cols = []
        for h in range(N_KV_HEADS):
            qcol = q[n * WINDOW:(n + 1) * WINDOW, h * LANES:(h + 1) * LANES]
            qs = jnp.concatenate([jnp.where(low, qcol, 0.0), jnp.where(low, 0.0, qcol)], axis=0).astype(BF16)
            kk = kdup[h][n * WINDOW:(n + 2) * WINDOW]
            sc = lax.dot_general(qs, kk, (((1,), (1,)), ((), ())), preferred_element_type=F32)
            sc = jnp.where(valid, sc, NEG)
            sink = jnp.where(is_g0, sinks_ref[2 * h], sinks_ref[2 * h + 1])
            m = jnp.maximum(jnp.max(sc, axis=-1, keepdims=True), sink)
            p = jnp.exp(sc - m)
            denom = jnp.sum(p, axis=-1, keepdims=True) + jnp.exp(sink - m)
            o = _dot(p.astype(BF16), vdup[h][n * WINDOW:(n + 2) * WINDOW]) / denom
            cols.append(jnp.where(low, o[0:WINDOW], o[WINDOW:2 * WINDOW]))
        ys_ref[1, n * WINDOW:(n + 1) * WINDOW, :] = jnp.concatenate(cols, axis=1).astype(BF16)

    cc = _dot(hb, win_ref[:, IN_C:IN_D])
    c3buf_ref[C3_HALO:C3_HALO + TM, :] = cc[:, 0:MIX_W] * cc[:, 2 * MIX_W:3 * MIX_W]
    conv = c3w_ref[0:1, :] * c3buf_ref[C3_HALO - 2:C3_HALO - 2 + TM, :]
    conv = conv + c3w_ref[1:2, :] * c3buf_ref[C3_HALO - 1:C3_HALO - 1 + TM, :]
    conv = conv + c3w_ref[2:3, :] * c3buf_ref[C3_HALO:C3_HALO + TM, :]
    ys_ref[2] = (cc[:, MIX_W:2 * MIX_W] * conv).astype(BF16)
    c3buf_ref[0:C3_HALO, :] = c3buf_ref[TM:TM + C3_HALO, :]

    dd = _dot(hb, win_ref[:, IN_D:D_IN])
    c31buf_ref[C31_HALO:C31_HALO + TM, :] = dd[:, 0:MIX_W] * _sigmoid(dd[:, MIX_W:2 * MIX_W])
    base = C31_HALO - (CONFORMER_CONV - 1)
    for r in range(TM // CONV_ROWS):
        acc = jnp.broadcast_to(c31b_ref[...], (CONV_ROWS, MIX_W))
        for sub in range(SUBLANES):
            taps = [t for t in range(CONFORMER_CONV) if (base + t) % SUBLANES == sub]
            last = (base + taps[-1]) // SUBLANES * SUBLANES
            start = r * CONV_ROWS + sub
            window = c31buf_ref[start:start + last + CONV_ROWS, :]
            part = None
            for tap in taps:
                off = (base + tap) // SUBLANES * SUBLANES
                term = c31w_ref[tap:tap + 1, :] * window[off:off + CONV_ROWS]
                part = term if part is None else part + term
            acc = acc + part
        yn = _layernorm(acc, cng_ref[...], cnb_ref[...])
        ys_ref[3, r * CONV_ROWS:(r + 1) * CONV_ROWS, :] = (yn * _sigmoid(yn)).astype(BF16)

    c31buf_ref[0:C31_HALO, :] = c31buf_ref[TM:TM + C31_HALO, :]

    hb_prev = hb_in[...]
    merged = jnp.zeros((TM, D_MODEL), F32)
    for b in range(N_BRANCH):
        yb = _dot(ys_in[b], wbr_ref[b])
        half_gate = (_dot(hb_prev, wgate_ref[:, b * D_MODEL:(b + 1) * D_MODEL])
                     + bgate_ref[:, b * D_MODEL:(b + 1) * D_MODEL])
        merged = merged + (jnp.tanh(half_gate) * yb + yb)
    xo = xk_in[...] + _dot((0.5 * merged).astype(BF16), wo_ref[...])
    _store_token_major(out_ref, xo, TM, pitch=OB_PITCH)

    ms2 = jnp.mean(xo * xo, axis=-1, keepdims=True)
    h2 = (xo * lax.rsqrt(ms2 + EPS) * n2g_ref[...]).astype(BF16)
    logits = _dot(h2, wr_ref[...]) + br_ref[...]
    slots_ref[...] = _allocate(_route_bucket(logits.T), alloc_valid, *alloc_state)


def _mixing_kernel(token_major_in, *refs):
    ins = refs[:N_MIX_IN]
    inv_ref, xs_hbm, tb_ref, meta_ref = refs[N_MIX_IN:N_MIX_IN + 4]
    (kprev_ref, vprev_ref, c3buf_ref, c31buf_ref, ys_a, ys_b, hb_a, hb_b, xk_a, xk_b, ob_ref,
     cnt_ref, cur_ref, gal_ref, tbacc_ref, upper_ref, slotv_a, slotv_b, slot_smem, statev_ref, state_smem,
     ztile_ref, sem, csem, zsem) = refs[N_MIX_IN + 4:]
    carries = (kprev_ref, vprev_ref, c3buf_ref, c31buf_ref)
    alloc_state = (cnt_ref, cur_ref, gal_ref, tbacc_ref, upper_ref)
    s = pl.program_id(0)
    ob_merge = ob_ref.at[lax.rem(s + 2, 3)]
    ob_issue = ob_ref.at[lax.rem(s + 1, 3)]
    tile = jnp.minimum(s, N_TILES - 1)
    j = lax.rem(tile, TILES_PER_SEQ)

    @pl.when((j == 0) & (s < N_TILES))
    def _():
        kprev_ref[...] = jnp.zeros_like(kprev_ref)
        vprev_ref[...] = jnp.zeros_like(vprev_ref)
        c3buf_ref[0:C3_HALO, :] = jnp.zeros((C3_HALO, MIX_W), F32)
        c31buf_ref[0:C31_HALO, :] = jnp.zeros((C31_HALO, MIX_W), F32)

    @pl.when(s == 0)
    def _():
        ys_b[...] = jnp.zeros_like(ys_b)
        hb_b[...] = jnp.zeros_like(hb_b)
        xk_b[...] = jnp.zeros_like(xk_b)
        for ref in (cnt_ref, cur_ref, gal_ref, tbacc_ref, ztile_ref):
            ref[...] = jnp.zeros_like(ref)
        r_i = lax.broadcasted_iota(jnp.int32, (TM, TM), 0)
        c_i = lax.broadcasted_iota(jnp.int32, (TM, TM), 1)
        upper_ref[...] = jnp.where(r_i <= c_i, 1.0, 0.0).astype(BF16)

        def junk_slots(i, carry):
            slot_smem[0, 0, i] = JUNK_SLOT0 + i
            slot_smem[1, 0, i] = JUNK_SLOT0 + i
            return carry
        lax.fori_loop(0, TM, junk_slots, 0)

    valid = s >= 1
    slots_old = slot_smem.at[lax.rem(s, 2)]

    def write_inverse():
        for i in range(TM):
            inv_ref[slots_old[0, i]] = (s - 3) * TM + i

    @pl.when(lax.rem(s, 2) == 0)
    def _():
        _mixing_step(token_major_in, j, *ins, ob_merge, slotv_a, alloc_state, valid, write_inverse, *carries,
                     ys_a, hb_a, xk_a, ys_b, hb_b, xk_b)

    @pl.when(lax.rem(s, 2) == 1)
    def _():
        _mixing_step(token_major_in, j, *ins, ob_merge, slotv_b, alloc_state, valid, write_inverse, *carries,
                     ys_b, hb_b, xk_b, ys_a, hb_a, xk_a)

    def slots_to_smem(parity):
        return pltpu.make_async_copy(slotv_b if parity else slotv_a, slot_smem.at[parity], csem)

    def wait_rows():
        for _ in range(TM // WAIT_ROWS):
            pltpu.make_async_copy(ob_ref.at[0, pl.ds(0, WAIT_ROWS * SUBLANES)],
                                  xs_hbm.at[pl.ds(0, WAIT_ROWS * SUBLANES)], sem).wait()

    def scatter_rows(src, slots):
        def place(i, carry):
            slot = slots[0, i]
            pltpu.make_async_copy(src.at[pl.ds(i * OB_PITCH, SUBLANES)],
                                  xs_hbm.at[pl.ds(slot * SUBLANES, SUBLANES)], sem).start()
            return carry
        lax.fori_loop(0, TM, place, 0, unroll=ISSUE_UNROLL)

    def invert_rows(slots, first_token):
        def place(i, carry):
            inv_ref[slots[0, i]] = first_token + i
            return carry
        lax.fori_loop(0, TM, place, 0, unroll=ISSUE_UNROLL)

    for parity in (0, 1):
        @pl.when((s >= 2) & (lax.rem(s, 2) != parity))
        def _():
            slots_to_smem(parity).wait()
    for parity in (0, 1):
        @pl.when((s >= 1) & (lax.rem(s, 2) == parity))
        def _():
            slots_to_smem(parity).start()

    @pl.when(s >= 3)
    def _():
        wait_rows()

    @pl.when(s >= 2)
    def _():
        scatter_rows(ob_issue, slot_smem.at[lax.rem(s + 1, 2)])

    @pl.when(s == N_TILES)
    def _():
        last = N_TILES - 1
        slots_to_smem(N_TILES % 2).wait()
        wait_rows()
        scatter_rows(ob_ref.at[last % 3], slot_smem.at[N_TILES % 2])
        invert_rows(slot_smem.at[(N_TILES + 1) % 2], (last - 1) * TM)
        invert_rows(slot_smem.at[N_TILES % 2], last * TM)
        wait_rows()
        _finish_sorted_layout(cnt_ref, cur_ref, gal_ref, tbacc_ref, statev_ref, state_smem, ztile_ref,
                              inv_ref, xs_hbm, tb_ref, meta_ref, zsem)


def _allocate(bucket, valid, cnt_ref, cur_ref, gal_ref, tbacc_ref, upper_ref):
    rows = lax.broadcasted_iota(jnp.int32, (NB_PAD, TM), 0)
    onehot = jnp.where(rows == bucket, 1.0, 0.0)
    prefix = _dot(onehot.astype(BF16), upper_ref[...])
    total = prefix[:, TM - 1:TM]
    cnt0, cur0, gal0 = cnt_ref[...], cur_ref[...], gal_ref[...]
    tiles_before = jnp.floor((cnt0 + (TMS - 1)) * (1.0 / TMS))
    tiles_after = jnp.floor((cnt0 + total + (TMS - 1)) * (1.0 / TMS))
    n_new = tiles_after - tiles_before
    rr = lax.broadcasted_iota(jnp.int32, (NB_PAD, NB_PAD), 0)
    cc = lax.broadcasted_iota(jnp.int32, (NB_PAD, NB_PAD), 1)
    strict_lower = jnp.where(cc < rr, 1.0, 0.0).astype(BF16)
    n_new_b = jnp.broadcast_to(n_new, (NB_PAD, LANES))
    first_new = gal0 + _dot(strict_lower, n_new_b.astype(BF16))[:, 0:1]
    rank = cnt0 + prefix - 1.0
    ordinal = jnp.floor(rank * (1.0 / TMS))
    fresh = ordinal - tiles_before
    tile_id = jnp.where(fresh < 0, cur0, first_new + fresh)
    slot = tile_id * TMS + (rank - ordinal * TMS)
    cnt_ref[...] = jnp.where(valid, cnt0 + total, cnt0)
    cur_ref[...] = jnp.where(valid & (n_new > 0), first_new + n_new - 1.0, cur0)
    gal_ref[...] = jnp.where(valid, gal0 + jnp.sum(n_new_b, axis=0, keepdims=True)[:, 0:1], gal0)
    lane = lax.broadcasted_iota(jnp.int32, (NB_PAD, TB_LEN), 1).astype(F32)
    bucket_id = lax.broadcasted_iota(jnp.int32, (NB_PAD, TB_LEN), 0).astype(F32)
    owner = jnp.zeros((NB_PAD, TB_LEN), F32)
    for extra in range(NEW_TILES_MAX):
        owner = owner + jnp.where((lane == first_new + extra) & (n_new > extra), bucket_id, 0.0)
    tbacc_ref[...] += jnp.where(valid, jnp.sum(owner, axis=0, keepdims=True), 0.0)
    return jnp.sum(onehot * slot, axis=0, keepdims=True).astype(jnp.int32)


def _finish_sorted_layout(cnt_ref, cur_ref, gal_ref, tbacc_ref, statev_ref, state_smem, ztile_ref,
                          inv_ref, xs_hbm, tb_ref, meta_ref, zsem):
    col = lax.broadcasted_iota(jnp.int32, (NB_PAD, LANES), 1)
    state = jnp.where(col == 0, cnt_ref[...], jnp.where(col == 1, cur_ref[...], gal_ref[...]))
    statev_ref[...] = state.astype(jnp.int32)
    meta_ref[...] = jnp.broadcast_to(gal_ref[...], (NB_PAD, LANES)).astype(jnp.int32)
    tb_ref[...] = tbacc_ref[...].astype(jnp.int32)
    to_smem = pltpu.make_async_copy(statev_ref, state_smem, zsem)
    to_smem.start()
    to_smem.wait()
    nused = state_smem[0, 2]

    def fill(slot, carry):
        inv_ref[slot] = NTOK + (slot & (JUNK_ROWS - 1))
        return carry

    def zero_rows(first_slot, n_rows):
        return pltpu.make_async_copy(ztile_ref.at[pl.ds(0, n_rows * SUBLANES)],
                                     xs_hbm.at[pl.ds(first_slot * SUBLANES, n_rows * SUBLANES)], zsem)

    for wait in (False, True):
        for k in range(N_BUCKETS):
            used_rows = state_smem[k, 0] & (TMS - 1)
            n_pad = jnp.where(used_rows > 0, TMS - used_rows, 0)
            slot = state_smem[k, 1] * TMS + (TMS - n_pad)
            if not wait:
                lax.fori_loop(slot, slot + n_pad, fill, 0)
            for bit in reversed(range(TMS.bit_length() - 1)):
                piece = n_pad & (1 << bit)

                @pl.when(piece != 0)
                def _():
                    copy = zero_rows(slot, 1 << bit)
                    copy.wait() if wait else copy.start()
                slot = slot + piece

        def unused(t, carry):
            copy = zero_rows(t * TMS, TMS)
            copy.wait() if wait else copy.start()
            return carry
        lax.fori_loop(nused, NT_MAX, unused, 0)
    lax.fori_loop(nused * TMS, NT_MAX * TMS, fill, 0)


def _const_spec(shape):
    zeros = (0,) * len(shape)
    return pl.BlockSpec(shape, lambda s: zeros, pipeline_mode=pl.Buffered(1))


def _mixing_call(x, cos_t, sin_t, n1g, w_in, sgu_g, sgu_b, sgu_w, sgu_bias, qg, kg, sinks,
                 c3w, c31w, c31b, cng, cnb, wbr, wgate, bgate, wo, n2g, wr, br):
    cur = lambda s: jnp.minimum(s, N_TILES - 1)
    tok = lambda width: pl.BlockSpec(
        (None, TM, width), lambda s: (cur(s) // TILES_PER_SEQ, lax.rem(cur(s), TILES_PER_SEQ), 0))
    token_major_in = x.ndim == 2
    in_specs = [
        pl.BlockSpec((TM * SUBLANES, LANES), lambda s: (cur(s), 0)) if token_major_in else tok(D_MODEL),
        tok(LANES), tok(LANES),
        _const_spec((1, D_MODEL)), _const_spec((D_MODEL, D_IN)),
        _const_spec((1, MIX_W)), _const_spec((1, MIX_W)),
        _const_spec((SGU_CHUNK, SGU_GROUPS * SGU_CHUNK)), _const_spec((SGU_CHUNK, MIX_W)),
        _const_spec((1, N_Q_HEADS * HEAD_DIM)), _const_spec((1, N_KV_HEADS * HEAD_DIM)),
        pl.BlockSpec(memory_space=pltpu.SMEM),
        _const_spec((SUBLANES, MIX_W)), _const_spec((C31_HALO, MIX_W)), _const_spec((1, MIX_W)),
        _const_spec((1, MIX_W)), _const_spec((1, MIX_W)),
        _const_spec((N_BRANCH, MIX_W, D_MODEL)), _const_spec((D_MODEL, N_BRANCH * D_MODEL)),
        _const_spec((1, N_BRANCH * D_MODEL)), _const_spec((D_MODEL, D_MODEL)),
        _const_spec((1, D_MODEL)), _const_spec((D_MODEL, LANES)), _const_spec((1, LANES)),
    ]
    return pl.pallas_call(
        functools.partial(_mixing_kernel, token_major_in),
        grid=(N_TILES + 1,),
        in_specs=in_specs,
        out_specs=[pl.BlockSpec(memory_space=pltpu.SMEM), pl.BlockSpec(memory_space=pl.ANY),
                   pl.BlockSpec((1, TB_LEN), lambda s: (0, 0)), pl.BlockSpec((NB_PAD, LANES), lambda s: (0, 0))],
        out_shape=[jax.ShapeDtypeStruct((JUNK_SLOT0 + TM,), jnp.int32),
                   jax.ShapeDtypeStruct((NT_MAX * TMS * SUBLANES, LANES), F32),
                   jax.ShapeDtypeStruct((1, TB_LEN), jnp.int32),
                   jax.ShapeDtypeStruct((NB_PAD, LANES), jnp.int32)],
        scratch_shapes=[
            pltpu.VMEM((WINDOW, LANES), F32), pltpu.VMEM((WINDOW, LANES), F32),
            pltpu.VMEM((C3_HALO + TM, MIX_W), F32), pltpu.VMEM((C31_HALO + TM, MIX_W), F32),
            pltpu.VMEM((N_BRANCH, TM, MIX_W), BF16), pltpu.VMEM((N_BRANCH, TM, MIX_W), BF16),
            pltpu.VMEM((TM, D_MODEL), BF16), pltpu.VMEM((TM, D_MODEL), BF16),
            pltpu.VMEM((TM, D_MODEL), F32), pltpu.VMEM((TM, D_MODEL), F32),
            pltpu.VMEM((3, TM * OB_PITCH, LANES), F32),
            pltpu.VMEM((NB_PAD, 1), F32), pltpu.VMEM((NB_PAD, 1), F32), pltpu.VMEM((NB_PAD, 1), F32),
            pltpu.VMEM((1, TB_LEN), F32), pltpu.VMEM((TM, TM), BF16),
            pltpu.VMEM((1, TM), jnp.int32), pltpu.VMEM((1, TM), jnp.int32), pltpu.SMEM((2, 1, TM), jnp.int32),
            pltpu.VMEM((NB_PAD, LANES), jnp.int32), pltpu.SMEM((NB_PAD, LANES), jnp.int32),
            pltpu.VMEM((TMS * SUBLANES, LANES), F32),
            pltpu.SemaphoreType.DMA(()), pltpu.SemaphoreType.DMA(()), pltpu.SemaphoreType.DMA(()),
        ],
        compiler_params=pltpu.CompilerParams(
            dimension_semantics=("arbitrary",), vmem_limit_bytes=VMEM_LIMIT, has_side_effects=True),
        name="mixing_block",
    )(x, cos_t, sin_t, n1g, w_in, sgu_g, sgu_b, sgu_w, sgu_bias, qg, kg, sinks,
      c3w, c31w, c31b, cng, cnb, wbr, wgate, bgate, wo, n2g, wr, br)


def _route_bucket(logits_t):
    g = [logits_t[i:i + 1, :] for i in range(N_GROUPS)]
    gmax, gidx = g[0], jnp.zeros(g[0].shape, jnp.int32)
    for i in range(1, N_GROUPS):
        better = g[i] > gmax
        gmax = jnp.where(better, g[i], gmax)
        gidx = jnp.where(better, i, gidx)
    e = []
    for i in range(EXPERTS_PER_GROUP):
        ei = logits_t[8 + i:9 + i, :]
        for grp in range(1, N_GROUPS):
            row = 8 + grp * EXPERTS_PER_GROUP + i
            ei = jnp.where(gidx == grp, logits_t[row:row + 1, :], ei)
        e.append(ei)
    v1, i1 = e[0], jnp.zeros_like(gidx)
    for i in range(1, EXPERTS_PER_GROUP):
        better = e[i] > v1
        v1 = jnp.where(better, e[i], v1)
        i1 = jnp.where(better, i, i1)
    v2, i2 = jnp.full_like(v1, -jnp.inf), jnp.zeros_like(gidx)
    for i in range(EXPERTS_PER_GROUP):
        better = (e[i] > v2) & (i1 != i)
        v2 = jnp.where(better, e[i], v2)
        i2 = jnp.where(better, i, i2)
    lo, hi = jnp.minimum(i1, i2), jnp.maximum(i1, i2)
    pair = jnp.where(lo == 0, hi - 1, jnp.where(lo == 1, hi + 1, 5))
    return gidx * PAIRS_PER_GROUP + pair


def _bucket_experts(b):
    g = b // PAIRS_PER_GROUP
    pair = b - g * PAIRS_PER_GROUP
    lo = jnp.where(pair < 3, 0, jnp.where(pair < 5, 1, 2))
    hi = jnp.where(pair < 3, pair + 1, jnp.where(pair < 5, pair - 1, 3))
    return g, g * EXPERTS_PER_GROUP + lo, g * EXPERTS_PER_GROUP + hi


def _load_token_major(ref, rows):
    return jnp.concatenate([ref[pl.ds(c, rows, stride=SUBLANES), :] for c in range(SUBLANES)], axis=1)


def _store_token_major(ref, value, rows, pitch=SUBLANES):
    for c in range(SUBLANES):
        ref[pl.ds(c, rows, stride=pitch), :] = value[:, c * LANES:(c + 1) * LANES]


def _to_rows_kernel(x_ref, out_ref):
    for p in range(TO_ROWS_TM // TM):
        piece = x_ref.at[pl.ds(p * TM * SUBLANES, TM * SUBLANES)]
        out_ref[p * TM:(p + 1) * TM, :] = _load_token_major(piece, TM)


def _to_rows_call(x_tm, ntok):
    return pl.pallas_call(
        _to_rows_kernel,
        grid=(ntok // TO_ROWS_TM,),
        in_specs=[pl.BlockSpec((TO_ROWS_TM * SUBLANES, LANES), lambda i: (i, 0))],
        out_specs=pl.BlockSpec((TO_ROWS_TM, D_MODEL), lambda i: (i, 0)),
        out_shape=jax.ShapeDtypeStruct((ntok, D_MODEL), F32),
        compiler_params=pltpu.CompilerParams(dimension_semantics=("arbitrary",), vmem_limit_bytes=VMEM_LIMIT),
        name="token_major_to_rows",
    )(x_tm)


def _moe_kernel(tb_ref, nused_ref, inv_ref, x_ref, n2g_ref, wr_ref, br_ref,
                wg_lo_ref, wu_lo_ref, wd_lo_ref, wg_hi_ref, wu_hi_ref, wd_hi_ref,
                out_hbm, obuf, ssem):
    i = pl.program_id(0)
    nused = nused_ref[0]
    slot = lax.rem(i, 2)

    def scatter_row(r, s):
        tok = inv_ref[0, r]
        return pltpu.make_async_copy(obuf.at[s, pl.ds(r * OB_PITCH, SUBLANES)],
                                     out_hbm.at[pl.ds(tok * SUBLANES, SUBLANES)], ssem.at[s])

    def start_rows(row_copy):
        def body(r, carry):
            row_copy(r).start()
            return carry
        lax.fori_loop(0, TMS, body, 0, unroll=ISSUE_UNROLL)

    def wait_tile(sem, s):
        for _ in range(TMS // WAIT_ROWS):
            pltpu.make_async_copy(obuf.at[s, pl.ds(0, WAIT_ROWS * SUBLANES)],
                                  out_hbm.at[pl.ds(0, WAIT_ROWS * SUBLANES)], sem.at[s]).wait()

    @pl.when(i == 0)
    def _():
        obuf[...] = jnp.zeros_like(obuf)
        for s in range(2):
            junk = pltpu.make_async_copy(
                obuf.at[s, pl.ds(0, TMS * SUBLANES)],
                out_hbm.at[pl.ds((NTOK + s * TMS) * SUBLANES, TMS * SUBLANES)], ssem.at[s])
            junk.start()
            junk.wait()

    @pl.when(i < nused)
    def _():
        @pl.when(i >= 2)
        def _():
            wait_tile(ssem, slot)
        x = _load_token_major(x_ref, TMS)
        ms = jnp.mean(x * x, axis=-1, keepdims=True)
        hb = (x * lax.rsqrt(ms + EPS) * n2g_ref[...]).astype(BF16)
        logits = _dot(hb, wr_ref[...]) + br_ref[...]
        g, e_lo, e_hi = _bucket_experts(tb_ref[i])
        lane = lax.broadcasted_iota(jnp.int32, (TMS, LANES), 1)
        pick = lambda l: jnp.sum(jnp.where(lane == l, logits, 0.0), axis=-1, keepdims=True)
        lg, l_lo, l_hi = pick(g), pick(8 + e_lo), pick(8 + e_hi)
        gsum = jnp.sum(jnp.where(lane < N_GROUPS, jnp.exp(logits - lg), 0.0), axis=-1, keepdims=True)
        g_w = 1.0 / gsum
        w_lo = g_w / (1.0 + jnp.exp(l_hi - l_lo))
        w_hi = g_w / (1.0 + jnp.exp(l_lo - l_hi))

        def expert(wg_ref, wu_ref, wd_ref, w):
            a = _dot(hb, wg_ref[...])
            hid = a * _sigmoid(a) * _dot(hb, wu_ref[...]) * w
            return _dot(hid.astype(BF16), wd_ref[...])
        y = (x + expert(wg_lo_ref, wu_lo_ref, wd_lo_ref, w_lo)
             + expert(wg_hi_ref, wu_hi_ref, wd_hi_ref, w_hi))
        _store_token_major(obuf.at[slot], y, TMS, pitch=OB_PITCH)

        start_rows(lambda r: scatter_row(r, slot))

    @pl.when(i == nused - 1)
    def _():
        wait_tile(ssem, slot)

        @pl.when(i >= 1)
        def _():
            wait_tile(ssem, 1 - slot)


def _moe_call(tile_bucket, nused, inv, xs_tm, n2g, wr, br, wg, wu, wd):
    def used(i, nu):
        return jnp.minimum(i, nu[0] - 1)

    def w_spec(shape, which):
        def index(i, tb, nu):
            return (_bucket_experts(tb[used(i, nu)])[which], 0, 0)
        return pl.BlockSpec((None,) + shape, index)

    const = lambda shape: pl.BlockSpec(shape, lambda i, tb, nu: (0, 0))
    up, down = (D_MODEL, D_EXPERT), (D_EXPERT, D_MODEL)
    inv3 = inv[:JUNK_SLOT0].reshape(NT_MAX, 1, TMS)
    return pl.pallas_call(
        _moe_kernel,
        grid_spec=pltpu.PrefetchScalarGridSpec(
            num_scalar_prefetch=2,
            grid=(NT_MAX,),
            in_specs=[
                pl.BlockSpec((None, 1, TMS), lambda i, tb, nu: (i, 0, 0), memory_space=pltpu.SMEM),
                pl.BlockSpec((TMS * SUBLANES, LANES), lambda i, tb, nu: (used(i, nu), 0)),
                const((1, D_MODEL)), const((D_MODEL, LANES)), const((1, LANES)),
                w_spec(up, 1), w_spec(up, 1), w_spec(down, 1),
                w_spec(up, 2), w_spec(up, 2), w_spec(down, 2),
            ],
            out_specs=pl.BlockSpec(memory_space=pl.ANY),
            scratch_shapes=[pltpu.VMEM((2, TMS * OB_PITCH, LANES), F32), pltpu.SemaphoreType.DMA((2,))],
        ),
        out_shape=jax.ShapeDtypeStruct(((NTOK + JUNK_ROWS) * SUBLANES, LANES), F32),
        compiler_params=pltpu.CompilerParams(
            dimension_semantics=("arbitrary",), vmem_limit_bytes=VMEM_LIMIT, has_side_effects=True),
        name="hier_moe",
    )(tile_bucket, nused, inv3, xs_tm, n2g, wr, br, wg, wu, wd, wg, wu, wd)


def _rope_tables(positions):
    inv = ROPE_THETA ** (-jnp.arange(0, ROT_DIM, 2, dtype=F32) / ROT_DIM)
    ang = positions.astype(F32)[..., None] * inv
    half = ROT_DIM // 2
    dim = jnp.arange(LANES) % HEAD_DIM
    rotated = dim < ROT_DIM
    expand = ((jnp.arange(half)[:, None] == dim[None, :] % half) & rotated[None, :]).astype(F32)
    spread = lambda t: jnp.dot(t.reshape(-1, half), expand,
                               precision=lax.Precision.HIGHEST).reshape(t.shape[:-1] + (LANES,))
    return spread(jnp.cos(ang)) + (~rotated).astype(F32), spread(jnp.sin(ang))


def kernel(x, positions, norm1_g, w_in, sgu_ln_g, sgu_ln_b, sgu_w, sgu_b, q_norm_g, k_norm_g, sinks, conv3_w, conv31_w, conv31_b, cnorm_g, cnorm_b, w_branch, w_gate, b_gate, w_o, norm2_g, w_group, b_group, w_expert, b_expert, w_e_gate, w_e_up, w_e_down):
    bsz, seq, d = x.shape
    assert (bsz, seq, d) == (BATCH, SEQ, D_MODEL)
    ntok = bsz * seq
    depth = norm1_g.shape[0]
    cos_t, sin_t = _rope_tables(positions)
    row = lambda t: t.reshape(1, -1)
    for l in range(depth):
        sgu_wcat = jnp.transpose(sgu_w[l], (1, 0, 2)).reshape(SGU_CHUNK, SGU_GROUPS * SGU_CHUNK)
        sgu_bias = jnp.repeat(sgu_b[l].T, MIX_W // SGU_GROUPS, axis=1)
        c3w = jnp.zeros((SUBLANES, MIX_W), F32).at[:SHORT_CONV].set(conv3_w[l])
        c31w = jnp.zeros((C31_HALO, MIX_W), F32).at[:CONFORMER_CONV].set(conv31_w[l])
        wr = jnp.zeros((d, LANES), F32).at[:, 0:N_GROUPS].set(w_group[l]).at[:, 8:8 + N_EXPERTS].set(w_expert[l])
        wr = wr.astype(BF16)
        br = jnp.zeros((1, LANES), F32).at[0, 0:N_GROUPS].set(b_group[l]).at[0, 8:8 + N_EXPERTS].set(b_expert[l])
        inv, xs, tile_bucket, meta = _mixing_call(
            x, cos_t, sin_t, row(norm1_g[l]), w_in[l].astype(BF16),
            row(sgu_ln_g[l]), row(sgu_ln_b[l]), sgu_wcat, sgu_bias,
            row(jnp.tile(q_norm_g[l], N_Q_HEADS)), row(jnp.tile(k_norm_g[l], N_KV_HEADS)), sinks[l],
            c3w, c31w, row(conv31_b[l]), row(cnorm_g[l]), row(cnorm_b[l]),
            w_branch[l].astype(BF16), (0.5 * w_gate[l]).astype(BF16), row(0.5 * b_gate[l]), w_o[l].astype(BF16),
            row(norm2_g[l]), wr, br)
        x = _moe_call(tile_bucket[0], meta[0, 0:1], inv, xs, row(norm2_g[l]), wr, br,
                      w_e_gate[l].astype(BF16), w_e_up[l].astype(BF16), w_e_down[l].astype(BF16))
    return _to_rows_call(x, ntok).reshape(bsz, seq, d)
```
